```python
import jax, jax.numpy as jnp
from jax import lax
import numpy as np

D_MODEL = 2048
BATCH = 1
SEQ = 8192
DEPTH = 4

HEAD_DIM = 128
D_MIX = D_MODEL
N_MIX_HEADS = D_MIX // HEAD_DIM
N_GMLP_GROUPS = N_MIX_HEADS // 4
N_RET_HEADS = (N_MIX_HEADS - N_GMLP_GROUPS) // 2
N_MLA_HEADS = N_MIX_HEADS - N_GMLP_GROUPS - N_RET_HEADS
RET_WIDTH = N_RET_HEADS * HEAD_DIM
MLA_WIDTH = N_MLA_HEADS * HEAD_DIM
GMLP_WIDTH = N_GMLP_GROUPS * HEAD_DIM

Q_LORA_RANK = D_MODEL // 4
KV_LORA_RANK = 512
QK_NOPE_DIM = 128
QK_ROPE_DIM = 64
V_HEAD_DIM = HEAD_DIM
QK_HEAD_DIM = QK_NOPE_DIM + QK_ROPE_DIM

CHUNK = 128
BLOCK_Q = 128
D_FF = ((8 * D_MODEL // 3 + 255) // 256) * 256
ROPE_BASE = 10000.0
EPS = 1e-6

IN_SIZES = [RET_WIDTH, RET_WIDTH, RET_WIDTH, RET_WIDTH,
            Q_LORA_RANK, KV_LORA_RANK, QK_ROPE_DIM,
            GMLP_WIDTH, GMLP_WIDTH]
IN_SPLITS = [int(s) for s in np.cumsum(IN_SIZES)[:-1]]
D_IN = int(sum(IN_SIZES))

kernel_name = "hybrid_retention_mla_gmlp_swiglu"


def rms_norm(x, gain):
    xf = x.astype(jnp.float32)
    y = xf * lax.rsqrt(jnp.mean(xf * xf, axis=-1, keepdims=True) + EPS)
    return (y * gain.astype(jnp.float32)).astype(x.dtype)


def rope_tables(positions, dim):
    inv = 1.0 / (ROPE_BASE ** (jnp.arange(0, dim, 2, dtype=jnp.float32) / dim))
    ang = positions.astype(jnp.float32)[..., None] * inv
    return jnp.cos(ang)[:, :, None, :], jnp.sin(ang)[:, :, None, :]


def apply_rope(x, cos, sin):
    x1, x2 = jnp.split(x.astype(jnp.float32), 2, axis=-1)
    return jnp.concatenate([x1 * cos - x2 * sin, x2 * cos + x1 * sin], axis=-1).astype(x.dtype)


def retention(q, k, v):
    B, S, H, Dh = q.shape
    n = S // CHUNK
    log_gamma = jnp.log1p(-jnp.exp2(-5.0 - jnp.arange(H, dtype=jnp.float32)))
    idx = jnp.arange(CHUNK, dtype=jnp.float32)
    rel = idx[:, None] - idx[None, :]
    inner_decay = jnp.where(rel >= 0, jnp.exp(log_gamma[:, None, None] * jnp.maximum(rel, 0.0)), 0.0)
    q_decay = jnp.exp(log_gamma[None, :] * (idx[:, None] + 1.0))
    k_decay = jnp.exp(log_gamma[None, :] * (CHUNK - 1.0 - idx[:, None]))
    chunk_decay = jnp.exp(log_gamma * CHUNK)
    qc = q.reshape(B, n, CHUNK, H, Dh)
    kc = k.reshape(B, n, CHUNK, H, Dh) * (Dh ** -0.5)
    vc = v.reshape(B, n, CHUNK, H, Dh)
    scores = jnp.einsum('bnihd,bnjhd->bnhij', qc, kc) * inner_decay.astype(q.dtype)
    inner = jnp.einsum('bnhij,bnjhd->bnihd', scores, vc)
    kv = jnp.einsum('bnjhd,bnjhe->bnhde', kc * k_decay[:, :, None].astype(k.dtype), vc).astype(jnp.float32)

    def step(state, kv_n):
        return chunk_decay[None, :, None, None] * state + kv_n, state

    _, prev = lax.scan(step, jnp.zeros((B, H, Dh, Dh), jnp.float32), jnp.moveaxis(kv, 1, 0))
    prev = jnp.moveaxis(prev, 0, 1).astype(q.dtype)
    cross = jnp.einsum('bnihd,bnhde->bnihe', qc * q_decay[:, :, None].astype(q.dtype), prev)
    return (inner + cross).reshape(B, S, H, Dh)


def causal_block_attention(q, k, v):
    B, S, H, Dqk = q.shape
    Dv = v.shape[-1]
    nb = S // BLOCK_Q
    scale = Dqk ** -0.5
    qb = jnp.moveaxis(q.reshape(B, nb, BLOCK_Q, H, Dqk), 1, 0)
    k_pos = jnp.arange(S)

    def one_block(args):
        i, q_blk = args
        s = jnp.einsum('bqhd,bkhd->bhqk', q_blk, k).astype(jnp.float32) * scale
        q_pos = i * BLOCK_Q + jnp.arange(BLOCK_Q)
        s = jnp.where(k_pos[None, :] <= q_pos[:, None], s, -jnp.inf)
        p = jax.nn.softmax(s, axis=-1).astype(v.dtype)
        return jnp.einsum('bhqk,bkhd->bqhd', p, v)

    out = lax.map(one_block, (jnp.arange(nb), qb))
    return jnp.moveaxis(out, 0, 1).reshape(B, S, H, Dv)


def chunked_spatial_gating(u, v, v_gain, w_s, b_s):
    B, S, G, Dg = v.shape
    n = S // CHUNK
    v = rms_norm(v, v_gain.reshape(G, Dg))
    mask = jnp.tril(jnp.ones((CHUNK, CHUNK), dtype=bool))
    w = jnp.where(mask[None], w_s, 0.0).astype(v.dtype)
    vc = v.reshape(B, n, CHUNK, G, Dg)
    s = jnp.einsum('gts,bnsgc->bntgc', w, vc) + b_s.T[None, None, :, :, None].astype(v.dtype)
    return u * s.reshape(B, S, G, Dg)


def hybrid_layer(x, rope_ret, rope_mla, attn_norm, w_in, mla_q_norm, w_uq, mla_kv_norm, w_ukv,
                 gmlp_v_norm, gmlp_w_s, gmlp_b_s, mix_norm, w_out, ffn_norm, w_gate, w_up, w_down):
    B, S, _ = x.shape
    h = rms_norm(x, attn_norm)
    z = h @ w_in
    r_q, r_k, r_v, r_g, c_q, c_kv, k_pe, g_u, g_v = jnp.split(z, IN_SPLITS, axis=-1)

    cos_r, sin_r = rope_ret
    rq = apply_rope(r_q.reshape(B, S, N_RET_HEADS, HEAD_DIM), cos_r, sin_r)
    rk = apply_rope(r_k.reshape(B, S, N_RET_HEADS, HEAD_DIM), cos_r, sin_r)
    y_ret = retention(rq, rk, r_v.reshape(B, S, N_RET_HEADS, HEAD_DIM))

    cos_m, sin_m = rope_mla
    mq = (rms_norm(c_q, mla_q_norm) @ w_uq).reshape(B, S, N_MLA_HEADS, QK_HEAD_DIM)
    q_nope, q_pe = jnp.split(mq, [QK_NOPE_DIM], axis=-1)
    mq = jnp.concatenate([q_nope, apply_rope(q_pe, cos_m, sin_m)], axis=-1)
    mkv = (rms_norm(c_kv, mla_kv_norm) @ w_ukv).reshape(B, S, N_MLA_HEADS, QK_NOPE_DIM + V_HEAD_DIM)
    k_nope, mv = jnp.split(mkv, [QK_NOPE_DIM], axis=-1)
    k_pe = apply_rope(k_pe.reshape(B, S, 1, QK_ROPE_DIM), cos_m, sin_m)
    mk = jnp.concatenate([k_nope, jnp.broadcast_to(k_pe, (B, S, N_MLA_HEADS, QK_ROPE_DIM))], axis=-1)
    y_mla = causal_block_attention(mq, mk, mv)

    gu = jax.nn.gelu(g_u).reshape(B, S, N_GMLP_GROUPS, HEAD_DIM)
    gv = jax.nn.gelu(g_v).reshape(B, S, N_GMLP_GROUPS, HEAD_DIM)
    y_gm = chunked_spatial_gating(gu, gv, gmlp_v_norm, gmlp_w_s, gmlp_b_s)

    y = jnp.concatenate([y_ret, y_mla, y_gm], axis=2)
    y = rms_norm(y, mix_norm.reshape(N_MIX_HEADS, HEAD_DIM)).reshape(B, S, D_MIX)
    y = jnp.concatenate([y[..., :RET_WIDTH] * jax.nn.silu(r_g), y[..., RET_WIDTH:]], axis=-1)
    x = x + y @ w_out

    h = rms_norm(x, ffn_norm)
    return x + (jax.nn.silu(h @ w_gate) * (h @ w_up)) @ w_down


def setup_inputs(seed: int = 0) -> dict:
    key = jax.random.key(seed)
    ks = jax.random.split(key, 20)
    f32 = jnp.float32

    def w(k, shape, fan_in):
        return jax.random.normal(k, shape, f32) * (fan_in ** -0.5)

    def gain(k, shape):
        return 1.0 + 0.02 * jax.random.normal(k, shape, f32)

    L = DEPTH
    tri = jnp.tril(jnp.ones((CHUNK, CHUNK), f32))
    w_s = jax.random.normal(ks[10], (L, N_GMLP_GROUPS, CHUNK, CHUNK), f32) * (CHUNK ** -0.5) * tri
    return {
        "x": jax.random.normal(ks[0], (BATCH, SEQ, D_MODEL), f32),
        "positions": jnp.broadcast_to(jnp.arange(SEQ, dtype=jnp.int32), (BATCH, SEQ)),
        "attn_norm": gain(ks[1], (L, D_MODEL)),
        "w_in": w(ks[2], (L, D_MODEL, D_IN), D_MODEL),
        "mla_q_norm": gain(ks[3], (L, Q_LORA_RANK)),
        "w_uq": w(ks[4], (L, Q_LORA_RANK, N_MLA_HEADS * QK_HEAD_DIM), Q_LORA_RANK),
        "mla_kv_norm": gain(ks[5], (L, KV_LORA_RANK)),
        "w_ukv": w(ks[6], (L, KV_LORA_RANK, N_MLA_HEADS * (QK_NOPE_DIM + V_HEAD_DIM)), KV_LORA_RANK),
        "gmlp_v_norm": gain(ks[7], (L, GMLP_WIDTH)),
        "gmlp_w_s": w_s,
        "gmlp_b_s": 1.0 + 0.02 * jax.random.normal(ks[11], (L, N_GMLP_GROUPS, CHUNK), f32),
        "mix_norm": gain(ks[12], (L, D_MIX)),
        "w_out": w(ks[13], (L, D_MIX, D_MODEL), D_MIX),
        "ffn_norm": gain(ks[14], (L, D_MODEL)),
        "w_gate": w(ks[15], (L, D_MODEL, D_FF), D_MODEL),
        "w_up": w(ks[16], (L, D_MODEL, D_FF), D_MODEL),
        "w_down": w(ks[17], (L, D_FF, D_MODEL), D_FF),
        "final_norm": gain(ks[18], (D_MODEL,)),
    }


def reference(x, positions, attn_norm, w_in, mla_q_norm, w_uq, mla_kv_norm, w_ukv, gmlp_v_norm,
              gmlp_w_s, gmlp_b_s, mix_norm, w_out, ffn_norm, w_gate, w_up, w_down, final_norm):
    rope_ret = rope_tables(positions, HEAD_DIM)
    rope_mla = rope_tables(positions, QK_ROPE_DIM)
    for l in range(DEPTH):
        x = hybrid_layer(x, rope_ret, rope_mla, attn_norm[l], w_in[l], mla_q_norm[l], w_uq[l],
                         mla_kv_norm[l], w_ukv[l], gmlp_v_norm[l], gmlp_w_s[l], gmlp_b_s[l],
                         mix_norm[l], w_out[l], ffn_norm[l], w_gate[l], w_up[l], w_down[l])
    return rms_norm(x, final_norm)
```

```python
import functools

import jax
import jax.numpy as jnp
import numpy as np
from jax import lax
from jax.experimental import pallas as pl
from jax.experimental.pallas import tpu as pltpu

F32 = jnp.float32
BF16 = jnp.bfloat16

HEAD_DIM = 128
N_RET_HEADS = 6
N_MLA_HEADS = 6
N_GMLP_GROUPS = 4
RET_WIDTH = N_RET_HEADS * HEAD_DIM
MLA_WIDTH = N_MLA_HEADS * HEAD_DIM
GMLP_WIDTH = N_GMLP_GROUPS * HEAD_DIM
Q_LORA_RANK = 512
KV_LORA_RANK = 512
QK_NOPE_DIM = 128
QK_ROPE_DIM = 64
QK_HEAD_DIM = QK_NOPE_DIM + QK_ROPE_DIM
CHUNK = 128
ROPE_BASE = 10000.0
EPS = 1e-6

LANES = 128
QK_PAD = 2 * LANES

Z_RET = 0
Z_CQ = 4 * RET_WIDTH
Z_CKV = Z_CQ + Q_LORA_RANK
Z_GU = Z_CKV + KV_LORA_RANK
Z_GV = Z_GU + GMLP_WIDTH
Z_KPE = Z_GV + GMLP_WIDTH
Z_TN = 768
Z_WIDTH = 7 * Z_TN

VMEM_LIMIT = 56 * 1024 * 1024


def _cparams(*sem):
    return pltpu.CompilerParams(dimension_semantics=sem, vmem_limit_bytes=VMEM_LIMIT)


def _rms(x, gain):
    return x * lax.rsqrt(jnp.mean(x * x, axis=-1, keepdims=True) + EPS) * gain


def _rope(x, cos, sin):
    return x * cos + pltpu.roll(x, 64, 1) * sin


def _rope_table_kernel(pos_ref, inv_r_ref, sgn_r_ref, inv_m_ref, cm_ref, sm_ref,
                       cos_r_ref, sin_r_ref, cos_m_ref, sin_m_ref):
    pos = pos_ref[...].astype(F32)
    ang_r = pos * inv_r_ref[...]
    cos_r_ref[...] = jnp.cos(ang_r)
    sin_r_ref[...] = jnp.sin(ang_r) * sgn_r_ref[...]
    ang_m = pos * inv_m_ref[...]
    cos_m_ref[...] = jnp.cos(ang_m) * cm_ref[...]
    sin_m_ref[...] = jnp.sin(ang_m) * sm_ref[...]


def _rope_tables(positions):
    s = positions.shape[-1]
    ts = min(s, 1024)
    pos = positions.reshape(s, 1)
    inv_r = 1.0 / (ROPE_BASE ** (jnp.arange(0, HEAD_DIM, 2, dtype=F32) / HEAD_DIM))
    inv_m = 1.0 / (ROPE_BASE ** (jnp.arange(0, QK_ROPE_DIM, 2, dtype=F32) / QK_ROPE_DIM))
    z32 = jnp.zeros((32,), F32)
    o32 = jnp.ones((32,), F32)
    inv_r_full = jnp.concatenate([inv_r, inv_r]).reshape(1, LANES)
    sgn_r = jnp.concatenate([-jnp.ones((64,), F32), jnp.ones((64,), F32)]).reshape(1, LANES)
    inv_m_full = jnp.concatenate([inv_m, z32, inv_m, z32]).reshape(1, LANES)
    cmask = jnp.concatenate([o32, z32, o32, z32]).reshape(1, LANES)
    smask = jnp.concatenate([-o32, z32, o32, z32]).reshape(1, LANES)
    row = pl.BlockSpec((1, LANES), lambda i: (0, 0))
    tab = pl.BlockSpec((ts, LANES), lambda i: (i, 0))
    out = jax.ShapeDtypeStruct((s, LANES), F32)
    return pl.pallas_call(
        _rope_table_kernel,
        grid=(s // ts,),
        in_specs=[pl.BlockSpec((ts, 1), lambda i: (i, 0)), row, row, row, row, row],
        out_specs=[tab, tab, tab, tab],
        out_shape=[out, out, out, out],
        compiler_params=_cparams("arbitrary"),
        name="rope_tables",
    )(pos, inv_r_full, sgn_r, inv_m_full, cmask, smask)


def _in_proj_kernel(x_ref, g_ref, w_ref, o_ref, h_sc):
    @pl.when(pl.program_id(1) == 0)
    def _():
        h_sc[...] = _rms(x_ref[...], g_ref[...]).astype(BF16)

    o_ref[...] = jnp.dot(h_sc[...], w_ref[...], preferred_element_type=F32).astype(o_ref.dtype)


def _in_proj(x, gain, w, tm):
    s, d = x.shape
    n = w.shape[1]
    return pl.pallas_call(
        _in_proj_kernel,
        grid=(s // tm, n // Z_TN),
        in_specs=[pl.BlockSpec((tm, d), lambda i, j: (i, 0)),
                  pl.BlockSpec((1, d), lambda i, j: (0, 0)),
                  pl.BlockSpec((d, Z_TN), lambda i, j: (0, j))],
        out_specs=pl.BlockSpec((tm, Z_TN), lambda i, j: (i, j)),
        out_shape=jax.ShapeDtypeStruct((s, n), BF16),
        scratch_shapes=[pltpu.VMEM((tm, d), BF16)],
        compiler_params=_cparams("arbitrary", "arbitrary"),
        name="in_proj",
    )(x, gain, w)


def _retention_kernel(q_ref, k_ref, v_ref, g_ref, cos_ref, sin_ref, inner_ref, qdec_ref, kdec_ref,
                      cdec_ref, gain_ref, o_ref, state_sc, *, chunk, n_chunks):
    @pl.when(pl.program_id(0) == 0)
    def _():
        state_sc[...] = jnp.zeros_like(state_sc)

    for c in range(n_chunks):
        rows = slice(c * chunk, (c + 1) * chunk)
        cos = cos_ref[rows, :]
        sin = sin_ref[rows, :]
        for h in range(N_RET_HEADS):
            cols = slice(h * HEAD_DIM, (h + 1) * HEAD_DIM)
            q = _rope(q_ref[rows, cols].astype(F32), cos, sin)
            k = _rope(k_ref[rows, cols].astype(F32), cos, sin) * (HEAD_DIM ** -0.5)
            v = v_ref[rows, cols]
            state = state_sc[h]
            scores = lax.dot_general(q.astype(BF16), k.astype(BF16), (((1,), (1,)), ((), ())),
                                     preferred_element_type=F32) * inner_ref[h]
            out = jnp.dot(scores.astype(BF16), v, preferred_element_type=F32)
            out += jnp.dot((q * qdec_ref[:, cols]).astype(BF16), state.astype(BF16),
                           preferred_element_type=F32)
            kt = (k * kdec_ref[:, cols]).T.astype(BF16)
            state_sc[h] = cdec_ref[:, cols] * state + jnp.dot(kt, v, preferred_element_type=F32)
            y = _rms(out, gain_ref[:, cols])
            gate = g_ref[rows, cols].astype(F32)
            o_ref[rows, cols] = (y * (gate / (1.0 + jnp.exp(-gate)))).astype(o_ref.dtype)


def _retention_tables(chunk):
    h = N_RET_HEADS
    log_gamma = jnp.log1p(-jnp.exp2(-5.0 - jnp.arange(h, dtype=F32)))
    idx = jnp.arange(chunk, dtype=F32)
    rel = idx[:, None] - idx[None, :]
    inner = jnp.where(rel >= 0, jnp.exp(log_gamma[:, None, None] * jnp.maximum(rel, 0.0)), 0.0)
    qdec = jnp.exp(log_gamma[None, :] * (idx[:, None] + 1.0))
    kdec = jnp.exp(log_gamma[None, :] * (chunk - 1.0 - idx[:, None]))
    cdec = jnp.exp(log_gamma * chunk)
    rep = lambda a: jnp.repeat(a, HEAD_DIM, axis=-1)
    return inner, rep(qdec), rep(kdec), rep(cdec[None, :])


def _retention(z, cos_r, sin_r, gain, t, chunk):
    s = z.shape[0]
    inner, qdec, kdec, cdec = _retention_tables(chunk)
    zspec = lambda c: pl.BlockSpec((t, RET_WIDTH), lambda i: (i, c))
    tab = pl.BlockSpec((t, LANES), lambda i: (i, 0))
    full = lambda a: pl.BlockSpec(a.shape, lambda i: (0,) * a.ndim)
    return pl.pallas_call(
        functools.partial(_retention_kernel, chunk=chunk, n_chunks=t // chunk),
        grid=(s // t,),
        in_specs=[zspec(0), zspec(1), zspec(2), zspec(3), tab, tab,
                  full(inner), full(qdec), full(kdec), full(cdec),
                  pl.BlockSpec((1, RET_WIDTH), lambda i: (0, 0))],
        out_specs=pl.BlockSpec((t, RET_WIDTH), lambda i: (i, 0)),
        out_shape=jax.ShapeDtypeStruct((s, RET_WIDTH), BF16),
        scratch_shapes=[pltpu.VMEM((N_RET_HEADS, HEAD_DIM, HEAD_DIM), F32)],
        compiler_params=_cparams("arbitrary"),
        name="retention",
    )(z, z, z, z, cos_r, sin_r, inner, qdec, kdec, cdec, gain)


def _mla_proj_kernel(cq_ref, ckv_ref, kpe_ref, gq_ref, gkv_ref, wuq_ref, wukv_ref, cos_ref, sin_ref,
                     q_ref, k_ref, v_ref):
    cos = cos_ref[...]
    sin = sin_ref[...]
    scale = QK_HEAD_DIM ** -0.5
    cq = _rms(cq_ref[...].astype(F32), gq_ref[...]).astype(BF16)
    mq = jnp.dot(cq, wuq_ref[...], preferred_element_type=F32)
    ckv = _rms(ckv_ref[...].astype(F32), gkv_ref[...]).astype(BF16)
    mkv = jnp.dot(ckv, wukv_ref[...], preferred_element_type=F32)
    kpe = _rope(kpe_ref[...].astype(F32), cos, sin).astype(BF16)
    for h in range(N_MLA_HEADS):
        nope = slice(h * QK_PAD, h * QK_PAD + LANES)
        pe = slice(h * QK_PAD + LANES, (h + 1) * QK_PAD)
        q_ref[:, nope] = (mq[:, nope] * scale).astype(BF16)
        q_ref[:, pe] = (_rope(mq[:, pe], cos, sin) * scale).astype(BF16)
        k_ref[:, nope] = mkv[:, h * LANES:(h + 1) * LANES].astype(BF16)
        k_ref[:, pe] = kpe
    v_ref[...] = mkv[:, MLA_WIDTH:].astype(BF16)


def _mla_proj(z, gq, gkv, wuq, wukv, cos_m, sin_m, tm):
    s = z.shape[0]
    full = lambda a: pl.BlockSpec(a.shape, lambda i: (0,) * a.ndim)
    tab = pl.BlockSpec((tm, LANES), lambda i: (i, 0))
    qk = N_MLA_HEADS * QK_PAD
    return pl.pallas_call(
        _mla_proj_kernel,
        grid=(s // tm,),
        in_specs=[pl.BlockSpec((tm, Q_LORA_RANK), lambda i: (i, Z_CQ // Q_LORA_RANK)),
                  pl.BlockSpec((tm, KV_LORA_RANK), lambda i: (i, Z_CKV // KV_LORA_RANK)),
                  pl.BlockSpec((tm, LANES), lambda i: (i, Z_KPE // LANES)),
                  full(gq), full(gkv), full(wuq), full(wukv), tab, tab],
        out_specs=[pl.BlockSpec((tm, qk), lambda i: (i, 0)),
                   pl.BlockSpec((tm, qk), lambda i: (i, 0)),
                   pl.BlockSpec((tm, MLA_WIDTH), lambda i: (i, 0))],
        out_shape=[jax.ShapeDtypeStruct((s, qk), BF16),
                   jax.ShapeDtypeStruct((s, qk), BF16),
                   jax.ShapeDtypeStruct((s, MLA_WIDTH), BF16)],
        compiler_params=_cparams("arbitrary"),
        name="mla_proj",
    )(z, z, z, gq, gkv, wuq, wukv, cos_m, sin_m)


def _attn_kernel(q_ref, k_ref, v_ref, gain_ref, o_ref, m_sc, l_sc, acc_sc, *, tq):
    qi = pl.program_id(1)
    q = q_ref[...]
    m_sc[...] = jnp.full_like(m_sc, -jnp.inf)
    l_sc[...] = jnp.zeros_like(l_sc)
    acc_sc[...] = jnp.zeros_like(acc_sc)

    def step(kb, masked):
        start = pl.multiple_of(kb * tq, tq)
        k = k_ref[pl.ds(start, tq), :]
        v = v_ref[pl.ds(start, tq), :]
        s = lax.dot_general(q, k, (((1,), (1,)), ((), ())), preferred_element_type=F32)
        if masked:
            row = lax.broadcasted_iota(jnp.int32, s.shape, 0)
            col = lax.broadcasted_iota(jnp.int32, s.shape, 1)
            s = jnp.where(col <= row, s, -jnp.inf)
        m_prev = m_sc[...]
        m_new = jnp.maximum(m_prev, jnp.max(s, axis=-1, keepdims=True))
        alpha = jnp.exp(m_prev - m_new)
        p = jnp.exp(s - m_new)
        l_sc[...] = alpha * l_sc[...] + jnp.sum(p, axis=-1, keepdims=True)
        acc_sc[...] = alpha * acc_sc[...] + jnp.dot(p.astype(BF16), v, preferred_element_type=F32)
        m_sc[...] = m_new

    def body(kb, carry):
        step(kb, False)
        return carry

    lax.fori_loop(0, qi, body, 0)
    step(qi, True)
    o = acc_sc[...] / l_sc[...]
    o_ref[...] = _rms(o, gain_ref[...]).astype(o_ref.dtype)


def _attention(q, k, v, gain, tq):
    s = q.shape[0]
    return pl.pallas_call(
        functools.partial(_attn_kernel, tq=tq),
        grid=(N_MLA_HEADS, s // tq),
        in_specs=[pl.BlockSpec((tq, QK_PAD), lambda h, i: (i, h)),
                  pl.BlockSpec((s, QK_PAD), lambda h, i: (0, h)),
                  pl.BlockSpec((s, HEAD_DIM), lambda h, i: (0, h)),
                  pl.BlockSpec((1, HEAD_DIM), lambda h, i: (0, h))],
        out_specs=pl.BlockSpec((tq, HEAD_DIM), lambda h, i: (i, h)),
        out_shape=jax.ShapeDtypeStruct((s, MLA_WIDTH), BF16),
        scratch_shapes=[pltpu.VMEM((tq, 1), F32), pltpu.VMEM((tq, 1), F32),
                        pltpu.VMEM((tq, HEAD_DIM), F32)],
        compiler_params=_cparams("arbitrary", "arbitrary"),
        name="mla_attention",
    )(q, k, v, gain)


def _gmlp_kernel(u_ref, v_ref, gv_ref, ws_ref, bs_ref, gm_ref, o_ref, *, n_chunks):
    row = lax.broadcasted_iota(jnp.int32, (CHUNK, CHUNK), 0)
    col = lax.broadcasted_iota(jnp.int32, (CHUNK, CHUNK), 1)
    for g in range(N_GMLP_GROUPS):
        cols = slice(g * HEAD_DIM, (g + 1) * HEAD_DIM)
        w = jnp.where(col <= row, ws_ref[g], 0.0).astype(BF16)
        bias = bs_ref[g]
        for c in range(n_chunks):
            rows = slice(c * CHUNK, (c + 1) * CHUNK)
            u = jax.nn.gelu(u_ref[rows, cols].astype(F32))
            v = jax.nn.gelu(v_ref[rows, cols].astype(F32))
            vn = _rms(v, gv_ref[:, cols]).astype(BF16)
            sg = jnp.dot(w, vn, preferred_element_type=F32) + bias
            o_ref[rows, cols] = _rms(u * sg, gm_ref[:, cols]).astype(o_ref.dtype)


def _gmlp(z, gv, ws, bs, gm, t):
    s = z.shape[0]
    full = lambda a: pl.BlockSpec(a.shape, lambda i: (0,) * a.ndim)
    return pl.pallas_call(
        functools.partial(_gmlp_kernel, n_chunks=t // CHUNK),
        grid=(s // t,),
        in_specs=[pl.BlockSpec((t, GMLP_WIDTH), lambda i: (i, Z_GU // GMLP_WIDTH)),
                  pl.BlockSpec((t, GMLP_WIDTH), lambda i: (i, Z_GV // GMLP_WIDTH)),
                  full(gv), full(ws), full(bs), full(gm)],
        out_specs=pl.BlockSpec((t, GMLP_WIDTH), lambda i: (i, 0)),
        out_shape=jax.ShapeDtypeStruct((s, GMLP_WIDTH), BF16),
        compiler_params=_cparams("arbitrary"),
        name="gmlp",
    )(z, z, gv, ws, bs, gm)


def _out_proj_kernel(x_ref, yr_ref, ym_ref, yg_ref, w_ref, o_ref):
    y = jnp.concatenate([yr_ref[...], ym_ref[...], yg_ref[...]], axis=-1)
    o_ref[...] = x_ref[...] + jnp.dot(y, w_ref[...], preferred_element_type=F32)


def _out_proj(x, yr, ym, yg, w, tm):
    s, d = x.shape
    blk = lambda n: pl.BlockSpec((tm, n), lambda i: (i, 0))
    return pl.pallas_call(
        _out_proj_kernel,
        grid=(s // tm,),
        in_specs=[blk(d), blk(RET_WIDTH), blk(MLA_WIDTH), blk(GMLP_WIDTH),
                  pl.BlockSpec(w.shape, lambda i: (0, 0))],
        out_specs=blk(d),
        out_shape=jax.ShapeDtypeStruct((s, d), F32),
        compiler_params=_cparams("arbitrary"),
        name="out_proj",
    )(x, yr, ym, yg, w)


def _ffn_kernel(x_ref, g_ref, wg_ref, wu_ref, wd_ref, fn_ref, o_ref, h_sc, *, final_norm):
    f = pl.program_id(1)

    @pl.when(f == 0)
    def _():
        x = x_ref[...]
        h_sc[...] = _rms(x, g_ref[...]).astype(BF16)
        o_ref[...] = x

    h = h_sc[...]
    gate = jnp.dot(h, wg_ref[...], preferred_element_type=F32)
    up = jnp.dot(h, wu_ref[...], preferred_element_type=F32)
    act = (gate / (1.0 + jnp.exp(-gate)) * up).astype(BF16)
    o_ref[...] += jnp.dot(act, wd_ref[...], preferred_element_type=F32)

    if final_norm:
        @pl.when(f == pl.num_programs(1) - 1)
        def _():
            o_ref[...] = _rms(o_ref[...], fn_ref[...])


def _ffn(x, gain, wg, wu, wd, fn_gain, tm, tf, final_norm):
    s, d = x.shape
    dff = wg.shape[1]
    row = pl.BlockSpec((1, d), lambda i, f: (0, 0))
    return pl.pallas_call(
        functools.partial(_ffn_kernel, final_norm=final_norm),
        grid=(s // tm, dff // tf),
        in_specs=[pl.BlockSpec((tm, d), lambda i, f: (i, 0)), row,
                  pl.BlockSpec((d, tf), lambda i, f: (0, f)),
                  pl.BlockSpec((d, tf), lambda i, f: (0, f)),
                  pl.BlockSpec((tf, d), lambda i, f: (f, 0)), row],
        out_specs=pl.BlockSpec((tm, d), lambda i, f: (i, 0)),
        out_shape=jax.ShapeDtypeStruct((s, d), F32),
        scratch_shapes=[pltpu.VMEM((tm, d), BF16)],
        compiler_params=_cparams("arbitrary", "arbitrary"),
        name="ffn",
    )(x, gain, wg, wu, wd, fn_gain)


def _spread_rope_cols(w):
    z = jnp.zeros(w.shape[:-1] + (32,), w.dtype)
    return jnp.concatenate([w[..., :32], z, w[..., 32:], z], axis=-1)


def _prep_w_in(w_in):
    l, d, _ = w_in.shape
    main = jnp.concatenate([w_in[..., :Z_CKV + KV_LORA_RANK],
                            w_in[..., Z_CKV + KV_LORA_RANK + QK_ROPE_DIM:]], axis=-1)
    kpe = _spread_rope_cols(w_in[..., Z_CKV + KV_LORA_RANK:Z_CKV + KV_LORA_RANK + QK_ROPE_DIM])
    pad = jnp.zeros((l, d, Z_WIDTH - Z_KPE - LANES), w_in.dtype)
    return jnp.concatenate([main, kpe, pad], axis=-1).astype(BF16)


def _prep_w_uq(w_uq):
    l, r, _ = w_uq.shape
    w = w_uq.reshape(l, r, N_MLA_HEADS, QK_HEAD_DIM)
    w = jnp.concatenate([w[..., :QK_NOPE_DIM], _spread_rope_cols(w[..., QK_NOPE_DIM:])], axis=-1)
    return w.reshape(l, r, N_MLA_HEADS * QK_PAD).astype(BF16)


def _prep_w_ukv(w_ukv):
    l, r, _ = w_ukv.shape
    w = w_ukv.reshape(l, r, N_MLA_HEADS, 2 * HEAD_DIM)
    k_nope = w[..., :HEAD_DIM].reshape(l, r, MLA_WIDTH)
    v = w[..., HEAD_DIM:].reshape(l, r, MLA_WIDTH)
    return jnp.concatenate([k_nope, v], axis=-1).astype(BF16)


def _tiles(s):
    pick = lambda pref: pref if s % pref == 0 else s
    return dict(in_tm=pick(1024), ret_t=pick(512), ret_chunk=128, mla_tm=pick(512), attn_tq=pick(512),
                gmlp_t=pick(512), out_tm=pick(512), ffn_tm=pick(512), ffn_tf=512)


def kernel(x, positions, attn_norm, w_in, mla_q_norm, w_uq, mla_kv_norm, w_ukv, gmlp_v_norm, gmlp_w_s,
           gmlp_b_s, mix_norm, w_out, ffn_norm, w_gate, w_up, w_down, final_norm):
    b, s, d = x.shape
    assert b == 1, "batch is folded away; only BATCH == 1 is supported"
    depth = w_in.shape[0]
    t = _tiles(s)

    w_in_b = _prep_w_in(w_in)
    w_uq_b = _prep_w_uq(w_uq)
    w_ukv_b = _prep_w_ukv(w_ukv)
    w_out_b = w_out.astype(BF16)
    w_gate_b = w_gate.astype(BF16)
    w_up_b = w_up.astype(BF16)
    w_down_b = w_down.astype(BF16)
    row = lambda a: a.reshape(1, -1)

    cos_r, sin_r, cos_m, sin_m = _rope_tables(positions)
    xs = x.reshape(s, d)
    for l in range(depth):
        mix = row(mix_norm[l])
        z = _in_proj(xs, row(attn_norm[l]), w_in_b[l], t["in_tm"])
        y_ret = _retention(z, cos_r, sin_r, mix[:, :RET_WIDTH], t["ret_t"], t["ret_chunk"])
        q, k, v = _mla_proj(z, row(mla_q_norm[l]), row(mla_kv_norm[l]), w_uq_b[l], w_ukv_b[l],
                            cos_m, sin_m, t["mla_tm"])
        y_mla = _attention(q, k, v, mix[:, RET_WIDTH:RET_WIDTH + MLA_WIDTH], t["attn_tq"])
        y_gm = _gmlp(z, row(gmlp_v_norm[l]), gmlp_w_s[l], gmlp_b_s[l][:, :, None],
                     mix[:, RET_WIDTH + MLA_WIDTH:], t["gmlp_t"])
        xs = _out_proj(xs, y_ret, y_mla, y_gm, w_out_b[l], t["out_tm"])
        xs = _ffn(xs, row(ffn_norm[l]), w_gate_b[l], w_up_b[l], w_down_b[l], row(final_norm),
                  t["ffn_tm"], t["ffn_tf"], final_norm=(l == depth - 1))
    return xs.reshape(b, s, d)
```

```python
import functools

import jax
import jax.numpy as jnp
from jax import lax
from jax.experimental import pallas as pl
from jax.experimental.pallas import tpu as pltpu

F32 = jnp.float32
BF16 = jnp.bfloat16

HEAD_DIM = 128
N_RET_HEADS = 6
N_MLA_HEADS = 6
N_GMLP_GROUPS = 4
RET_WIDTH = N_RET_HEADS * HEAD_DIM
MLA_WIDTH = N_MLA_HEADS * HEAD_DIM
GMLP_WIDTH = N_GMLP_GROUPS * HEAD_DIM
Q_LORA_RANK = 512
KV_LORA_RANK = 512
QK_NOPE_DIM = 128
QK_ROPE_DIM = 64
QK_HEAD_DIM = QK_NOPE_DIM + QK_ROPE_DIM
CHUNK = 128
ROPE_BASE = 10000.0
EPS = 1e-6

LANES = 128
QK_PAD = 2 * LANES
ATTN_HEADS_PER_STEP = 2

Z_CQ = 4 * RET_WIDTH
Z_CKV = Z_CQ + Q_LORA_RANK
Z_GU = Z_CKV + KV_LORA_RANK
Z_GV = Z_GU + GMLP_WIDTH
Z_KPE = Z_GV + GMLP_WIDTH
Z_TN = 768
Z_WIDTH = 7 * Z_TN

VMEM_LIMIT = 56 * 1024 * 1024


def _cparams(*sem):
    return pltpu.CompilerParams(dimension_semantics=sem, vmem_limit_bytes=VMEM_LIMIT)


def _layer_spec(arr, l, ngrid):
    zeros = (0,) * (arr.ndim - 1)
    return pl.BlockSpec((None,) + arr.shape[1:], lambda *_: (l,) + zeros)


def _rms(x, gain):
    return x * lax.rsqrt(jnp.mean(x * x, axis=-1, keepdims=True) + EPS) * gain


def _rope(x, cos, sin):
    return x * cos + pltpu.roll(x, 64, 1) * sin


def _rope_table_kernel(pos_ref, inv_r_ref, sgn_r_ref, inv_m_ref, cm_ref, sm_ref,
                       cos_r_ref, sin_r_ref, cos_m_ref, sin_m_ref):
    pos = pos_ref[...].astype(F32)
    ang_r = pos * inv_r_ref[...]
    cos_r_ref[...] = jnp.cos(ang_r)
    sin_r_ref[...] = jnp.sin(ang_r) * sgn_r_ref[...]
    ang_m = pos * inv_m_ref[...]
    cos_m_ref[...] = jnp.cos(ang_m) * cm_ref[...]
    sin_m_ref[...] = jnp.sin(ang_m) * sm_ref[...]


def _rope_tables(positions):
    s = positions.shape[-1]
    ts = min(s, 1024)
    pos = positions.reshape(s, 1)
    inv_r = 1.0 / (ROPE_BASE ** (jnp.arange(0, HEAD_DIM, 2, dtype=F32) / HEAD_DIM))
    inv_m = 1.0 / (ROPE_BASE ** (jnp.arange(0, QK_ROPE_DIM, 2, dtype=F32) / QK_ROPE_DIM))
    z32 = jnp.zeros((32,), F32)
    o32 = jnp.ones((32,), F32)
    inv_r_full = jnp.concatenate([inv_r, inv_r]).reshape(1, LANES)
    sgn_r = jnp.concatenate([-jnp.ones((64,), F32), jnp.ones((64,), F32)]).reshape(1, LANES)
    inv_m_full = jnp.concatenate([inv_m, z32, inv_m, z32]).reshape(1, LANES)
    cmask = jnp.concatenate([o32, z32, o32, z32]).reshape(1, LANES)
    smask = jnp.concatenate([-o32, z32, o32, z32]).reshape(1, LANES)
    row = pl.BlockSpec((1, LANES), lambda i: (0, 0))
    tab = pl.BlockSpec((ts, LANES), lambda i: (i, 0))
    out = jax.ShapeDtypeStruct((s, LANES), F32)
    return pl.pallas_call(
        _rope_table_kernel,
        grid=(s // ts,),
        in_specs=[pl.BlockSpec((ts, 1), lambda i: (i, 0)), row, row, row, row, row],
        out_specs=[tab, tab, tab, tab],
        out_shape=[out, out, out, out],
        compiler_params=_cparams("arbitrary"),
        name="rope_tables",
    )(pos, inv_r_full, sgn_r, inv_m_full, cmask, smask)


def _in_proj_kernel(x_ref, g_ref, w_ref, o_ref, h_sc):
    @pl.when(pl.program_id(1) == 0)
    def _():
        h_sc[...] = _rms(x_ref[...], g_ref[...]).astype(BF16)

    o_ref[...] = jnp.dot(h_sc[...], w_ref[...], preferred_element_type=F32).astype(o_ref.dtype)


def _in_proj(x, gain, w, l, tm):
    s, d = x.shape
    n = w.shape[2]
    return pl.pallas_call(
        _in_proj_kernel,
        grid=(s // tm, n // Z_TN),
        in_specs=[pl.BlockSpec((tm, d), lambda i, j: (i, 0)),
                  _layer_spec(gain, l, 2),
                  pl.BlockSpec((None, d, Z_TN), lambda i, j: (l, 0, j))],
        out_specs=pl.BlockSpec((tm, Z_TN), lambda i, j: (i, j)),
        out_shape=jax.ShapeDtypeStruct((s, n), BF16),
        scratch_shapes=[pltpu.VMEM((tm, d), BF16)],
        compiler_params=_cparams("arbitrary", "arbitrary"),
        name="in_proj",
    )(x, gain, w)


def _retention_kernel(q_ref, k_ref, v_ref, g_ref, cos_ref, sin_ref, inner_ref, qdec_ref, kdec_ref,
                      cdec_ref, gain_ref, o_ref, state_sc, *, chunk, n_chunks):
    @pl.when(pl.program_id(0) == 0)
    def _():
        state_sc[...] = jnp.zeros_like(state_sc)

    for c in range(n_chunks):
        rows = slice(c * chunk, (c + 1) * chunk)
        cos = cos_ref[rows, :]
        sin = sin_ref[rows, :]
        for h in range(N_RET_HEADS):
            cols = slice(h * HEAD_DIM, (h + 1) * HEAD_DIM)
            q = _rope(q_ref[rows, cols].astype(F32), cos, sin)
            k = _rope(k_ref[rows, cols].astype(F32), cos, sin) * (HEAD_DIM ** -0.5)
            v = v_ref[rows, cols]
            state = state_sc[h]
            scores = lax.dot_general(q.astype(BF16), k.astype(BF16), (((1,), (1,)), ((), ())),
                                     preferred_element_type=F32) * inner_ref[h]
            out = jnp.dot(scores.astype(BF16), v, preferred_element_type=F32)
            out += jnp.dot((q * qdec_ref[:, cols]).astype(BF16), state.astype(BF16),
                           preferred_element_type=F32)
            kt = (k * kdec_ref[:, cols]).T.astype(BF16)
            state_sc[h] = cdec_ref[:, cols] * state + jnp.dot(kt, v, preferred_element_type=F32)
            y = _rms(out, gain_ref[:, cols])
            gate = g_ref[rows, cols].astype(F32)
            o_ref[rows, cols] = (y * (gate / (1.0 + jnp.exp(-gate)))).astype(o_ref.dtype)


def _retention_tables(chunk):
    h = N_RET_HEADS
    log_gamma = jnp.log1p(-jnp.exp2(-5.0 - jnp.arange(h, dtype=F32)))
    idx = jnp.arange(chunk, dtype=F32)
    rel = idx[:, None] - idx[None, :]
    inner = jnp.where(rel >= 0, jnp.exp(log_gamma[:, None, None] * jnp.maximum(rel, 0.0)), 0.0)
    qdec = jnp.exp(log_gamma[None, :] * (idx[:, None] + 1.0))
    kdec = jnp.exp(log_gamma[None, :] * (chunk - 1.0 - idx[:, None]))
    cdec = jnp.exp(log_gamma * chunk)
    rep = lambda a: jnp.repeat(a, HEAD_DIM, axis=-1)
    return inner, rep(qdec), rep(kdec), rep(cdec[None, :])


def _retention(z, cos_r, sin_r, tables, mix, l, t, chunk):
    s = z.shape[0]
    zspec = lambda c: pl.BlockSpec((t, RET_WIDTH), lambda i: (i, c))
    tab = pl.BlockSpec((t, LANES), lambda i: (i, 0))
    full = lambda a: pl.BlockSpec(a.shape, lambda i: (0,) * a.ndim)
    return pl.pallas_call(
        functools.partial(_retention_kernel, chunk=chunk, n_chunks=t // chunk),
        grid=(s // t,),
        in_specs=[zspec(0), zspec(1), zspec(2), zspec(3), tab, tab] + [full(a) for a in tables]
        + [pl.BlockSpec((None, 1, RET_WIDTH), lambda i: (l, 0, 0))],
        out_specs=pl.BlockSpec((t, RET_WIDTH), lambda i: (i, 0)),
        out_shape=jax.ShapeDtypeStruct((s, RET_WIDTH), BF16),
        scratch_shapes=[pltpu.VMEM((N_RET_HEADS, HEAD_DIM, HEAD_DIM), F32)],
        compiler_params=_cparams("arbitrary"),
        name="retention",
    )(z, z, z, z, cos_r, sin_r, *tables, mix)


def _mla_proj_kernel(cq_ref, ckv_ref, kpe_ref, gq_ref, gkv_ref, wuq_ref, wukv_ref, cos_ref, sin_ref,
                     q_ref, k_ref, vt_ref):
    cos = cos_ref[...]
    sin = sin_ref[...]
    scale = QK_HEAD_DIM ** -0.5
    cq = _rms(cq_ref[...].astype(F32), gq_ref[...]).astype(BF16)
    mq = jnp.dot(cq, wuq_ref[...], preferred_element_type=F32)
    ckv = _rms(ckv_ref[...].astype(F32), gkv_ref[...]).astype(BF16)
    mkv = jnp.dot(ckv, wukv_ref[...], preferred_element_type=F32)
    kpe = _rope(kpe_ref[...].astype(F32), cos, sin).astype(BF16)
    for h in range(N_MLA_HEADS):
        nope = slice(h * QK_PAD, h * QK_PAD + LANES)
        pe = slice(h * QK_PAD + LANES, (h + 1) * QK_PAD)
        q_ref[:, nope] = (mq[:, nope] * scale).astype(BF16)
        q_ref[:, pe] = (_rope(mq[:, pe], cos, sin) * scale).astype(BF16)
        k_ref[:, nope] = mkv[:, h * LANES:(h + 1) * LANES].astype(BF16)
        k_ref[:, pe] = kpe
    vt_ref[...] = mkv[:, MLA_WIDTH:].T.astype(BF16)


def _mla_proj(z, gq, gkv, wuq, wukv, cos_m, sin_m, l, tm):
    s = z.shape[0]
    tab = pl.BlockSpec((tm, LANES), lambda i: (i, 0))
    qk = N_MLA_HEADS * QK_PAD
    return pl.pallas_call(
        _mla_proj_kernel,
        grid=(s // tm,),
        in_specs=[pl.BlockSpec((tm, Q_LORA_RANK), lambda i: (i, Z_CQ // Q_LORA_RANK)),
                  pl.BlockSpec((tm, KV_LORA_RANK), lambda i: (i, Z_CKV // KV_LORA_RANK)),
                  pl.BlockSpec((tm, LANES), lambda i: (i, Z_KPE // LANES)),
                  _layer_spec(gq, l, 1), _layer_spec(gkv, l, 1),
                  _layer_spec(wuq, l, 1), _layer_spec(wukv, l, 1), tab, tab],
        out_specs=[pl.BlockSpec((tm, qk), lambda i: (i, 0)),
                   pl.BlockSpec((tm, qk), lambda i: (i, 0)),
                   pl.BlockSpec((None, MLA_WIDTH, tm), lambda i: (i, 0, 0))],
        out_shape=[jax.ShapeDtypeStruct((s, qk), BF16),
                   jax.ShapeDtypeStruct((s, qk), BF16),
                   jax.ShapeDtypeStruct((s // tm, MLA_WIDTH, tm), BF16)],
        compiler_params=_cparams("arbitrary"),
        name="mla_proj",
    )(z, z, z, gq, gkv, wuq, wukv, cos_m, sin_m)


def _attn_kernel(q_ref, k_ref, vt_ref, gain_ref, o_ref, m_sc, l_sc, acc_sc, *, tq, hp):
    qi = pl.program_id(1)
    m_sc[...] = jnp.full_like(m_sc, -jnp.inf)
    l_sc[...] = jnp.zeros_like(l_sc)
    acc_sc[...] = jnp.zeros_like(acc_sc)

    def step(kb, masked):
        start = pl.multiple_of(kb * tq, tq)
        for j in range(hp):
            qk_cols = slice(j * QK_PAD, (j + 1) * QK_PAD)
            q = q_ref[:, qk_cols]
            k = k_ref[pl.ds(start, tq), qk_cols]
            vt = vt_ref[kb, j * HEAD_DIM:(j + 1) * HEAD_DIM, :]
            st = lax.dot_general(k, q, (((1,), (1,)), ((), ())), preferred_element_type=F32)
            if masked:
                kv_pos = lax.broadcasted_iota(jnp.int32, st.shape, 0)
                q_pos = lax.broadcasted_iota(jnp.int32, st.shape, 1)
                st = jnp.where(kv_pos <= q_pos, st, -jnp.inf)
            m_prev = m_sc[j]
            m_new = jnp.maximum(m_prev, jnp.max(st, axis=0, keepdims=True))
            alpha = jnp.exp(m_prev - m_new)
            p = jnp.exp(st - m_new)
            l_sc[j] = alpha * l_sc[j] + jnp.sum(p, axis=0, keepdims=True)
            acc_sc[j] = alpha * acc_sc[j] + jnp.dot(vt, p.astype(BF16), preferred_element_type=F32)
            m_sc[j] = m_new

    def body(kb, carry):
        step(kb, False)
        return carry

    lax.fori_loop(0, qi, body, 0)
    step(qi, True)
    for j in range(hp):
        cols = slice(j * HEAD_DIM, (j + 1) * HEAD_DIM)
        o = (acc_sc[j] * (1.0 / l_sc[j])).T
        o_ref[:, cols] = _rms(o, gain_ref[:, cols]).astype(o_ref.dtype)


def _attention(q, k, vt, mix, l, tq):
    s = q.shape[0]
    hp = ATTN_HEADS_PER_STEP
    width = hp * HEAD_DIM
    return pl.pallas_call(
        functools.partial(_attn_kernel, tq=tq, hp=hp),
        grid=(N_MLA_HEADS // hp, s // tq),
        in_specs=[pl.BlockSpec((tq, hp * QK_PAD), lambda h, i: (i, h)),
                  pl.BlockSpec((s, hp * QK_PAD), lambda h, i: (0, h)),
                  pl.BlockSpec((s // tq, width, tq), lambda h, i: (0, h, 0)),
                  pl.BlockSpec((None, 1, width), lambda h, i: (l, 0, RET_WIDTH // width + h))],
        out_specs=pl.BlockSpec((tq, width), lambda h, i: (i, h)),
        out_shape=jax.ShapeDtypeStruct((s, MLA_WIDTH), BF16),
        scratch_shapes=[pltpu.VMEM((hp, 1, tq), F32), pltpu.VMEM((hp, 1, tq), F32),
                        pltpu.VMEM((hp, HEAD_DIM, tq), F32)],
        compiler_params=_cparams("arbitrary", "arbitrary"),
        name="mla_attention",
    )(q, k, vt, mix)


def _gmlp_kernel(u_ref, v_ref, gv_ref, ws_ref, bs_ref, gm_ref, o_ref, *, n_chunks):
    row = lax.broadcasted_iota(jnp.int32, (CHUNK, CHUNK), 0)
    col = lax.broadcasted_iota(jnp.int32, (CHUNK, CHUNK), 1)
    for g in range(N_GMLP_GROUPS):
        cols = slice(g * HEAD_DIM, (g + 1) * HEAD_DIM)
        w = jnp.where(col <= row, ws_ref[g], 0.0).astype(BF16)
        bias = bs_ref[g]
        for c in range(n_chunks):
            rows = slice(c * CHUNK, (c + 1) * CHUNK)
            u = jax.nn.gelu(u_ref[rows, cols].astype(F32))
            v = jax.nn.gelu(v_ref[rows, cols].astype(F32))
            vn = _rms(v, gv_ref[:, cols]).astype(BF16)
            sg = jnp.dot(w, vn, preferred_element_type=F32) + bias
            o_ref[rows, cols] = _rms(u * sg, gm_ref[:, cols]).astype(o_ref.dtype)


def _gmlp(z, gv, ws, bs, mix, l, t):
    s = z.shape[0]
    return pl.pallas_call(
        functools.partial(_gmlp_kernel, n_chunks=t // CHUNK),
        grid=(s // t,),
        in_specs=[pl.BlockSpec((t, GMLP_WIDTH), lambda i: (i, Z_GU // GMLP_WIDTH)),
                  pl.BlockSpec((t, GMLP_WIDTH), lambda i: (i, Z_GV // GMLP_WIDTH)),
                  _layer_spec(gv, l, 1), _layer_spec(ws, l, 1), _layer_spec(bs, l, 1),
                  pl.BlockSpec((None, 1, GMLP_WIDTH),
                               lambda i: (l, 0, (RET_WIDTH + MLA_WIDTH) // GMLP_WIDTH))],
        out_specs=pl.BlockSpec((t, GMLP_WIDTH), lambda i: (i, 0)),
        out_shape=jax.ShapeDtypeStruct((s, GMLP_WIDTH), BF16),
        compiler_params=_cparams("arbitrary"),
        name="gmlp",
    )(z, z, gv, ws, bs, mix)


def _out_proj_kernel(x_ref, yr_ref, ym_ref, yg_ref, w_ref, o_ref):
    y = jnp.concatenate([yr_ref[...], ym_ref[...], yg_ref[...]], axis=-1)
    o_ref[...] = x_ref[...] + jnp.dot(y, w_ref[...], preferred_element_type=F32)


def _out_proj(x, yr, ym, yg, w, l, tm):
    s, d = x.shape
    blk = lambda n: pl.BlockSpec((tm, n), lambda i: (i, 0))
    return pl.pallas_call(
        _out_proj_kernel,
        grid=(s // tm,),
        in_specs=[blk(d), blk(RET_WIDTH), blk(MLA_WIDTH), blk(GMLP_WIDTH), _layer_spec(w, l, 1)],
        out_specs=blk(d),
        out_shape=jax.ShapeDtypeStruct((s, d), F32),
        compiler_params=_cparams("arbitrary"),
        name="out_proj",
    )(x, yr, ym, yg, w)


def _ffn_kernel(x_ref, g_ref, wg_ref, wu_ref, wd_ref, fn_ref, o_ref, h_sc, *, final_norm):
    f = pl.program_id(1)

    @pl.when(f == 0)
    def _():
        x = x_ref[...]
        h_sc[...] = _rms(x, g_ref[...]).astype(BF16)
        o_ref[...] = x

    h = h_sc[...]
    gate = jnp.dot(h, wg_ref[...], preferred_element_type=F32)
    up = jnp.dot(h, wu_ref[...], preferred_element_type=F32)
    act = (gate / (1.0 + jnp.exp(-gate)) * up).astype(BF16)
    o_ref[...] += jnp.dot(act, wd_ref[...], preferred_element_type=F32)

    if final_norm:
        @pl.when(f == pl.num_programs(1) - 1)
        def _():
            o_ref[...] = _rms(o_ref[...], fn_ref[...])


def _ffn(x, gain, wg, wu, wd, fn_gain, l, tm, tf, final_norm):
    s, d = x.shape
    dff = wg.shape[2]
    return pl.pallas_call(
        functools.partial(_ffn_kernel, final_norm=final_norm),
        grid=(s // tm, dff // tf),
        in_specs=[pl.BlockSpec((tm, d), lambda i, f: (i, 0)),
                  _layer_spec(gain, l, 2),
                  pl.BlockSpec((None, d, tf), lambda i, f: (l, 0, f)),
                  pl.BlockSpec((None, d, tf), lambda i, f: (l, 0, f)),
                  pl.BlockSpec((None, tf, d), lambda i, f: (l, f, 0)),
                  pl.BlockSpec((1, d), lambda i, f: (0, 0))],
        out_specs=pl.BlockSpec((tm, d), lambda i, f: (i, 0)),
        out_shape=jax.ShapeDtypeStruct((s, d), F32),
        scratch_shapes=[pltpu.VMEM((tm, d), BF16)],
        compiler_params=_cparams("arbitrary", "arbitrary"),
        name="ffn",
    )(x, gain, wg, wu, wd, fn_gain)


def _spread_rope_cols(w):
    z = jnp.zeros(w.shape[:-1] + (32,), w.dtype)
    return jnp.concatenate([w[..., :32], z, w[..., 32:], z], axis=-1)


def _prep_w_in(w_in):
    l, d, _ = w_in.shape
    kpe_start = Z_CKV + KV_LORA_RANK
    main = jnp.concatenate([w_in[..., :kpe_start],
                            w_in[..., kpe_start + QK_ROPE_DIM:]], axis=-1)
    kpe = _spread_rope_cols(w_in[..., kpe_start:kpe_start + QK_ROPE_DIM])
    pad = jnp.zeros((l, d, Z_WIDTH - Z_KPE - LANES), w_in.dtype)
    return jnp.concatenate([main, kpe, pad], axis=-1).astype(BF16)


def _prep_w_uq(w_uq):
    l, r, _ = w_uq.shape
    w = w_uq.reshape(l, r, N_MLA_HEADS, QK_HEAD_DIM)
    w = jnp.concatenate([w[..., :QK_NOPE_DIM], _spread_rope_cols(w[..., QK_NOPE_DIM:])], axis=-1)
    return w.reshape(l, r, N_MLA_HEADS * QK_PAD).astype(BF16)


def _prep_w_ukv(w_ukv):
    l, r, _ = w_ukv.shape
    w = w_ukv.reshape(l, r, N_MLA_HEADS, 2 * HEAD_DIM)
    k_nope = w[..., :HEAD_DIM].reshape(l, r, MLA_WIDTH)
    v = w[..., HEAD_DIM:].reshape(l, r, MLA_WIDTH)
    return jnp.concatenate([k_nope, v], axis=-1).astype(BF16)


def _tiles(s):
    pick = lambda pref: pref if s % pref == 0 else s
    attn = pick(512)
    return dict(in_tm=pick(1024), ret_t=pick(512), ret_chunk=128, mla_tm=attn, attn_tq=attn,
                gmlp_t=pick(512), out_tm=pick(512), ffn_tm=pick(512), ffn_tf=512)


def kernel(x, positions, attn_norm, w_in, mla_q_norm, w_uq, mla_kv_norm, w_ukv, gmlp_v_norm, gmlp_w_s,
           gmlp_b_s, mix_norm, w_out, ffn_norm, w_gate, w_up, w_down, final_norm):
    b, s, d = x.shape
    assert b == 1, "batch is folded away; only BATCH == 1 is supported"
    depth = w_in.shape[0]
    t = _tiles(s)

    w_in_b = _prep_w_in(w_in)
    w_uq_b = _prep_w_uq(w_uq)
    w_ukv_b = _prep_w_ukv(w_ukv)
    w_out_b = w_out.astype(BF16)
    w_gate_b = w_gate.astype(BF16)
    w_up_b = w_up.astype(BF16)
    w_down_b = w_down.astype(BF16)
    rows = lambda a: a.reshape(depth, 1, -1)
    attn_g, q_g, kv_g, gv_g, mix_g, ffn_g = map(
        rows, (attn_norm, mla_q_norm, mla_kv_norm, gmlp_v_norm, mix_norm, ffn_norm))
    b_s = gmlp_b_s[..., None]
    fn_g = final_norm.reshape(1, d)

    cos_r, sin_r, cos_m, sin_m = _rope_tables(positions)
    ret_tables = _retention_tables(t["ret_chunk"])
    xs = x.reshape(s, d)
    for l in range(depth):
        z = _in_proj(xs, attn_g, w_in_b, l, t["in_tm"])
        y_ret = _retention(z, cos_r, sin_r, ret_tables, mix_g, l, t["ret_t"], t["ret_chunk"])
        q, k, vt = _mla_proj(z, q_g, kv_g, w_uq_b, w_ukv_b, cos_m, sin_m, l, t["mla_tm"])
        y_mla = _attention(q, k, vt, mix_g, l, t["attn_tq"])
        y_gm = _gmlp(z, gv_g, gmlp_w_s, b_s, mix_g, l, t["gmlp_t"])
        xs = _out_proj(xs, y_ret, y_mla, y_gm, w_out_b, l, t["out_tm"])
        xs = _ffn(xs, ffn_g, w_gate_b, w_up_b, w_down_b, fn_g, l, t["ffn_tm"], t["ffn_tf"],
                  final_norm=(l == depth - 1))
    return xs.reshape(b, s, d)
```

```python
import functools

import jax
import jax.numpy as jnp
from jax import lax
from jax.experimental import pallas as pl
from jax.experimental.pallas import tpu as pltpu

F32 = jnp.float32
BF16 = jnp.bfloat16

HEAD_DIM = 128
N_RET_HEADS = 6
N_MLA_HEADS = 6
N_GMLP_GROUPS = 4
RET_WIDTH = N_RET_HEADS * HEAD_DIM
MLA_WIDTH = N_MLA_HEADS * HEAD_DIM
GMLP_WIDTH = N_GMLP_GROUPS * HEAD_DIM
Q_LORA_RANK = 512
KV_LORA_RANK = 512
QK_NOPE_DIM = 128
QK_ROPE_DIM = 64
QK_HEAD_DIM = QK_NOPE_DIM + QK_ROPE_DIM
CHUNK = 128
ROPE_BASE = 10000.0
EPS = 1e-6

LANES = 128
QK_PAD = 2 * LANES
ATTN_HEADS_PER_STEP = 2
VT_ROWS = HEAD_DIM + 16
LOG2_E = 1.4426950408889634

Z_CQ = 4 * RET_WIDTH
Z_CKV = Z_CQ + Q_LORA_RANK
Z_MAIN = Z_CKV + KV_LORA_RANK
Z_GU = Z_MAIN
Z_GV = Z_GU + GMLP_WIDTH
Z_KPE = Z_GV + GMLP_WIDTH
Z_TN = 512
Z_TAIL = 3 * Z_TN
Z_WIDTH = Z_MAIN + Z_TAIL

VMEM_LIMIT = 56 * 1024 * 1024


def _cparams(*sem):
    return pltpu.CompilerParams(dimension_semantics=sem, vmem_limit_bytes=VMEM_LIMIT)


def _layer_spec(arr, l, ngrid):
    zeros = (0,) * (arr.ndim - 1)
    return pl.BlockSpec((None,) + arr.shape[1:], lambda *_: (l,) + zeros)


def _rms(x, gain):
    return x * lax.rsqrt(jnp.mean(x * x, axis=-1, keepdims=True) + EPS) * gain


def _rope(x, cos, sin):
    return x * cos + pltpu.roll(x, 64, 1) * sin


def _rope_table_kernel(pos_ref, inv_r_ref, sgn_r_ref, inv_m_ref, cm_ref, sm_ref,
                       cos_r_ref, sin_r_ref, cos_m_ref, sin_m_ref):
    pos = pos_ref[...].astype(F32)
    ang_r = pos * inv_r_ref[...]
    cos_r_ref[...] = jnp.cos(ang_r)
    sin_r_ref[...] = jnp.sin(ang_r) * sgn_r_ref[...]
    ang_m = pos * inv_m_ref[...]
    cos_m_ref[...] = jnp.cos(ang_m) * cm_ref[...]
    sin_m_ref[...] = jnp.sin(ang_m) * sm_ref[...]


def _rope_tables(positions):
    s = positions.shape[-1]
    ts = min(s, 1024)
    pos = positions.reshape(s, 1)
    inv_r = 1.0 / (ROPE_BASE ** (jnp.arange(0, HEAD_DIM, 2, dtype=F32) / HEAD_DIM))
    inv_m = 1.0 / (ROPE_BASE ** (jnp.arange(0, QK_ROPE_DIM, 2, dtype=F32) / QK_ROPE_DIM))
    z32 = jnp.zeros((32,), F32)
    o32 = jnp.ones((32,), F32)
    inv_r_full = jnp.concatenate([inv_r, inv_r]).reshape(1, LANES)
    sgn_r = jnp.concatenate([-jnp.ones((64,), F32), jnp.ones((64,), F32)]).reshape(1, LANES)
    inv_m_full = jnp.concatenate([inv_m, z32, inv_m, z32]).reshape(1, LANES)
    cmask = jnp.concatenate([o32, z32, o32, z32]).reshape(1, LANES)
    smask = jnp.concatenate([-o32, z32, o32, z32]).reshape(1, LANES)
    row = pl.BlockSpec((1, LANES), lambda i: (0, 0))
    tab = pl.BlockSpec((ts, LANES), lambda i: (i, 0))
    out = jax.ShapeDtypeStruct((s, LANES), F32)
    return pl.pallas_call(
        _rope_table_kernel,
        grid=(s // ts,),
        in_specs=[pl.BlockSpec((ts, 1), lambda i: (i, 0)), row, row, row, row, row],
        out_specs=[tab, tab, tab, tab],
        out_shape=[out, out, out, out],
        compiler_params=_cparams("arbitrary"),
        name="rope_tables",
    )(pos, inv_r_full, sgn_r, inv_m_full, cmask, smask)


def _in_proj_kernel(x_ref, g_ref, wm_ref, wt_ref, o_ref, h_sc, *, n_main):
    j = pl.program_id(1)

    @pl.when(j == 0)
    def _():
        h_sc[...] = _rms(x_ref[...], g_ref[...]).astype(BF16)

    @pl.when(j < n_main)
    def _():
        w = wm_ref[...].astype(BF16)
        o_ref[...] = jnp.dot(h_sc[...], w, preferred_element_type=F32).astype(o_ref.dtype)

    @pl.when(j >= n_main)
    def _():
        o_ref[...] = jnp.dot(h_sc[...], wt_ref[...], preferred_element_type=F32).astype(o_ref.dtype)


def _in_proj(x, gain, w_in, w_tail, l, tm):
    s, d = x.shape
    n_main = Z_MAIN // Z_TN
    n_tail = Z_TAIL // Z_TN
    return pl.pallas_call(
        functools.partial(_in_proj_kernel, n_main=n_main),
        grid=(s // tm, n_main + n_tail),
        in_specs=[pl.BlockSpec((tm, d), lambda i, j: (i, 0)),
                  _layer_spec(gain, l, 2),
                  pl.BlockSpec((None, d, Z_TN), lambda i, j: (l, 0, jnp.minimum(j, n_main - 1))),
                  pl.BlockSpec((None, d, Z_TN), lambda i, j: (l, 0, jnp.maximum(j - n_main, 0)))],
        out_specs=pl.BlockSpec((tm, Z_TN), lambda i, j: (i, j)),
        out_shape=jax.ShapeDtypeStruct((s, Z_WIDTH), BF16),
        scratch_shapes=[pltpu.VMEM((tm, d), BF16)],
        compiler_params=_cparams("arbitrary", "arbitrary"),
        name="in_proj",
    )(x, gain, w_in, w_tail)


def _retention_kernel(q_ref, k_ref, v_ref, g_ref, cos_ref, sin_ref, inner_ref, qdec_ref, kdec_ref,
                      cdec_ref, gain_ref, o_ref, state_sc, *, chunk, n_chunks):
    @pl.when(pl.program_id(0) == 0)
    def _():
        state_sc[...] = jnp.zeros_like(state_sc)

    for c in range(n_chunks):
        rows = slice(c * chunk, (c + 1) * chunk)
        cos = cos_ref[rows, :]
        sin = sin_ref[rows, :]
        for h in range(N_RET_HEADS):
            cols = slice(h * HEAD_DIM, (h + 1) * HEAD_DIM)
            q = _rope(q_ref[rows, cols].astype(F32), cos, sin)
            k = _rope(k_ref[rows, cols].astype(F32), cos, sin) * (HEAD_DIM ** -0.5)
            v = v_ref[rows, cols]
            state = state_sc[h]
            scores = lax.dot_general(q.astype(BF16), k.astype(BF16), (((1,), (1,)), ((), ())),
                                     preferred_element_type=F32) * inner_ref[h]
            out = jnp.dot(scores.astype(BF16), v, preferred_element_type=F32)
            out += jnp.dot((q * qdec_ref[:, cols]).astype(BF16), state.astype(BF16),
                           preferred_element_type=F32)
            kt = (k * kdec_ref[:, cols]).T.astype(BF16)
            state_sc[h] = cdec_ref[:, cols] * state + jnp.dot(kt, v, preferred_element_type=F32)
            y = _rms(out, gain_ref[:, cols])
            gate = g_ref[rows, cols].astype(F32)
            o_ref[rows, cols] = (y * (gate / (1.0 + jnp.exp(-gate)))).astype(o_ref.dtype)


def _retention_tables(chunk):
    h = N_RET_HEADS
    log_gamma = jnp.log1p(-jnp.exp2(-5.0 - jnp.arange(h, dtype=F32)))
    idx = jnp.arange(chunk, dtype=F32)
    rel = idx[:, None] - idx[None, :]
    inner = jnp.where(rel >= 0, jnp.exp(log_gamma[:, None, None] * jnp.maximum(rel, 0.0)), 0.0)
    qdec = jnp.exp(log_gamma[None, :] * (idx[:, None] + 1.0))
    kdec = jnp.exp(log_gamma[None, :] * (chunk - 1.0 - idx[:, None]))
    cdec = jnp.exp(log_gamma * chunk)
    rep = lambda a: jnp.repeat(a, HEAD_DIM, axis=-1)
    return inner, rep(qdec), rep(kdec), rep(cdec[None, :])


def _retention(z, cos_r, sin_r, tables, mix, l, t, chunk):
    s = z.shape[0]
    zspec = lambda c: pl.BlockSpec((t, RET_WIDTH), lambda i: (i, c))
    tab = pl.BlockSpec((t, LANES), lambda i: (i, 0))
    full = lambda a: pl.BlockSpec(a.shape, lambda i: (0,) * a.ndim)
    return pl.pallas_call(
        functools.partial(_retention_kernel, chunk=chunk, n_chunks=t // chunk),
        grid=(s // t,),
        in_specs=[zspec(0), zspec(1), zspec(2), zspec(3), tab, tab] + [full(a) for a in tables]
        + [pl.BlockSpec((None, 1, RET_WIDTH), lambda i: (l, 0, 0))],
        out_specs=pl.BlockSpec((t, RET_WIDTH), lambda i: (i, 0)),
        out_shape=jax.ShapeDtypeStruct((s, RET_WIDTH), BF16),
        scratch_shapes=[pltpu.VMEM((N_RET_HEADS, HEAD_DIM, HEAD_DIM), F32)],
        compiler_params=_cparams("arbitrary"),
        name="retention",
    )(z, z, z, z, cos_r, sin_r, *tables, mix)


def _mla_proj_kernel(cq_ref, ckv_ref, kpe_ref, gq_ref, gkv_ref, wuq_ref, wukv_ref, cos_ref, sin_ref,
                     q_ref, k_ref, vt_ref):
    cos = cos_ref[...]
    sin = sin_ref[...]
    scale = QK_HEAD_DIM ** -0.5 * LOG2_E
    cq = _rms(cq_ref[...].astype(F32), gq_ref[...]).astype(BF16)
    mq = jnp.dot(cq, wuq_ref[...], preferred_element_type=F32)
    ckv = _rms(ckv_ref[...].astype(F32), gkv_ref[...]).astype(BF16)
    mkv = jnp.dot(ckv, wukv_ref[...], preferred_element_type=F32)
    kpe = _rope(kpe_ref[...].astype(F32), cos, sin).astype(BF16)
    for h in range(N_MLA_HEADS):
        nope = slice(h * QK_PAD, h * QK_PAD + LANES)
        pe = slice(h * QK_PAD + LANES, (h + 1) * QK_PAD)
        q_ref[:, nope] = (mq[:, nope] * scale).astype(BF16)
        q_ref[:, pe] = (_rope(mq[:, pe], cos, sin) * scale).astype(BF16)
        k_ref[:, nope] = mkv[:, h * LANES:(h + 1) * LANES].astype(BF16)
        k_ref[:, pe] = kpe
    vt = mkv[:, MLA_WIDTH:].T.astype(BF16)
    tm = vt.shape[1]
    extra = lax.broadcasted_iota(jnp.int32, (VT_ROWS - HEAD_DIM, tm), 0)
    ones_row = jnp.where(extra == 0, 1.0, 0.0).astype(BF16)
    for h in range(N_MLA_HEADS):
        vt_ref[h * VT_ROWS:h * VT_ROWS + HEAD_DIM, :] = vt[h * HEAD_DIM:(h + 1) * HEAD_DIM, :]
        vt_ref[h * VT_ROWS + HEAD_DIM:(h + 1) * VT_ROWS, :] = ones_row


def _mla_proj(z, gq, gkv, wuq, wukv, cos_m, sin_m, l, tm):
    s = z.shape[0]
    tab = pl.BlockSpec((tm, LANES), lambda i: (i, 0))
    qk = N_MLA_HEADS * QK_PAD
    return pl.pallas_call(
        _mla_proj_kernel,
        grid=(s // tm,),
        in_specs=[pl.BlockSpec((tm, Q_LORA_RANK), lambda i: (i, Z_CQ // Q_LORA_RANK)),
                  pl.BlockSpec((tm, KV_LORA_RANK), lambda i: (i, Z_CKV // KV_LORA_RANK)),
                  pl.BlockSpec((tm, LANES), lambda i: (i, Z_KPE // LANES)),
                  _layer_spec(gq, l, 1), _layer_spec(gkv, l, 1),
                  _layer_spec(wuq, l, 1), _layer_spec(wukv, l, 1), tab, tab],
        out_specs=[pl.BlockSpec((tm, qk), lambda i: (i, 0)),
                   pl.BlockSpec((tm, qk), lambda i: (i, 0)),
                   pl.BlockSpec((None, N_MLA_HEADS * VT_ROWS, tm), lambda i: (i, 0, 0))],
        out_shape=[jax.ShapeDtypeStruct((s, qk), BF16),
                   jax.ShapeDtypeStruct((s, qk), BF16),
                   jax.ShapeDtypeStruct((s // tm, N_MLA_HEADS * VT_ROWS, tm), BF16)],
        compiler_params=_cparams("arbitrary"),
        name="mla_proj",
    )(z, z, z, gq, gkv, wuq, wukv, cos_m, sin_m)


def _attn_kernel(q_ref, k_ref, vt_ref, gain_ref, o_ref, m_sc, acc_sc, st0_sc, st1_sc, *, tq, hp):
    qi = pl.program_id(1)
    m_sc[...] = jnp.full_like(m_sc, -jnp.inf)
    acc_sc[...] = jnp.zeros_like(acc_sc)

    def scores(kb, st_sc):
        start = pl.multiple_of(kb * tq, tq)
        for j in range(hp):
            qk_cols = slice(j * QK_PAD, (j + 1) * QK_PAD)
            st_sc[j] = lax.dot_general(k_ref[pl.ds(start, tq), qk_cols], q_ref[:, qk_cols],
                                       (((1,), (1,)), ((), ())), preferred_element_type=F32)

    def softmax_pv(kb, st_sc, masked):
        for j in range(hp):
            vt = vt_ref[kb, j * VT_ROWS:(j + 1) * VT_ROWS, :]
            st = st_sc[j]
            if masked:
                kv_pos = lax.broadcasted_iota(jnp.int32, st.shape, 0)
                q_pos = lax.broadcasted_iota(jnp.int32, st.shape, 1)
                st = jnp.where(kv_pos <= q_pos, st, -jnp.inf)
            m_prev = m_sc[j]
            m_new = jnp.maximum(m_prev, jnp.max(st, axis=0, keepdims=True))
            alpha = jnp.exp2(m_prev - m_new)
            p = jnp.exp2(st - m_new).astype(BF16)
            acc_sc[j] = alpha * acc_sc[j] + jnp.dot(vt, p, preferred_element_type=F32)
            m_sc[j] = m_new

    scores(0, st0_sc)

    def pair(i, carry):
        kb = 2 * i
        scores(kb + 1, st1_sc)
        softmax_pv(kb, st0_sc, False)
        scores(kb + 2, st0_sc)
        softmax_pv(kb + 1, st1_sc, False)
        return carry

    lax.fori_loop(0, qi // 2, pair, 0)

    @pl.when(qi % 2 == 0)
    def _():
        softmax_pv(qi, st0_sc, True)

    @pl.when(qi % 2 == 1)
    def _():
        scores(qi, st1_sc)
        softmax_pv(qi - 1, st0_sc, False)
        softmax_pv(qi, st1_sc, True)

    for j in range(hp):
        cols = slice(j * HEAD_DIM, (j + 1) * HEAD_DIM)
        denom = acc_sc[j, HEAD_DIM:HEAD_DIM + 1, :]
        o = (acc_sc[j, :HEAD_DIM, :] * (1.0 / denom)).T
        o_ref[:, cols] = _rms(o, gain_ref[:, cols]).astype(o_ref.dtype)


def _attention(q, k, vt, mix, l, tq):
    s = q.shape[0]
    hp = ATTN_HEADS_PER_STEP
    width = hp * HEAD_DIM
    return pl.pallas_call(
        functools.partial(_attn_kernel, tq=tq, hp=hp),
        grid=(N_MLA_HEADS // hp, s // tq),
        in_specs=[pl.BlockSpec((tq, hp * QK_PAD), lambda h, i: (i, h)),
                  pl.BlockSpec((s, hp * QK_PAD), lambda h, i: (0, h)),
                  pl.BlockSpec((s // tq, hp * VT_ROWS, tq), lambda h, i: (0, h, 0)),
                  pl.BlockSpec((None, 1, width), lambda h, i: (l, 0, RET_WIDTH // width + h))],
        out_specs=pl.BlockSpec((tq, width), lambda h, i: (i, h)),
        out_shape=jax.ShapeDtypeStruct((s, MLA_WIDTH), BF16),
        scratch_shapes=[pltpu.VMEM((hp, 1, tq), F32),
                        pltpu.VMEM((hp, VT_ROWS, tq), F32),
                        pltpu.VMEM((hp, tq, tq), F32), pltpu.VMEM((hp, tq, tq), F32)],
        compiler_params=_cparams("arbitrary", "arbitrary"),
        name="mla_attention",
    )(q, k, vt, mix)


def _gmlp_kernel(u_ref, v_ref, gv_ref, ws_ref, bs_ref, gm_ref, o_ref, *, n_chunks):
    row = lax.broadcasted_iota(jnp.int32, (CHUNK, CHUNK), 0)
    col = lax.broadcasted_iota(jnp.int32, (CHUNK, CHUNK), 1)
    for g in range(N_GMLP_GROUPS):
        cols = slice(g * HEAD_DIM, (g + 1) * HEAD_DIM)
        w = jnp.where(col <= row, ws_ref[g], 0.0).astype(BF16)
        bias = bs_ref[g]
        for c in range(n_chunks):
            rows = slice(c * CHUNK, (c + 1) * CHUNK)
            u = jax.nn.gelu(u_ref[rows, cols].astype(F32))
            v = jax.nn.gelu(v_ref[rows, cols].astype(F32))
            vn = _rms(v, gv_ref[:, cols]).astype(BF16)
            sg = jnp.dot(w, vn, preferred_element_type=F32) + bias
            o_ref[rows, cols] = _rms(u * sg, gm_ref[:, cols]).astype(o_ref.dtype)


def _gmlp(z, gv, ws, bs, mix, l, t):
    s = z.shape[0]
    return pl.pallas_call(
        functools.partial(_gmlp_kernel, n_chunks=t // CHUNK),
        grid=(s // t,),
        in_specs=[pl.BlockSpec((t, GMLP_WIDTH), lambda i: (i, Z_GU // GMLP_WIDTH)),
                  pl.BlockSpec((t, GMLP_WIDTH), lambda i: (i, Z_GV // GMLP_WIDTH)),
                  _layer_spec(gv, l, 1), _layer_spec(ws, l, 1), _layer_spec(bs, l, 1),
                  pl.BlockSpec((None, 1, GMLP_WIDTH),
                               lambda i: (l, 0, (RET_WIDTH + MLA_WIDTH) // GMLP_WIDTH))],
        out_specs=pl.BlockSpec((t, GMLP_WIDTH), lambda i: (i, 0)),
        out_shape=jax.ShapeDtypeStruct((s, GMLP_WIDTH), BF16),
        compiler_params=_cparams("arbitrary"),
        name="gmlp",
    )(z, z, gv, ws, bs, mix)


def _out_proj_kernel(x_ref, yr_ref, ym_ref, yg_ref, w_ref, o_ref):
    y = jnp.concatenate([yr_ref[...], ym_ref[...], yg_ref[...]], axis=-1)
    o_ref[...] = x_ref[...] + jnp.dot(y, w_ref[...], preferred_element_type=F32)


def _out_proj(x, yr, ym, yg, w, l, tm):
    s, d = x.shape
    blk = lambda n: pl.BlockSpec((tm, n), lambda i: (i, 0))
    return pl.pallas_call(
        _out_proj_kernel,
        grid=(s // tm,),
        in_specs=[blk(d), blk(RET_WIDTH), blk(MLA_WIDTH), blk(GMLP_WIDTH), _layer_spec(w, l, 1)],
        out_specs=blk(d),
        out_shape=jax.ShapeDtypeStruct((s, d), F32),
        compiler_params=_cparams("arbitrary"),
        name="out_proj",
    )(x, yr, ym, yg, w)


def _ffn_kernel(x_ref, g_ref, wg_ref, wu_ref, wd_ref, fn_ref, o_ref, h_sc, *, final_norm):
    f = pl.program_id(1)

    @pl.when(f == 0)
    def _():
        x = x_ref[...]
        h_sc[...] = _rms(x, g_ref[...]).astype(BF16)
        o_ref[...] = x

    h = h_sc[...]
    gate = jnp.dot(h, wg_ref[...], preferred_element_type=F32)
    up = jnp.dot(h, wu_ref[...], preferred_element_type=F32)
    act = (gate / (1.0 + jnp.exp(-gate)) * up).astype(BF16)
    o_ref[...] += jnp.dot(act, wd_ref[...], preferred_element_type=F32)

    if final_norm:
        @pl.when(f == pl.num_programs(1) - 1)
        def _():
            o_ref[...] = _rms(o_ref[...], fn_ref[...])


def _ffn(x, gain, wg, wu, wd, fn_gain, l, tm, tf, final_norm):
    s, d = x.shape
    dff = wg.shape[2]
    return pl.pallas_call(
        functools.partial(_ffn_kernel, final_norm=final_norm),
        grid=(s // tm, dff // tf),
        in_specs=[pl.BlockSpec((tm, d), lambda i, f: (i, 0)),
                  _layer_spec(gain, l, 2),
                  pl.BlockSpec((None, d, tf), lambda i, f: (l, 0, f)),
                  pl.BlockSpec((None, d, tf), lambda i, f: (l, 0, f)),
                  pl.BlockSpec((None, tf, d), lambda i, f: (l, f, 0)),
                  pl.BlockSpec((1, d), lambda i, f: (0, 0))],
        out_specs=pl.BlockSpec((tm, d), lambda i, f: (i, 0)),
        out_shape=jax.ShapeDtypeStruct((s, d), F32),
        scratch_shapes=[pltpu.VMEM((tm, d), BF16)],
        compiler_params=_cparams("arbitrary", "arbitrary"),
        name="ffn",
    )(x, gain, wg, wu, wd, fn_gain)


def _spread_rope_cols(w):
    z = jnp.zeros(w.shape[:-1] + (32,), w.dtype)
    return jnp.concatenate([w[..., :32], z, w[..., 32:], z], axis=-1)


def _prep_w_in_tail(w_in):
    l, d, _ = w_in.shape
    tail = w_in[..., Z_MAIN:].astype(BF16)
    kpe = _spread_rope_cols(tail[..., :QK_ROPE_DIM])
    pad = jnp.zeros((l, d, Z_TAIL - 2 * GMLP_WIDTH - LANES), BF16)
    return jnp.concatenate([tail[..., QK_ROPE_DIM:], kpe, pad], axis=-1)


def _prep_w_uq(w_uq):
    l, r, _ = w_uq.shape
    w = w_uq.reshape(l, r, N_MLA_HEADS, QK_HEAD_DIM)
    w = jnp.concatenate([w[..., :QK_NOPE_DIM], _spread_rope_cols(w[..., QK_NOPE_DIM:])], axis=-1)
    return w.reshape(l, r, N_MLA_HEADS * QK_PAD).astype(BF16)


def _prep_w_ukv(w_ukv):
    l, r, _ = w_ukv.shape
    w = w_ukv.reshape(l, r, N_MLA_HEADS, 2 * HEAD_DIM)
    k_nope = w[..., :HEAD_DIM].reshape(l, r, MLA_WIDTH)
    v = w[..., HEAD_DIM:].reshape(l, r, MLA_WIDTH)
    return jnp.concatenate([k_nope, v], axis=-1).astype(BF16)


def _tiles(s):
    pick = lambda pref: pref if s % pref == 0 else s
    attn = pick(512)
    return dict(in_tm=pick(1024), ret_t=pick(512), ret_chunk=128, mla_tm=attn, attn_tq=attn,
                gmlp_t=pick(512), out_tm=pick(512), ffn_tm=pick(1024), ffn_tf=512)


def kernel(x, positions, attn_norm, w_in, mla_q_norm, w_uq, mla_kv_norm, w_ukv, gmlp_v_norm, gmlp_w_s,
           gmlp_b_s, mix_norm, w_out, ffn_norm, w_gate, w_up, w_down, final_norm):
    b, s, d = x.shape
    assert b == 1, "batch is folded away; only BATCH == 1 is supported"
    depth = w_in.shape[0]
    t = _tiles(s)

    w_in_tail = _prep_w_in_tail(w_in)
    w_uq_b = _prep_w_uq(w_uq)
    w_ukv_b = _prep_w_ukv(w_ukv)
    w_out_b = w_out.astype(BF16)
    w_gate_b = w_gate.astype(BF16)
    w_up_b = w_up.astype(BF16)
    w_down_b = w_down.astype(BF16)
    rows = lambda a: a.reshape(depth, 1, -1)
    attn_g, q_g, kv_g, gv_g, mix_g, ffn_g = map(
        rows, (attn_norm, mla_q_norm, mla_kv_norm, gmlp_v_norm, mix_norm, ffn_norm))
    b_s = gmlp_b_s[..., None]
    fn_g = final_norm.reshape(1, d)

    cos_r, sin_r, cos_m, sin_m = _rope_tables(positions)
    ret_tables = _retention_tables(t["ret_chunk"])
    xs = x.reshape(s, d)
    for l in range(depth):
        z = _in_proj(xs, attn_g, w_in, w_in_tail, l, t["in_tm"])
        y_ret = _retention(z, cos_r, sin_r, ret_tables, mix_g, l, t["ret_t"], t["ret_chunk"])
        q, k, vt = _mla_proj(z, q_g, kv_g, w_uq_b, w_ukv_b, cos_m, sin_m, l, t["mla_tm"])
        y_mla = _attention(q, k, vt, mix_g, l, t["attn_tq"])
        y_gm = _gmlp(z, gv_g, gmlp_w_s, b_s, mix_g, l, t["gmlp_t"])
        xs = _out_proj(xs, y_ret, y_mla, y_gm, w_out_b, l, t["out_tm"])
        xs = _ffn(xs, ffn_g, w_gate_b, w_up_b, w_down_b, fn_g, l, t["ffn_tm"], t["ffn_tf"],
                  final_norm=(l == depth - 1))
    return xs.reshape(b, s, d)
```

```python
import functools

import jax
import jax.numpy as jnp
from jax import lax
from jax.experimental import pallas as pl
from jax.experimental.pallas import tpu as pltpu

F32 = jnp.float32
BF16 = jnp.bfloat16

HEAD_DIM = 128
N_RET_HEADS = 6
N_MLA_HEADS = 6
N_GMLP_GROUPS = 4
RET_WIDTH = N_RET_HEADS * HEAD_DIM
MLA_WIDTH = N_MLA_HEADS * HEAD_DIM
GMLP_WIDTH = N_GMLP_GROUPS * HEAD_DIM
Q_LORA_RANK = 512
KV_LORA_RANK = 512
QK_NOPE_DIM = 128
QK_ROPE_DIM = 64
QK_HEAD_DIM = QK_NOPE_DIM + QK_ROPE_DIM
CHUNK = 128
ROPE_BASE = 10000.0
EPS = 1e-6

LANES = 128
QK_PAD = 2 * LANES
ATTN_HEADS_PER_STEP = 2
VT_ROWS = HEAD_DIM + 16
LOG2_E = 1.4426950408889634

Z_CQ = 4 * RET_WIDTH
Z_CKV = Z_CQ + Q_LORA_RANK
Z_MAIN = Z_CKV + KV_LORA_RANK
Z_GU = Z_MAIN
Z_GV = Z_GU + GMLP_WIDTH
Z_KPE = Z_GV + GMLP_WIDTH
Z_TN = 512
Z_TAIL = 3 * Z_TN
Z_WIDTH = Z_MAIN + Z_TAIL

VMEM_LIMIT = 56 * 1024 * 1024


def _cparams(*sem):
    return pltpu.CompilerParams(dimension_semantics=sem, vmem_limit_bytes=VMEM_LIMIT)


def _layer_spec(arr, l, ngrid):
    zeros = (0,) * (arr.ndim - 1)
    return pl.BlockSpec((None,) + arr.shape[1:], lambda *_: (l,) + zeros)


def _rms(x, gain):
    return x * lax.rsqrt(jnp.mean(x * x, axis=-1, keepdims=True) + EPS) * gain


def _rope(x, cos, sin):
    return x * cos + pltpu.roll(x, 64, 1) * sin


def _rope_table_kernel(pos_ref, inv_r_ref, sgn_r_ref, inv_m_ref, cm_ref, sm_ref,
                       cos_r_ref, sin_r_ref, cos_m_ref, sin_m_ref):
    pos = pos_ref[...].astype(F32)
    ang_r = pos * inv_r_ref[...]
    cos_r_ref[...] = jnp.cos(ang_r)
    sin_r_ref[...] = jnp.sin(ang_r) * sgn_r_ref[...]
    ang_m = pos * inv_m_ref[...]
    cos_m_ref[...] = jnp.cos(ang_m) * cm_ref[...]
    sin_m_ref[...] = jnp.sin(ang_m) * sm_ref[...]


def _rope_tables(positions):
    s = positions.shape[-1]
    ts = min(s, 1024)
    pos = positions.reshape(s, 1)
    inv_r = 1.0 / (ROPE_BASE ** (jnp.arange(0, HEAD_DIM, 2, dtype=F32) / HEAD_DIM))
    inv_m = 1.0 / (ROPE_BASE ** (jnp.arange(0, QK_ROPE_DIM, 2, dtype=F32) / QK_ROPE_DIM))
    z32 = jnp.zeros((32,), F32)
    o32 = jnp.ones((32,), F32)
    inv_r_full = jnp.concatenate([inv_r, inv_r]).reshape(1, LANES)
    sgn_r = jnp.concatenate([-jnp.ones((64,), F32), jnp.ones((64,), F32)]).reshape(1, LANES)
    inv_m_full = jnp.concatenate([inv_m, z32, inv_m, z32]).reshape(1, LANES)
    cmask = jnp.concatenate([o32, z32, o32, z32]).reshape(1, LANES)
    smask = jnp.concatenate([-o32, z32, o32, z32]).reshape(1, LANES)
    row = pl.BlockSpec((1, LANES), lambda i: (0, 0))
    tab = pl.BlockSpec((ts, LANES), lambda i: (i, 0))
    out = jax.ShapeDtypeStruct((s, LANES), F32)
    return pl.pallas_call(
        _rope_table_kernel,
        grid=(s // ts,),
        in_specs=[pl.BlockSpec((ts, 1), lambda i: (i, 0)), row, row, row, row, row],
        out_specs=[tab, tab, tab, tab],
        out_shape=[out, out, out, out],
        compiler_params=_cparams("arbitrary"),
        name="rope_tables",
    )(pos, inv_r_full, sgn_r, inv_m_full, cmask, smask)


def _in_proj_kernel(x_ref, g_ref, wm_ref, wt_ref, o_ref, h_sc, *, n_main):
    j = pl.program_id(1)

    @pl.when(j == 0)
    def _():
        h_sc[...] = _rms(x_ref[...], g_ref[...]).astype(BF16)

    @pl.when(j < n_main)
    def _():
        o_ref[...] = jnp.dot(h_sc[...], wm_ref[...], preferred_element_type=F32).astype(o_ref.dtype)

    @pl.when(j >= n_main)
    def _():
        o_ref[...] = jnp.dot(h_sc[...], wt_ref[...], preferred_element_type=F32).astype(o_ref.dtype)


def _in_proj(x, gain, w_in, w_tail, l, tm):
    s, d = x.shape
    n_main = Z_MAIN // Z_TN
    n_tail = Z_TAIL // Z_TN
    return pl.pallas_call(
        functools.partial(_in_proj_kernel, n_main=n_main),
        grid=(s // tm, n_main + n_tail),
        in_specs=[pl.BlockSpec((tm, d), lambda i, j: (i, 0)),
                  _layer_spec(gain, l, 2),
                  pl.BlockSpec((None, d, Z_TN), lambda i, j: (l, 0, jnp.minimum(j, n_main - 1))),
                  pl.BlockSpec((None, d, Z_TN), lambda i, j: (l, 0, jnp.maximum(j - n_main, 0)))],
        out_specs=pl.BlockSpec((tm, Z_TN), lambda i, j: (i, j)),
        out_shape=jax.ShapeDtypeStruct((s, Z_WIDTH), BF16),
        scratch_shapes=[pltpu.VMEM((tm, d), BF16)],
        compiler_params=_cparams("arbitrary", "arbitrary"),
        name="in_proj",
    )(x, gain, w_in, w_tail)


def _retention_kernel(q_ref, k_ref, v_ref, g_ref, cos_ref, sin_ref, inner_ref, qdec_ref, kdec_ref,
                      cdec_ref, gain_ref, o_ref, state_sc, *, chunk, n_chunks):
    @pl.when(pl.program_id(0) == 0)
    def _():
        state_sc[...] = jnp.zeros_like(state_sc)

    for c in range(n_chunks):
        rows = slice(c * chunk, (c + 1) * chunk)
        cos = cos_ref[rows, :]
        sin = sin_ref[rows, :]
        for h in range(N_RET_HEADS):
            cols = slice(h * HEAD_DIM, (h + 1) * HEAD_DIM)
            q = _rope(q_ref[rows, cols].astype(F32), cos, sin)
            k = _rope(k_ref[rows, cols].astype(F32), cos, sin) * (HEAD_DIM ** -0.5)
            v = v_ref[rows, cols]
            state = state_sc[h]
            scores = lax.dot_general(q.astype(BF16), k.astype(BF16), (((1,), (1,)), ((), ())),
                                     preferred_element_type=F32) * inner_ref[h]
            out = jnp.dot(scores.astype(BF16), v, preferred_element_type=F32)
            out += jnp.dot((q * qdec_ref[:, cols]).astype(BF16), state.astype(BF16),
                           preferred_element_type=F32)
            kt = (k * kdec_ref[:, cols]).T.astype(BF16)
            state_sc[h] = cdec_ref[:, cols] * state + jnp.dot(kt, v, preferred_element_type=F32)
            y = _rms(out, gain_ref[:, cols])
            gate = g_ref[rows, cols].astype(F32)
            o_ref[rows, cols] = (y * (gate / (1.0 + jnp.exp(-gate)))).astype(o_ref.dtype)


def _retention_tables(chunk):
    h = N_RET_HEADS
    log_gamma = jnp.log1p(-jnp.exp2(-5.0 - jnp.arange(h, dtype=F32)))
    idx = jnp.arange(chunk, dtype=F32)
    rel = idx[:, None] - idx[None, :]
    inner = jnp.where(rel >= 0, jnp.exp(log_gamma[:, None, None] * jnp.maximum(rel, 0.0)), 0.0)
    qdec = jnp.exp(log_gamma[None, :] * (idx[:, None] + 1.0))
    kdec = jnp.exp(log_gamma[None, :] * (chunk - 1.0 - idx[:, None]))
    cdec = jnp.exp(log_gamma * chunk)
    rep = lambda a: jnp.repeat(a, HEAD_DIM, axis=-1)
    return inner, rep(qdec), rep(kdec), rep(cdec[None, :])


def _retention(z, cos_r, sin_r, tables, mix, l, t, chunk):
    s = z.shape[0]
    zspec = lambda c: pl.BlockSpec((t, RET_WIDTH), lambda i: (i, c))
    tab = pl.BlockSpec((t, LANES), lambda i: (i, 0))
    full = lambda a: pl.BlockSpec(a.shape, lambda i: (0,) * a.ndim)
    return pl.pallas_call(
        functools.partial(_retention_kernel, chunk=chunk, n_chunks=t // chunk),
        grid=(s // t,),
        in_specs=[zspec(0), zspec(1), zspec(2), zspec(3), tab, tab] + [full(a) for a in tables]
        + [pl.BlockSpec((None, 1, RET_WIDTH), lambda i: (l, 0, 0))],
        out_specs=pl.BlockSpec((t, RET_WIDTH), lambda i: (i, 0)),
        out_shape=jax.ShapeDtypeStruct((s, RET_WIDTH), BF16),
        scratch_shapes=[pltpu.VMEM((N_RET_HEADS, HEAD_DIM, HEAD_DIM), F32)],
        compiler_params=_cparams("arbitrary"),
        name="retention",
    )(z, z, z, z, cos_r, sin_r, *tables, mix)


def _mla_proj_kernel(cq_ref, ckv_ref, kpe_ref, gq_ref, gkv_ref, wuq_ref, wukv_ref, cos_ref, sin_ref,
                     qt_ref, k_ref, vt_ref):
    cos = cos_ref[...]
    sin = sin_ref[...]
    scale = QK_HEAD_DIM ** -0.5 * LOG2_E
    cq = _rms(cq_ref[...].astype(F32), gq_ref[...]).astype(BF16)
    mq = jnp.dot(cq, wuq_ref[...], preferred_element_type=F32)
    ckv = _rms(ckv_ref[...].astype(F32), gkv_ref[...]).astype(BF16)
    mkv = jnp.dot(ckv, wukv_ref[...], preferred_element_type=F32)
    kpe = _rope(kpe_ref[...].astype(F32), cos, sin).astype(BF16)
    for h in range(N_MLA_HEADS):
        nope = slice(h * QK_PAD, h * QK_PAD + LANES)
        pe = slice(h * QK_PAD + LANES, (h + 1) * QK_PAD)
        qt_ref[nope, :] = (mq[:, nope] * scale).T.astype(BF16)
        qt_ref[pe, :] = (_rope(mq[:, pe], cos, sin) * scale).T.astype(BF16)
        k_ref[:, nope] = mkv[:, h * LANES:(h + 1) * LANES].astype(BF16)
        k_ref[:, pe] = kpe
    vt = mkv[:, MLA_WIDTH:].T.astype(BF16)
    tm = vt.shape[1]
    extra = lax.broadcasted_iota(jnp.int32, (VT_ROWS - HEAD_DIM, tm), 0)
    ones_row = jnp.where(extra == 0, 1.0, 0.0).astype(BF16)
    for h in range(N_MLA_HEADS):
        vt_ref[h * VT_ROWS:h * VT_ROWS + HEAD_DIM, :] = vt[h * HEAD_DIM:(h + 1) * HEAD_DIM, :]
        vt_ref[h * VT_ROWS + HEAD_DIM:(h + 1) * VT_ROWS, :] = ones_row


def _mla_proj(z, gq, gkv, wuq, wukv, cos_m, sin_m, l, tm):
    s = z.shape[0]
    tab = pl.BlockSpec((tm, LANES), lambda i: (i, 0))
    qk = N_MLA_HEADS * QK_PAD
    return pl.pallas_call(
        _mla_proj_kernel,
        grid=(s // tm,),
        in_specs=[pl.BlockSpec((tm, Q_LORA_RANK), lambda i: (i, Z_CQ // Q_LORA_RANK)),
                  pl.BlockSpec((tm, KV_LORA_RANK), lambda i: (i, Z_CKV // KV_LORA_RANK)),
                  pl.BlockSpec((tm, LANES), lambda i: (i, Z_KPE // LANES)),
                  _layer_spec(gq, l, 1), _layer_spec(gkv, l, 1),
                  _layer_spec(wuq, l, 1), _layer_spec(wukv, l, 1), tab, tab],
        out_specs=[pl.BlockSpec((qk, tm), lambda i: (0, i)),
                   pl.BlockSpec((tm, qk), lambda i: (i, 0)),
                   pl.BlockSpec((None, N_MLA_HEADS * VT_ROWS, tm), lambda i: (i, 0, 0))],
        out_shape=[jax.ShapeDtypeStruct((qk, s), BF16),
                   jax.ShapeDtypeStruct((s, qk), BF16),
                   jax.ShapeDtypeStruct((s // tm, N_MLA_HEADS * VT_ROWS, tm), BF16)],
        compiler_params=_cparams("arbitrary"),
        name="mla_proj",
    )(z, z, z, gq, gkv, wuq, wukv, cos_m, sin_m)


def _attn_kernel(qt_ref, k_ref, vt_ref, gain_ref, o_ref, m_sc, acc_sc, st0_sc, st1_sc, *, tq, hp):
    qi = pl.program_id(1)
    m_sc[...] = jnp.full_like(m_sc, -jnp.inf)
    acc_sc[...] = jnp.zeros_like(acc_sc)

    def scores(kb, st_sc):
        start = pl.multiple_of(kb * tq, tq)
        for j in range(hp):
            qk_cols = slice(j * QK_PAD, (j + 1) * QK_PAD)
            st_sc[j] = jnp.dot(k_ref[pl.ds(start, tq), qk_cols], qt_ref[qk_cols, :],
                               preferred_element_type=F32)

    def softmax_pv(kb, st_sc, masked):
        for j in range(hp):
            vt = vt_ref[kb, j * VT_ROWS:(j + 1) * VT_ROWS, :]
            st = st_sc[j]
            if masked:
                kv_pos = lax.broadcasted_iota(jnp.int32, st.shape, 0)
                q_pos = lax.broadcasted_iota(jnp.int32, st.shape, 1)
                st = jnp.where(kv_pos <= q_pos, st, -jnp.inf)
            m_prev = m_sc[j]
            m_new = jnp.maximum(m_prev, jnp.max(st, axis=0, keepdims=True))
            alpha = jnp.exp2(m_prev - m_new)
            p = jnp.exp2(st - m_new).astype(BF16)
            acc_sc[j] = alpha * acc_sc[j] + jnp.dot(vt, p, preferred_element_type=F32)
            m_sc[j] = m_new

    scores(0, st0_sc)

    def pair(i, carry):
        kb = 2 * i
        scores(kb + 1, st1_sc)
        softmax_pv(kb, st0_sc, False)
        scores(kb + 2, st0_sc)
        softmax_pv(kb + 1, st1_sc, False)
        return carry

    lax.fori_loop(0, qi // 2, pair, 0)

    @pl.when(qi % 2 == 0)
    def _():
        softmax_pv(qi, st0_sc, True)

    @pl.when(qi % 2 == 1)
    def _():
        scores(qi, st1_sc)
        softmax_pv(qi - 1, st0_sc, False)
        softmax_pv(qi, st1_sc, True)

    for j in range(hp):
        cols = slice(j * HEAD_DIM, (j + 1) * HEAD_DIM)
        denom = acc_sc[j, HEAD_DIM:HEAD_DIM + 1, :]
        o = (acc_sc[j, :HEAD_DIM, :] * (1.0 / denom)).T
        o_ref[:, cols] = _rms(o, gain_ref[:, cols]).astype(o_ref.dtype)


def _attention(qt, k, vt, mix, l, tq):
    s = k.shape[0]
    hp = ATTN_HEADS_PER_STEP
    width = hp * HEAD_DIM
    return pl.pallas_call(
        functools.partial(_attn_kernel, tq=tq, hp=hp),
        grid=(N_MLA_HEADS // hp, s // tq),
        in_specs=[pl.BlockSpec((hp * QK_PAD, tq), lambda h, i: (h, i)),
                  pl.BlockSpec((s, hp * QK_PAD), lambda h, i: (0, h)),
                  pl.BlockSpec((s // tq, hp * VT_ROWS, tq), lambda h, i: (0, h, 0)),
                  pl.BlockSpec((None, 1, width), lambda h, i: (l, 0, RET_WIDTH // width + h))],
        out_specs=pl.BlockSpec((tq, width), lambda h, i: (i, h)),
        out_shape=jax.ShapeDtypeStruct((s, MLA_WIDTH), BF16),
        scratch_shapes=[pltpu.VMEM((hp, 1, tq), F32),
                        pltpu.VMEM((hp, VT_ROWS, tq), F32),
                        pltpu.VMEM((hp, tq, tq), F32), pltpu.VMEM((hp, tq, tq), F32)],
        compiler_params=_cparams("arbitrary", "arbitrary"),
        name="mla_attention",
    )(qt, k, vt, mix)


def _gmlp_kernel(u_ref, v_ref, gv_ref, ws_ref, bs_ref, gm_ref, o_ref, *, n_chunks):
    row = lax.broadcasted_iota(jnp.int32, (CHUNK, CHUNK), 0)
    col = lax.broadcasted_iota(jnp.int32, (CHUNK, CHUNK), 1)
    for g in range(N_GMLP_GROUPS):
        cols = slice(g * HEAD_DIM, (g + 1) * HEAD_DIM)
        w = jnp.where(col <= row, ws_ref[g], 0.0).astype(BF16)
        bias = bs_ref[g]
        for c in range(n_chunks):
            rows = slice(c * CHUNK, (c + 1) * CHUNK)
            u = jax.nn.gelu(u_ref[rows, cols].astype(F32))
            v = jax.nn.gelu(v_ref[rows, cols].astype(F32))
            vn = _rms(v, gv_ref[:, cols]).astype(BF16)
            sg = jnp.dot(w, vn, preferred_element_type=F32) + bias
            o_ref[rows, cols] = _rms(u * sg, gm_ref[:, cols]).astype(o_ref.dtype)


def _gmlp(z, gv, ws, bs, mix, l, t):
    s = z.shape[0]
    return pl.pallas_call(
        functools.partial(_gmlp_kernel, n_chunks=t // CHUNK),
        grid=(s // t,),
        in_specs=[pl.BlockSpec((t, GMLP_WIDTH), lambda i: (i, Z_GU // GMLP_WIDTH)),
                  pl.BlockSpec((t, GMLP_WIDTH), lambda i: (i, Z_GV // GMLP_WIDTH)),
                  _layer_spec(gv, l, 1), _layer_spec(ws, l, 1), _layer_spec(bs, l, 1),
                  pl.BlockSpec((None, 1, GMLP_WIDTH),
                               lambda i: (l, 0, (RET_WIDTH + MLA_WIDTH) // GMLP_WIDTH))],
        out_specs=pl.BlockSpec((t, GMLP_WIDTH), lambda i: (i, 0)),
        out_shape=jax.ShapeDtypeStruct((s, GMLP_WIDTH), BF16),
        compiler_params=_cparams("arbitrary"),
        name="gmlp",
    )(z, z, gv, ws, bs, mix)


def _out_proj_kernel(x_ref, yr_ref, ym_ref, yg_ref, w_ref, o_ref):
    y = jnp.concatenate([yr_ref[...], ym_ref[...], yg_ref[...]], axis=-1)
    o_ref[...] = x_ref[...] + jnp.dot(y, w_ref[...], preferred_element_type=F32)


def _out_proj(x, yr, ym, yg, w, l, tm):
    s, d = x.shape
    blk = lambda n: pl.BlockSpec((tm, n), lambda i: (i, 0))
    return pl.pallas_call(
        _out_proj_kernel,
        grid=(s // tm,),
        in_specs=[blk(d), blk(RET_WIDTH), blk(MLA_WIDTH), blk(GMLP_WIDTH), _layer_spec(w, l, 1)],
        out_specs=blk(d),
        out_shape=jax.ShapeDtypeStruct((s, d), F32),
        compiler_params=_cparams("arbitrary"),
        name="out_proj",
    )(x, yr, ym, yg, w)


def _ffn_kernel(x_ref, g_ref, wg_ref, wu_ref, wd_ref, fn_ref, o_ref, h_sc, *, final_norm):
    f = pl.program_id(1)

    @pl.when(f == 0)
    def _():
        x = x_ref[...]
        h_sc[...] = _rms(x, g_ref[...]).astype(BF16)
        o_ref[...] = x

    h = h_sc[...]
    gate = jnp.dot(h, wg_ref[...], preferred_element_type=F32)
    up = jnp.dot(h, wu_ref[...], preferred_element_type=F32)
    act = (gate / (1.0 + jnp.exp(-gate)) * up).astype(BF16)
    o_ref[...] += jnp.dot(act, wd_ref[...], preferred_element_type=F32)

    if final_norm:
        @pl.when(f == pl.num_programs(1) - 1)
        def _():
            o_ref[...] = _rms(o_ref[...], fn_ref[...])


def _ffn(x, gain, wg, wu, wd, fn_gain, l, tm, tf, final_norm):
    s, d = x.shape
    dff = wg.shape[2]
    return pl.pallas_call(
        functools.partial(_ffn_kernel, final_norm=final_norm),
        grid=(s // tm, dff // tf),
        in_specs=[pl.BlockSpec((tm, d), lambda i, f: (i, 0)),
                  _layer_spec(gain, l, 2),
                  pl.BlockSpec((None, d, tf), lambda i, f: (l, 0, f)),
                  pl.BlockSpec((None, d, tf), lambda i, f: (l, 0, f)),
                  pl.BlockSpec((None, tf, d), lambda i, f: (l, f, 0)),
                  pl.BlockSpec((1, d), lambda i, f: (0, 0))],
        out_specs=pl.BlockSpec((tm, d), lambda i, f: (i, 0)),
        out_shape=jax.ShapeDtypeStruct((s, d), F32),
        scratch_shapes=[pltpu.VMEM((tm, d), BF16)],
        compiler_params=_cparams("arbitrary", "arbitrary"),
        name="ffn",
    )(x, gain, wg, wu, wd, fn_gain)


def _spread_rope_cols(w):
    z = jnp.zeros(w.shape[:-1] + (32,), w.dtype)
    return jnp.concatenate([w[..., :32], z, w[..., 32:], z], axis=-1)


def _prep_w_in(w_in):
    l, d, _ = w_in.shape
    main = w_in[..., :Z_MAIN].astype(BF16)
    kpe = _spread_rope_cols(w_in[..., Z_MAIN:Z_MAIN + QK_ROPE_DIM])
    pad = jnp.zeros((l, d, Z_TAIL - 2 * GMLP_WIDTH - LANES), w_in.dtype)
    tail = jnp.concatenate([w_in[..., Z_MAIN + QK_ROPE_DIM:], kpe, pad], axis=-1).astype(BF16)
    return main, tail


def _prep_w_uq(w_uq):
    l, r, _ = w_uq.shape
    w = w_uq.reshape(l, r, N_MLA_HEADS, QK_HEAD_DIM)
    w = jnp.concatenate([w[..., :QK_NOPE_DIM], _spread_rope_cols(w[..., QK_NOPE_DIM:])], axis=-1)
    return w.reshape(l, r, N_MLA_HEADS * QK_PAD).astype(BF16)


def _prep_w_ukv(w_ukv):
    l, r, _ = w_ukv.shape
    w = w_ukv.reshape(l, r, N_MLA_HEADS, 2 * HEAD_DIM)
    k_nope = w[..., :HEAD_DIM].reshape(l, r, MLA_WIDTH)
    v = w[..., HEAD_DIM:].reshape(l, r, MLA_WIDTH)
    return jnp.concatenate([k_nope, v], axis=-1).astype(BF16)


def _tiles(s):
    pick = lambda pref: pref if s % pref == 0 else s
    attn = pick(512)
    return dict(in_tm=pick(1024), ret_t=pick(512), ret_chunk=128, mla_tm=attn, attn_tq=attn,
                gmlp_t=pick(512), out_tm=pick(512), ffn_tm=pick(1024), ffn_tf=512)


def kernel(x, positions, attn_norm, w_in, mla_q_norm, w_uq, mla_kv_norm, w_ukv, gmlp_v_norm, gmlp_w_s,
           gmlp_b_s, mix_norm, w_out, ffn_norm, w_gate, w_up, w_down, final_norm):
    b, s, d = x.shape
    assert b == 1, "batch is folded away; only BATCH == 1 is supported"
    depth = w_in.shape[0]
    t = _tiles(s)

    w_in_main, w_in_tail = _prep_w_in(w_in)
    w_uq_b = _prep_w_uq(w_uq)
    w_ukv_b = _prep_w_ukv(w_ukv)
    w_out_b = w_out.astype(BF16)
    w_gate_b = w_gate.astype(BF16)
    w_up_b = w_up.astype(BF16)
    w_down_b = w_down.astype(BF16)
    rows = lambda a: a.reshape(depth, 1, -1)
    attn_g, q_g, kv_g, gv_g, mix_g, ffn_g = map(
        rows, (attn_norm, mla_q_norm, mla_kv_norm, gmlp_v_norm, mix_norm, ffn_norm))
    b_s = gmlp_b_s[..., None]
    fn_g = final_norm.reshape(1, d)

    cos_r, sin_r, cos_m, sin_m = _rope_tables(positions)
    ret_tables = _retention_tables(t["ret_chunk"])
    xs = x.reshape(s, d)
    for l in range(depth):
        z = _in_proj(xs, attn_g, w_in_main, w_in_tail, l, t["in_tm"])
        y_ret = _retention(z, cos_r, sin_r, ret_tables, mix_g, l, t["ret_t"], t["ret_chunk"])
        qt, k, vt = _mla_proj(z, q_g, kv_g, w_uq_b, w_ukv_b, cos_m, sin_m, l, t["mla_tm"])
        y_mla = _attention(qt, k, vt, mix_g, l, t["attn_tq"])
        y_gm = _gmlp(z, gv_g, gmlp_w_s, b_s, mix_g, l, t["gmlp_t"])
        xs = _out_proj(xs, y_ret, y_mla, y_gm, w_out_b, l, t["out_tm"])
        xs = _ffn(xs, ffn_g, w_gate_b, w_up_b, w_down_b, fn_g, l, t["ffn_tm"], t["ffn_tf"],
                  final_norm=(l == depth - 1))
    return xs.reshape(b, s, d)
```

```python
import functools

import jax
import jax.numpy as jnp
from jax import lax
from jax.experimental import pallas as pl
from jax.experimental.pallas import tpu as pltpu

F32 = jnp.float32
BF16 = jnp.bfloat16

HEAD_DIM = 128
N_RET_HEADS = 6
N_MLA_HEADS = 6
N_GMLP_GROUPS = 4
RET_WIDTH = N_RET_HEADS * HEAD_DIM
MLA_WIDTH = N_MLA_HEADS * HEAD_DIM
GMLP_WIDTH = N_GMLP_GROUPS * HEAD_DIM
Q_LORA_RANK = 512
KV_LORA_RANK = 512
QK_NOPE_DIM = 128
QK_ROPE_DIM = 64
QK_HEAD_DIM = QK_NOPE_DIM + QK_ROPE_DIM
CHUNK = 128
ROPE_BASE = 10000.0
EPS = 1e-6

LANES = 128
QK_PAD = 2 * LANES
ATTN_HEADS_PER_STEP = 2
VT_ROWS = HEAD_DIM + 16
LOG2_E = 1.4426950408889634

Z_CQ = 4 * RET_WIDTH
Z_CKV = Z_CQ + Q_LORA_RANK
Z_MAIN = Z_CKV + KV_LORA_RANK
Z_TN = 1024
ZT_GU = 0
ZT_GV = ZT_GU + GMLP_WIDTH
ZT_KPE = ZT_GV + GMLP_WIDTH
Z_TAIL = ZT_KPE + LANES

VMEM_LIMIT = 56 * 1024 * 1024


def _cparams(*sem):
    return pltpu.CompilerParams(dimension_semantics=sem, vmem_limit_bytes=VMEM_LIMIT)


def _layer_spec(arr, l, ngrid):
    zeros = (0,) * (arr.ndim - 1)
    return pl.BlockSpec((None,) + arr.shape[1:], lambda *_: (l,) + zeros)


def _rms(x, gain):
    return x * lax.rsqrt(jnp.mean(x * x, axis=-1, keepdims=True) + EPS) * gain


def _rope(x, cos, sin):
    return x * cos + pltpu.roll(x, 64, 1) * sin


def _rope_table_kernel(pos_ref, inv_r_ref, sgn_r_ref, inv_m_ref, cm_ref, sm_ref,
                       cos_r_ref, sin_r_ref, cos_m_ref, sin_m_ref):
    pos = pos_ref[...].astype(F32)
    ang_r = pos * inv_r_ref[...]
    cos_r_ref[...] = jnp.cos(ang_r)
    sin_r_ref[...] = jnp.sin(ang_r) * sgn_r_ref[...]
    ang_m = pos * inv_m_ref[...]
    cos_m_ref[...] = jnp.cos(ang_m) * cm_ref[...]
    sin_m_ref[...] = jnp.sin(ang_m) * sm_ref[...]


def _rope_tables(positions):
    s = positions.shape[-1]
    ts = min(s, 1024)
    pos = positions.reshape(s, 1)
    inv_r = 1.0 / (ROPE_BASE ** (jnp.arange(0, HEAD_DIM, 2, dtype=F32) / HEAD_DIM))
    inv_m = 1.0 / (ROPE_BASE ** (jnp.arange(0, QK_ROPE_DIM, 2, dtype=F32) / QK_ROPE_DIM))
    z32 = jnp.zeros((32,), F32)
    o32 = jnp.ones((32,), F32)
    inv_r_full = jnp.concatenate([inv_r, inv_r]).reshape(1, LANES)
    sgn_r = jnp.concatenate([-jnp.ones((64,), F32), jnp.ones((64,), F32)]).reshape(1, LANES)
    inv_m_full = jnp.concatenate([inv_m, z32, inv_m, z32]).reshape(1, LANES)
    cmask = jnp.concatenate([o32, z32, o32, z32]).reshape(1, LANES)
    smask = jnp.concatenate([-o32, z32, o32, z32]).reshape(1, LANES)
    row = pl.BlockSpec((1, LANES), lambda i: (0, 0))
    tab = pl.BlockSpec((ts, LANES), lambda i: (i, 0))
    out = jax.ShapeDtypeStruct((s, LANES), F32)
    return pl.pallas_call(
        _rope_table_kernel,
        grid=(s // ts,),
        in_specs=[pl.BlockSpec((ts, 1), lambda i: (i, 0)), row, row, row, row, row],
        out_specs=[tab, tab, tab, tab],
        out_shape=[out, out, out, out],
        compiler_params=_cparams("arbitrary"),
        name="rope_tables",
    )(pos, inv_r_full, sgn_r, inv_m_full, cmask, smask)


def _in_proj_kernel(x_ref, g_ref, wm_ref, wt_ref, zm_ref, zt_ref, h_sc, *, n_main):
    j = pl.program_id(1)

    @pl.when(j == 0)
    def _():
        h_sc[...] = _rms(x_ref[...], g_ref[...]).astype(BF16)

    @pl.when(j < n_main)
    def _():
        zm_ref[...] = jnp.dot(h_sc[...], wm_ref[...], preferred_element_type=F32).astype(zm_ref.dtype)

    @pl.when(j == n_main)
    def _():
        zt_ref[...] = jnp.dot(h_sc[...], wt_ref[...], preferred_element_type=F32).astype(zt_ref.dtype)


def _in_proj(x, gain, w_in_b, w_tail, l, tm):
    s, d = x.shape
    n_main = Z_MAIN // Z_TN
    last = n_main - 1
    return pl.pallas_call(
        functools.partial(_in_proj_kernel, n_main=n_main),
        grid=(s // tm, n_main + 1),
        in_specs=[pl.BlockSpec((tm, d), lambda i, j: (i, 0)),
                  _layer_spec(gain, l, 2),
                  pl.BlockSpec((None, d, Z_TN), lambda i, j: (l, 0, jnp.minimum(j, last))),
                  _layer_spec(w_tail, l, 2)],
        out_specs=[pl.BlockSpec((tm, Z_TN), lambda i, j: (i, jnp.minimum(j, last))),
                   pl.BlockSpec((tm, Z_TAIL), lambda i, j: (i, 0))],
        out_shape=[jax.ShapeDtypeStruct((s, Z_MAIN), BF16),
                   jax.ShapeDtypeStruct((s, Z_TAIL), BF16)],
        scratch_shapes=[pltpu.VMEM((tm, d), BF16)],
        compiler_params=_cparams("arbitrary", "arbitrary"),
        name="in_proj",
    )(x, gain, w_in_b, w_tail)


def _retention_kernel(q_ref, k_ref, v_ref, g_ref, cos_ref, sin_ref, inner_ref, qdec_ref, kdec_ref,
                      cdec_ref, gain_ref, o_ref, state_sc, *, chunk, n_chunks):
    @pl.when(pl.program_id(0) == 0)
    def _():
        state_sc[...] = jnp.zeros_like(state_sc)

    for c in range(n_chunks):
        rows = slice(c * chunk, (c + 1) * chunk)
        cos = cos_ref[rows, :]
        sin = sin_ref[rows, :]
        for h in range(N_RET_HEADS):
            cols = slice(h * HEAD_DIM, (h + 1) * HEAD_DIM)
            q = _rope(q_ref[rows, cols].astype(F32), cos, sin)
            k = _rope(k_ref[rows, cols].astype(F32), cos, sin) * (HEAD_DIM ** -0.5)
            v = v_ref[rows, cols]
            state = state_sc[h]
            scores = lax.dot_general(q.astype(BF16), k.astype(BF16), (((1,), (1,)), ((), ())),
                                     preferred_element_type=F32) * inner_ref[h]
            out = jnp.dot(scores.astype(BF16), v, preferred_element_type=F32)
            out += jnp.dot((q * qdec_ref[:, cols]).astype(BF16), state.astype(BF16),
                           preferred_element_type=F32)
            kt = (k * kdec_ref[:, cols]).T.astype(BF16)
            state_sc[h] = cdec_ref[:, cols] * state + jnp.dot(kt, v, preferred_element_type=F32)
            y = _rms(out, gain_ref[:, cols])
            gate = g_ref[rows, cols].astype(F32)
            o_ref[rows, cols] = (y * (gate / (1.0 + jnp.exp(-gate)))).astype(o_ref.dtype)


def _retention_tables(chunk):
    h = N_RET_HEADS
    log_gamma = jnp.log1p(-jnp.exp2(-5.0 - jnp.arange(h, dtype=F32)))
    idx = jnp.arange(chunk, dtype=F32)
    rel = idx[:, None] - idx[None, :]
    inner = jnp.where(rel >= 0, jnp.exp(log_gamma[:, None, None] * jnp.maximum(rel, 0.0)), 0.0)
    qdec = jnp.exp(log_gamma[None, :] * (idx[:, None] + 1.0))
    kdec = jnp.exp(log_gamma[None, :] * (chunk - 1.0 - idx[:, None]))
    cdec = jnp.exp(log_gamma * chunk)
    rep = lambda a: jnp.repeat(a, HEAD_DIM, axis=-1)
    return inner, rep(qdec), rep(kdec), rep(cdec[None, :])


def _retention(z, cos_r, sin_r, tables, mix, l, t, chunk):
    s = z.shape[0]
    zspec = lambda c: pl.BlockSpec((t, RET_WIDTH), lambda i: (i, c))
    tab = pl.BlockSpec((t, LANES), lambda i: (i, 0))
    full = lambda a: pl.BlockSpec(a.shape, lambda i: (0,) * a.ndim)
    return pl.pallas_call(
        functools.partial(_retention_kernel, chunk=chunk, n_chunks=t // chunk),
        grid=(s // t,),
        in_specs=[zspec(0), zspec(1), zspec(2), zspec(3), tab, tab] + [full(a) for a in tables]
        + [pl.BlockSpec((None, 1, RET_WIDTH), lambda i: (l, 0, 0))],
        out_specs=pl.BlockSpec((t, RET_WIDTH), lambda i: (i, 0)),
        out_shape=jax.ShapeDtypeStruct((s, RET_WIDTH), BF16),
        scratch_shapes=[pltpu.VMEM((N_RET_HEADS, HEAD_DIM, HEAD_DIM), F32)],
        compiler_params=_cparams("arbitrary"),
        name="retention",
    )(z, z, z, z, cos_r, sin_r, *tables, mix)


def _mla_proj_kernel(cq_ref, ckv_ref, kpe_ref, gq_ref, gkv_ref, wuq_ref, wukv_ref, cos_ref, sin_ref,
                     qt_ref, k_ref, vt_ref):
    cos = cos_ref[...]
    sin = sin_ref[...]
    scale = QK_HEAD_DIM ** -0.5 * LOG2_E
    cq = _rms(cq_ref[...].astype(F32), gq_ref[...]).astype(BF16)
    mq = jnp.dot(cq, wuq_ref[...], preferred_element_type=F32)
    ckv = _rms(ckv_ref[...].astype(F32), gkv_ref[...]).astype(BF16)
    mkv = jnp.dot(ckv, wukv_ref[...], preferred_element_type=F32)
    kpe = _rope(kpe_ref[...].astype(F32), cos, sin).astype(BF16)
    for h in range(N_MLA_HEADS):
        nope = slice(h * QK_PAD, h * QK_PAD + LANES)
        pe = slice(h * QK_PAD + LANES, (h + 1) * QK_PAD)
        qt_ref[nope, :] = (mq[:, nope] * scale).T.astype(BF16)
        qt_ref[pe, :] = (_rope(mq[:, pe], cos, sin) * scale).T.astype(BF16)
        k_ref[:, nope] = mkv[:, h * LANES:(h + 1) * LANES].astype(BF16)
        k_ref[:, pe] = kpe
    vt = mkv[:, MLA_WIDTH:].T.astype(BF16)
    tm = vt.shape[1]
    extra = lax.broadcasted_iota(jnp.int32, (VT_ROWS - HEAD_DIM, tm), 0)
    ones_row = jnp.where(extra == 0, 1.0, 0.0).astype(BF16)
    for h in range(N_MLA_HEADS):
        vt_ref[h * VT_ROWS:h * VT_ROWS + HEAD_DIM, :] = vt[h * HEAD_DIM:(h + 1) * HEAD_DIM, :]
        vt_ref[h * VT_ROWS + HEAD_DIM:(h + 1) * VT_ROWS, :] = ones_row


def _mla_proj(z, zt, gq, gkv, wuq, wukv, cos_m, sin_m, l, tm):
    s = z.shape[0]
    tab = pl.BlockSpec((tm, LANES), lambda i: (i, 0))
    qk = N_MLA_HEADS * QK_PAD
    return pl.pallas_call(
        _mla_proj_kernel,
        grid=(s // tm,),
        in_specs=[pl.BlockSpec((tm, Q_LORA_RANK), lambda i: (i, Z_CQ // Q_LORA_RANK)),
                  pl.BlockSpec((tm, KV_LORA_RANK), lambda i: (i, Z_CKV // KV_LORA_RANK)),
                  pl.BlockSpec((tm, LANES), lambda i: (i, ZT_KPE // LANES)),
                  _layer_spec(gq, l, 1), _layer_spec(gkv, l, 1),
                  _layer_spec(wuq, l, 1), _layer_spec(wukv, l, 1), tab, tab],
        out_specs=[pl.BlockSpec((qk, tm), lambda i: (0, i)),
                   pl.BlockSpec((tm, qk), lambda i: (i, 0)),
                   pl.BlockSpec((None, N_MLA_HEADS * VT_ROWS, tm), lambda i: (i, 0, 0))],
        out_shape=[jax.ShapeDtypeStruct((qk, s), BF16),
                   jax.ShapeDtypeStruct((s, qk), BF16),
                   jax.ShapeDtypeStruct((s // tm, N_MLA_HEADS * VT_ROWS, tm), BF16)],
        compiler_params=_cparams("arbitrary"),
        name="mla_proj",
    )(z, z, zt, gq, gkv, wuq, wukv, cos_m, sin_m)


def _attn_kernel(qt_ref, k_ref, vt_ref, gain_ref, o_ref, m_sc, acc_sc, st0_sc, st1_sc, *, tq, hp):
    qi = pl.program_id(1)
    m_sc[...] = jnp.full_like(m_sc, -jnp.inf)
    acc_sc[...] = jnp.zeros_like(acc_sc)

    def scores(kb, st_sc):
        start = pl.multiple_of(kb * tq, tq)
        for j in range(hp):
            qk_cols = slice(j * QK_PAD, (j + 1) * QK_PAD)
            st_sc[j] = jnp.dot(k_ref[pl.ds(start, tq), qk_cols], qt_ref[qk_cols, :],
                               preferred_element_type=F32)

    def softmax_pv(kb, st_sc, masked):
        for j in range(hp):
            vt = vt_ref[kb, j * VT_ROWS:(j + 1) * VT_ROWS, :]
            st = st_sc[j]
            if masked:
                kv_pos = lax.broadcasted_iota(jnp.int32, st.shape, 0)
                q_pos = lax.broadcasted_iota(jnp.int32, st.shape, 1)
                st = jnp.where(kv_pos <= q_pos, st, -jnp.inf)
            m_prev = m_sc[j]
            m_new = jnp.maximum(m_prev, jnp.max(st, axis=0, keepdims=True))
            alpha = jnp.exp2(m_prev - m_new)
            p = jnp.exp2(st - m_new).astype(BF16)
            acc_sc[j] = alpha * acc_sc[j] + jnp.dot(vt, p, preferred_element_type=F32)
            m_sc[j] = m_new

    scores(0, st0_sc)

    def pair(i, carry):
        kb = 2 * i
        scores(kb + 1, st1_sc)
        softmax_pv(kb, st0_sc, False)
        scores(kb + 2, st0_sc)
        softmax_pv(kb + 1, st1_sc, False)
        return carry

    lax.fori_loop(0, qi // 2, pair, 0)

    @pl.when(qi % 2 == 0)
    def _():
        softmax_pv(qi, st0_sc, True)

    @pl.when(qi % 2 == 1)
    def _():
        scores(qi, st1_sc)
        softmax_pv(qi - 1, st0_sc, False)
        softmax_pv(qi, st1_sc, True)

    for j in range(hp):
        cols = slice(j * HEAD_DIM, (j + 1) * HEAD_DIM)
        denom = acc_sc[j, HEAD_DIM:HEAD_DIM + 1, :]
        o = (acc_sc[j, :HEAD_DIM, :] * (1.0 / denom)).T
        o_ref[:, cols] = _rms(o, gain_ref[:, cols]).astype(o_ref.dtype)


def _attention(qt, k, vt, mix, l, tq):
    s = k.shape[0]
    hp = ATTN_HEADS_PER_STEP
    width = hp * HEAD_DIM
    return pl.pallas_call(
        functools.partial(_attn_kernel, tq=tq, hp=hp),
        grid=(N_MLA_HEADS // hp, s // tq),
        in_specs=[pl.BlockSpec((hp * QK_PAD, tq), lambda h, i: (h, i)),
                  pl.BlockSpec((s, hp * QK_PAD), lambda h, i: (0, h)),
                  pl.BlockSpec((s // tq, hp * VT_ROWS, tq), lambda h, i: (0, h, 0)),
                  pl.BlockSpec((None, 1, width), lambda h, i: (l, 0, RET_WIDTH // width + h))],
        out_specs=pl.BlockSpec((tq, width), lambda h, i: (i, h)),
        out_shape=jax.ShapeDtypeStruct((s, MLA_WIDTH), BF16),
        scratch_shapes=[pltpu.VMEM((hp, 1, tq), F32),
                        pltpu.VMEM((hp, VT_ROWS, tq), F32),
                        pltpu.VMEM((hp, tq, tq), F32), pltpu.VMEM((hp, tq, tq), F32)],
        compiler_params=_cparams("arbitrary", "arbitrary"),
        name="mla_attention",
    )(qt, k, vt, mix)


def _gmlp_kernel(u_ref, v_ref, gv_ref, ws_ref, bs_ref, gm_ref, o_ref, *, n_chunks):
    row = lax.broadcasted_iota(jnp.int32, (CHUNK, CHUNK), 0)
    col = lax.broadcasted_iota(jnp.int32, (CHUNK, CHUNK), 1)
    for g in range(N_GMLP_GROUPS):
        cols = slice(g * HEAD_DIM, (g + 1) * HEAD_DIM)
        w = jnp.where(col <= row, ws_ref[g], 0.0).astype(BF16)
        bias = bs_ref[g]
        for c in range(n_chunks):
            rows = slice(c * CHUNK, (c + 1) * CHUNK)
            u = jax.nn.gelu(u_ref[rows, cols].astype(F32))
            v = jax.nn.gelu(v_ref[rows, cols].astype(F32))
            vn = _rms(v, gv_ref[:, cols]).astype(BF16)
            sg = jnp.dot(w, vn, preferred_element_type=F32) + bias
            o_ref[rows, cols] = _rms(u * sg, gm_ref[:, cols]).astype(o_ref.dtype)


def _gmlp(zt, gv, ws, bs, mix, l, t):
    s = zt.shape[0]
    return pl.pallas_call(
        functools.partial(_gmlp_kernel, n_chunks=t // CHUNK),
        grid=(s // t,),
        in_specs=[pl.BlockSpec((t, GMLP_WIDTH), lambda i: (i, ZT_GU // GMLP_WIDTH)),
                  pl.BlockSpec((t, GMLP_WIDTH), lambda i: (i, ZT_GV // GMLP_WIDTH)),
                  _layer_spec(gv, l, 1), _layer_spec(ws, l, 1), _layer_spec(bs, l, 1),
                  pl.BlockSpec((None, 1, GMLP_WIDTH),
                               lambda i: (l, 0, (RET_WIDTH + MLA_WIDTH) // GMLP_WIDTH))],
        out_specs=pl.BlockSpec((t, GMLP_WIDTH), lambda i: (i, 0)),
        out_shape=jax.ShapeDtypeStruct((s, GMLP_WIDTH), BF16),
        compiler_params=_cparams("arbitrary"),
        name="gmlp",
    )(zt, zt, gv, ws, bs, mix)


def _out_proj_kernel(x_ref, yr_ref, ym_ref, yg_ref, w_ref, o_ref):
    y = jnp.concatenate([yr_ref[...], ym_ref[...], yg_ref[...]], axis=-1)
    o_ref[...] = x_ref[...] + jnp.dot(y, w_ref[...], preferred_element_type=F32)


def _out_proj(x, yr, ym, yg, w, l, tm):
    s, d = x.shape
    blk = lambda n: pl.BlockSpec((tm, n), lambda i: (i, 0))
    return pl.pallas_call(
        _out_proj_kernel,
        grid=(s // tm,),
        in_specs=[blk(d), blk(RET_WIDTH), blk(MLA_WIDTH), blk(GMLP_WIDTH), _layer_spec(w, l, 1)],
        out_specs=blk(d),
        out_shape=jax.ShapeDtypeStruct((s, d), F32),
        compiler_params=_cparams("arbitrary"),
        name="out_proj",
    )(x, yr, ym, yg, w)


def _ffn_kernel(x_ref, g_ref, wg_ref, wu_ref, wd_ref, fn_ref, o_ref, h_sc, *, final_norm):
    f = pl.program_id(1)

    @pl.when(f == 0)
    def _():
        x = x_ref[...]
        h_sc[...] = _rms(x, g_ref[...]).astype(BF16)
        o_ref[...] = x

    h = h_sc[...]
    gate = jnp.dot(h, wg_ref[...], preferred_element_type=F32)
    up = jnp.dot(h, wu_ref[...], preferred_element_type=F32)
    act = (gate / (1.0 + jnp.exp(-gate)) * up).astype(BF16)
    o_ref[...] += jnp.dot(act, wd_ref[...], preferred_element_type=F32)

    if final_norm:
        @pl.when(f == pl.num_programs(1) - 1)
        def _():
            o_ref[...] = _rms(o_ref[...], fn_ref[...])


def _ffn(x, gain, wg, wu, wd, fn_gain, l, tm, tf, final_norm):
    s, d = x.shape
    dff = wg.shape[2]
    return pl.pallas_call(
        functools.partial(_ffn_kernel, final_norm=final_norm),
        grid=(s // tm, dff // tf),
        in_specs=[pl.BlockSpec((tm, d), lambda i, f: (i, 0)),
                  _layer_spec(gain, l, 2),
                  pl.BlockSpec((None, d, tf), lambda i, f: (l, 0, f)),
                  pl.BlockSpec((None, d, tf), lambda i, f: (l, 0, f)),
                  pl.BlockSpec((None, tf, d), lambda i, f: (l, f, 0)),
                  pl.BlockSpec((1, d), lambda i, f: (0, 0))],
        out_specs=pl.BlockSpec((tm, d), lambda i, f: (i, 0)),
        out_shape=jax.ShapeDtypeStruct((s, d), F32),
        scratch_shapes=[pltpu.VMEM((tm, d), BF16)],
        compiler_params=_cparams("arbitrary", "arbitrary"),
        name="ffn",
    )(x, gain, wg, wu, wd, fn_gain)


def _spread_rope_cols(w):
    z = jnp.zeros(w.shape[:-1] + (32,), w.dtype)
    return jnp.concatenate([w[..., :32], z, w[..., 32:], z], axis=-1)


def _prep_w_in(w_in):
    w_in_b = w_in.astype(BF16)
    kpe = _spread_rope_cols(w_in_b[..., Z_MAIN:Z_MAIN + QK_ROPE_DIM])
    tail = jnp.concatenate([w_in_b[..., Z_MAIN + QK_ROPE_DIM:], kpe], axis=-1)
    return w_in_b, tail


def _prep_w_uq(w_uq):
    l, r, _ = w_uq.shape
    w = w_uq.reshape(l, r, N_MLA_HEADS, QK_HEAD_DIM)
    w = jnp.concatenate([w[..., :QK_NOPE_DIM], _spread_rope_cols(w[..., QK_NOPE_DIM:])], axis=-1)
    return w.reshape(l, r, N_MLA_HEADS * QK_PAD).astype(BF16)


def _prep_w_ukv(w_ukv):
    l, r, _ = w_ukv.shape
    w = w_ukv.reshape(l, r, N_MLA_HEADS, 2 * HEAD_DIM)
    k_nope = w[..., :HEAD_DIM].reshape(l, r, MLA_WIDTH)
    v = w[..., HEAD_DIM:].reshape(l, r, MLA_WIDTH)
    return jnp.concatenate([k_nope, v], axis=-1).astype(BF16)


def _tiles(s):
    pick = lambda pref: pref if s % pref == 0 else s
    attn = pick(512)
    return dict(in_tm=pick(1024), ret_t=pick(512), ret_chunk=128, mla_tm=attn, attn_tq=attn,
                gmlp_t=pick(512), out_tm=pick(512), ffn_tm=pick(1024), ffn_tf=512)


def kernel(x, positions, attn_norm, w_in, mla_q_norm, w_uq, mla_kv_norm, w_ukv, gmlp_v_norm, gmlp_w_s,
           gmlp_b_s, mix_norm, w_out, ffn_norm, w_gate, w_up, w_down, final_norm):
    b, s, d = x.shape
    assert b == 1, "batch is folded away; only BATCH == 1 is supported"
    depth = w_in.shape[0]
    t = _tiles(s)

    w_in_b, w_in_tail = _prep_w_in(w_in)
    w_uq_b = _prep_w_uq(w_uq)
    w_ukv_b = _prep_w_ukv(w_ukv)
    w_out_b = w_out.astype(BF16)
    w_gate_b = w_gate.astype(BF16)
    w_up_b = w_up.astype(BF16)
    w_down_b = w_down.astype(BF16)
    rows = lambda a: a.reshape(depth, 1, -1)
    attn_g, q_g, kv_g, gv_g, mix_g, ffn_g = map(
        rows, (attn_norm, mla_q_norm, mla_kv_norm, gmlp_v_norm, mix_norm, ffn_norm))
    b_s = gmlp_b_s[..., None]
    fn_g = final_norm.reshape(1, d)

    cos_r, sin_r, cos_m, sin_m = _rope_tables(positions)
    ret_tables = _retention_tables(t["ret_chunk"])
    xs = x.reshape(s, d)
    for l in range(depth):
        z, zt = _in_proj(xs, attn_g, w_in_b, w_in_tail, l, t["in_tm"])
        y_ret = _retention(z, cos_r, sin_r, ret_tables, mix_g, l, t["ret_t"], t["ret_chunk"])
        qt, k, vt = _mla_proj(z, zt, q_g, kv_g, w_uq_b, w_ukv_b, cos_m, sin_m, l, t["mla_tm"])
        y_mla = _attention(qt, k, vt, mix_g, l, t["attn_tq"])
        y_gm = _gmlp(zt, gv_g, gmlp_w_s, b_s, mix_g, l, t["gmlp_t"])
        xs = _out_proj(xs, y_ret, y_mla, y_gm, w_out_b, l, t["out_tm"])
        xs = _ffn(xs, ffn_g, w_gate_b, w_up_b, w_down_b, fn_g, l, t["ffn_tm"], t["ffn_tf"],
                  final_norm=(l == depth - 1))
    return xs.reshape(b, s, d)
```

```python
import functools

import jax
import jax.numpy as jnp
from jax import lax
from jax.experimental import pallas as pl
from jax.experimental.pallas import tpu as pltpu

F32 = jnp.float32
BF16 = jnp.bfloat16

HEAD_DIM = 128
N_RET_HEADS = 6
N_MLA_HEADS = 6
N_GMLP_GROUPS = 4
RET_WIDTH = N_RET_HEADS * HEAD_DIM
MLA_WIDTH = N_MLA_HEADS * HEAD_DIM
GMLP_WIDTH = N_GMLP_GROUPS * HEAD_DIM
Q_LORA_RANK = 512
KV_LORA_RANK = 512
QK_NOPE_DIM = 128
QK_ROPE_DIM = 64
QK_HEAD_DIM = QK_NOPE_DIM + QK_ROPE_DIM
CHUNK = 128
ROPE_BASE = 10000.0
EPS = 1e-6

LANES = 128
QK_PAD = 2 * LANES
ATTN_HEADS_PER_STEP = 2
VT_ROWS = HEAD_DIM + 16
LOG2_E = 1.4426950408889634

Z_CQ = 4 * RET_WIDTH
Z_CKV = Z_CQ + Q_LORA_RANK
Z_MAIN = Z_CKV + KV_LORA_RANK
Z_TN = 1024
ZT_GU = 0
ZT_GV = ZT_GU + GMLP_WIDTH
ZT_KPE = ZT_GV + GMLP_WIDTH
Z_TAIL = ZT_KPE + LANES

VMEM_LIMIT = 56 * 1024 * 1024
VMEM_LIMIT_FFN_CAST = 60 * 1024 * 1024


def _cparams(*sem, vmem_limit=VMEM_LIMIT):
    return pltpu.CompilerParams(dimension_semantics=sem, vmem_limit_bytes=vmem_limit)


def _layer_spec(arr, l, ngrid):
    zeros = (0,) * (arr.ndim - 1)
    return pl.BlockSpec((None,) + arr.shape[1:], lambda *_: (l,) + zeros)


def _rms(x, gain):
    return x * lax.rsqrt(jnp.mean(x * x, axis=-1, keepdims=True) + EPS) * gain


def _rope(x, cos, sin):
    return x * cos + pltpu.roll(x, 64, 1) * sin


def _rope_table_kernel(pos_ref, inv_r_ref, sgn_r_ref, inv_m_ref, cm_ref, sm_ref,
                       cos_r_ref, sin_r_ref, cos_m_ref, sin_m_ref):
    pos = pos_ref[...].astype(F32)
    ang_r = pos * inv_r_ref[...]
    cos_r_ref[...] = jnp.cos(ang_r)
    sin_r_ref[...] = jnp.sin(ang_r) * sgn_r_ref[...]
    ang_m = pos * inv_m_ref[...]
    cos_m_ref[...] = jnp.cos(ang_m) * cm_ref[...]
    sin_m_ref[...] = jnp.sin(ang_m) * sm_ref[...]


def _rope_tables(positions):
    s = positions.shape[-1]
    ts = min(s, 1024)
    pos = positions.reshape(s, 1)
    inv_r = 1.0 / (ROPE_BASE ** (jnp.arange(0, HEAD_DIM, 2, dtype=F32) / HEAD_DIM))
    inv_m = 1.0 / (ROPE_BASE ** (jnp.arange(0, QK_ROPE_DIM, 2, dtype=F32) / QK_ROPE_DIM))
    z32 = jnp.zeros((32,), F32)
    o32 = jnp.ones((32,), F32)
    inv_r_full = jnp.concatenate([inv_r, inv_r]).reshape(1, LANES)
    sgn_r = jnp.concatenate([-jnp.ones((64,), F32), jnp.ones((64,), F32)]).reshape(1, LANES)
    inv_m_full = jnp.concatenate([inv_m, z32, inv_m, z32]).reshape(1, LANES)
    cmask = jnp.concatenate([o32, z32, o32, z32]).reshape(1, LANES)
    smask = jnp.concatenate([-o32, z32, o32, z32]).reshape(1, LANES)
    row = pl.BlockSpec((1, LANES), lambda i: (0, 0))
    tab = pl.BlockSpec((ts, LANES), lambda i: (i, 0))
    out = jax.ShapeDtypeStruct((s, LANES), F32)
    return pl.pallas_call(
        _rope_table_kernel,
        grid=(s // ts,),
        in_specs=[pl.BlockSpec((ts, 1), lambda i: (i, 0)), row, row, row, row, row],
        out_specs=[tab, tab, tab, tab],
        out_shape=[out, out, out, out],
        compiler_params=_cparams("arbitrary"),
        name="rope_tables",
    )(pos, inv_r_full, sgn_r, inv_m_full, cmask, smask)


def _in_proj_kernel(x_ref, g_ref, wm_ref, wt_ref, zm_ref, zt_ref, h_sc, *, n_main):
    j = pl.program_id(1)

    @pl.when(j == 0)
    def _():
        h_sc[...] = _rms(x_ref[...], g_ref[...]).astype(BF16)

    @pl.when(j < n_main)
    def _():
        zm_ref[...] = jnp.dot(h_sc[...], wm_ref[...], preferred_element_type=F32).astype(zm_ref.dtype)

    @pl.when(j == n_main)
    def _():
        zt_ref[...] = jnp.dot(h_sc[...], wt_ref[...], preferred_element_type=F32).astype(zt_ref.dtype)


def _in_proj(x, gain, w_in_b, w_tail, l, tm):
    s, d = x.shape
    n_main = Z_MAIN // Z_TN
    last = n_main - 1
    return pl.pallas_call(
        functools.partial(_in_proj_kernel, n_main=n_main),
        grid=(s // tm, n_main + 1),
        in_specs=[pl.BlockSpec((tm, d), lambda i, j: (i, 0)),
                  _layer_spec(gain, l, 2),
                  pl.BlockSpec((None, d, Z_TN), lambda i, j: (l, 0, jnp.minimum(j, last))),
                  _layer_spec(w_tail, l, 2)],
        out_specs=[pl.BlockSpec((tm, Z_TN), lambda i, j: (i, jnp.minimum(j, last))),
                   pl.BlockSpec((tm, Z_TAIL), lambda i, j: (i, 0))],
        out_shape=[jax.ShapeDtypeStruct((s, Z_MAIN), BF16),
                   jax.ShapeDtypeStruct((s, Z_TAIL), BF16)],
        scratch_shapes=[pltpu.VMEM((tm, d), BF16)],
        compiler_params=_cparams("arbitrary", "arbitrary"),
        name="in_proj",
    )(x, gain, w_in_b, w_tail)


def _retention_kernel(q_ref, k_ref, v_ref, g_ref, cos_ref, sin_ref, inner_ref, qdec_ref, kdec_ref,
                      cdec_ref, gain_ref, o_ref, state_sc, *, chunk, n_chunks):
    @pl.when(pl.program_id(0) == 0)
    def _():
        state_sc[...] = jnp.zeros_like(state_sc)

    for c in range(n_chunks):
        rows = slice(c * chunk, (c + 1) * chunk)
        cos = cos_ref[rows, :]
        sin = sin_ref[rows, :]
        for h in range(N_RET_HEADS):
            cols = slice(h * HEAD_DIM, (h + 1) * HEAD_DIM)
            q = _rope(q_ref[rows, cols].astype(F32), cos, sin)
            k = _rope(k_ref[rows, cols].astype(F32), cos, sin) * (HEAD_DIM ** -0.5)
            v = v_ref[rows, cols]
            state = state_sc[h]
            scores = lax.dot_general(q.astype(BF16), k.astype(BF16), (((1,), (1,)), ((), ())),
                                     preferred_element_type=F32) * inner_ref[h]
            out = jnp.dot(scores.astype(BF16), v, preferred_element_type=F32)
            out += jnp.dot((q * qdec_ref[:, cols]).astype(BF16), state.astype(BF16),
                           preferred_element_type=F32)
            kt = (k * kdec_ref[:, cols]).T.astype(BF16)
            state_sc[h] = cdec_ref[:, cols] * state + jnp.dot(kt, v, preferred_element_type=F32)
            y = _rms(out, gain_ref[:, cols])
            gate = g_ref[rows, cols].astype(F32)
            o_ref[rows, cols] = (y * (gate / (1.0 + jnp.exp(-gate)))).astype(o_ref.dtype)


def _retention_tables(chunk):
    h = N_RET_HEADS
    log_gamma = jnp.log1p(-jnp.exp2(-5.0 - jnp.arange(h, dtype=F32)))
    idx = jnp.arange(chunk, dtype=F32)
    rel = idx[:, None] - idx[None, :]
    inner = jnp.where(rel >= 0, jnp.exp(log_gamma[:, None, None] * jnp.maximum(rel, 0.0)), 0.0)
    qdec = jnp.exp(log_gamma[None, :] * (idx[:, None] + 1.0))
    kdec = jnp.exp(log_gamma[None, :] * (chunk - 1.0 - idx[:, None]))
    cdec = jnp.exp(log_gamma * chunk)
    rep = lambda a: jnp.repeat(a, HEAD_DIM, axis=-1)
    return inner, rep(qdec), rep(kdec), rep(cdec[None, :])


def _retention(z, cos_r, sin_r, tables, mix, l, t, chunk):
    s = z.shape[0]
    zspec = lambda c: pl.BlockSpec((t, RET_WIDTH), lambda i: (i, c))
    tab = pl.BlockSpec((t, LANES), lambda i: (i, 0))
    full = lambda a: pl.BlockSpec(a.shape, lambda i: (0,) * a.ndim)
    return pl.pallas_call(
        functools.partial(_retention_kernel, chunk=chunk, n_chunks=t // chunk),
        grid=(s // t,),
        in_specs=[zspec(0), zspec(1), zspec(2), zspec(3), tab, tab] + [full(a) for a in tables]
        + [pl.BlockSpec((None, 1, RET_WIDTH), lambda i: (l, 0, 0))],
        out_specs=pl.BlockSpec((t, RET_WIDTH), lambda i: (i, 0)),
        out_shape=jax.ShapeDtypeStruct((s, RET_WIDTH), BF16),
        scratch_shapes=[pltpu.VMEM((N_RET_HEADS, HEAD_DIM, HEAD_DIM), F32)],
        compiler_params=_cparams("arbitrary"),
        name="retention",
    )(z, z, z, z, cos_r, sin_r, *tables, mix)


def _mla_proj_kernel(cq_ref, ckv_ref, kpe_ref, gq_ref, gkv_ref, wuq_ref, wukv_ref, cos_ref, sin_ref,
                     qt_ref, k_ref, vt_ref):
    cos = cos_ref[...]
    sin = sin_ref[...]
    scale = QK_HEAD_DIM ** -0.5 * LOG2_E
    cq = _rms(cq_ref[...].astype(F32), gq_ref[...]).astype(BF16)
    mq = jnp.dot(cq, wuq_ref[...], preferred_element_type=F32)
    ckv = _rms(ckv_ref[...].astype(F32), gkv_ref[...]).astype(BF16)
    mkv = jnp.dot(ckv, wukv_ref[...], preferred_element_type=F32)
    kpe = _rope(kpe_ref[...].astype(F32), cos, sin).astype(BF16)
    for h in range(N_MLA_HEADS):
        nope = slice(h * QK_PAD, h * QK_PAD + LANES)
        pe = slice(h * QK_PAD + LANES, (h + 1) * QK_PAD)
        qt_ref[nope, :] = (mq[:, nope] * scale).T.astype(BF16)
        qt_ref[pe, :] = (_rope(mq[:, pe], cos, sin) * scale).T.astype(BF16)
        k_ref[:, nope] = mkv[:, h * LANES:(h + 1) * LANES].astype(BF16)
        k_ref[:, pe] = kpe
    vt = mkv[:, MLA_WIDTH:].T.astype(BF16)
    tm = vt.shape[1]
    extra = lax.broadcasted_iota(jnp.int32, (VT_ROWS - HEAD_DIM, tm), 0)
    ones_row = jnp.where(extra == 0, 1.0, 0.0).astype(BF16)
    for h in range(N_MLA_HEADS):
        vt_ref[h * VT_ROWS:h * VT_ROWS + HEAD_DIM, :] = vt[h * HEAD_DIM:(h + 1) * HEAD_DIM, :]
        vt_ref[h * VT_ROWS + HEAD_DIM:(h + 1) * VT_ROWS, :] = ones_row


def _mla_proj(z, zt, gq, gkv, wuq, wukv, cos_m, sin_m, l, tm):
    s = z.shape[0]
    tab = pl.BlockSpec((tm, LANES), lambda i: (i, 0))
    qk = N_MLA_HEADS * QK_PAD
    return pl.pallas_call(
        _mla_proj_kernel,
        grid=(s // tm,),
        in_specs=[pl.BlockSpec((tm, Q_LORA_RANK), lambda i: (i, Z_CQ // Q_LORA_RANK)),
                  pl.BlockSpec((tm, KV_LORA_RANK), lambda i: (i, Z_CKV // KV_LORA_RANK)),
                  pl.BlockSpec((tm, LANES), lambda i: (i, ZT_KPE // LANES)),
                  _layer_spec(gq, l, 1), _layer_spec(gkv, l, 1),
                  _layer_spec(wuq, l, 1), _layer_spec(wukv, l, 1), tab, tab],
        out_specs=[pl.BlockSpec((qk, tm), lambda i: (0, i)),
                   pl.BlockSpec((tm, qk), lambda i: (i, 0)),
                   pl.BlockSpec((None, N_MLA_HEADS * VT_ROWS, tm), lambda i: (i, 0, 0))],
        out_shape=[jax.ShapeDtypeStruct((qk, s), BF16),
                   jax.ShapeDtypeStruct((s, qk), BF16),
                   jax.ShapeDtypeStruct((s // tm, N_MLA_HEADS * VT_ROWS, tm), BF16)],
        compiler_params=_cparams("arbitrary"),
        name="mla_proj",
    )(z, z, zt, gq, gkv, wuq, wukv, cos_m, sin_m)


def _attn_kernel(qt_ref, k_ref, vt_ref, gain_ref, o_ref, m_sc, acc_sc, st0_sc, st1_sc, *, tq, hp):
    qi = pl.program_id(1)
    m_sc[...] = jnp.full_like(m_sc, -jnp.inf)
    acc_sc[...] = jnp.zeros_like(acc_sc)

    def scores(kb, st_sc):
        start = pl.multiple_of(kb * tq, tq)
        for j in range(hp):
            qk_cols = slice(j * QK_PAD, (j + 1) * QK_PAD)
            st_sc[j] = jnp.dot(k_ref[pl.ds(start, tq), qk_cols], qt_ref[qk_cols, :],
                               preferred_element_type=F32)

    def softmax_pv(kb, st_sc, masked):
        for j in range(hp):
            vt = vt_ref[kb, j * VT_ROWS:(j + 1) * VT_ROWS, :]
            st = st_sc[j]
            if masked:
                kv_pos = lax.broadcasted_iota(jnp.int32, st.shape, 0)
                q_pos = lax.broadcasted_iota(jnp.int32, st.shape, 1)
                st = jnp.where(kv_pos <= q_pos, st, -jnp.inf)
            m_prev = m_sc[j]
            m_new = jnp.maximum(m_prev, jnp.max(st, axis=0, keepdims=True))
            alpha = jnp.exp2(m_prev - m_new)
            p = jnp.exp2(st - m_new).astype(BF16)
            acc_sc[j] = alpha * acc_sc[j] + jnp.dot(vt, p, preferred_element_type=F32)
            m_sc[j] = m_new

    scores(0, st0_sc)

    def pair(i, carry):
        kb = 2 * i
        scores(kb + 1, st1_sc)
        softmax_pv(kb, st0_sc, False)
        scores(kb + 2, st0_sc)
        softmax_pv(kb + 1, st1_sc, False)
        return carry

    lax.fori_loop(0, qi // 2, pair, 0)

    @pl.when(qi % 2 == 0)
    def _():
        softmax_pv(qi, st0_sc, True)

    @pl.when(qi % 2 == 1)
    def _():
        scores(qi, st1_sc)
        softmax_pv(qi - 1, st0_sc, False)
        softmax_pv(qi, st1_sc, True)

    for j in range(hp):
        cols = slice(j * HEAD_DIM, (j + 1) * HEAD_DIM)
        denom = acc_sc[j, HEAD_DIM:HEAD_DIM + 1, :]
        o = (acc_sc[j, :HEAD_DIM, :] * (1.0 / denom)).T
        o_ref[:, cols] = _rms(o, gain_ref[:, cols]).astype(o_ref.dtype)


def _attention(qt, k, vt, mix, l, tq):
    s = k.shape[0]
    hp = ATTN_HEADS_PER_STEP
    width = hp * HEAD_DIM
    return pl.pallas_call(
        functools.partial(_attn_kernel, tq=tq, hp=hp),
        grid=(N_MLA_HEADS // hp, s // tq),
        in_specs=[pl.BlockSpec((hp * QK_PAD, tq), lambda h, i: (h, i)),
                  pl.BlockSpec((s, hp * QK_PAD), lambda h, i: (0, h)),
                  pl.BlockSpec((s // tq, hp * VT_ROWS, tq), lambda h, i: (0, h, 0)),
                  pl.BlockSpec((None, 1, width), lambda h, i: (l, 0, RET_WIDTH // width + h))],
        out_specs=pl.BlockSpec((tq, width), lambda h, i: (i, h)),
        out_shape=jax.ShapeDtypeStruct((s, MLA_WIDTH), BF16),
        scratch_shapes=[pltpu.VMEM((hp, 1, tq), F32),
                        pltpu.VMEM((hp, VT_ROWS, tq), F32),
                        pltpu.VMEM((hp, tq, tq), F32), pltpu.VMEM((hp, tq, tq), F32)],
        compiler_params=_cparams("arbitrary", "arbitrary"),
        name="mla_attention",
    )(qt, k, vt, mix)


def _gmlp_kernel(u_ref, v_ref, gv_ref, ws_ref, bs_ref, gm_ref, o_ref, *, n_chunks):
    row = lax.broadcasted_iota(jnp.int32, (CHUNK, CHUNK), 0)
    col = lax.broadcasted_iota(jnp.int32, (CHUNK, CHUNK), 1)
    for g in range(N_GMLP_GROUPS):
        cols = slice(g * HEAD_DIM, (g + 1) * HEAD_DIM)
        w = jnp.where(col <= row, ws_ref[g], 0.0).astype(BF16)
        bias = bs_ref[g]
        for c in range(n_chunks):
            rows = slice(c * CHUNK, (c + 1) * CHUNK)
            u = jax.nn.gelu(u_ref[rows, cols].astype(F32))
            v = jax.nn.gelu(v_ref[rows, cols].astype(F32))
            vn = _rms(v, gv_ref[:, cols]).astype(BF16)
            sg = jnp.dot(w, vn, preferred_element_type=F32) + bias
            o_ref[rows, cols] = _rms(u * sg, gm_ref[:, cols]).astype(o_ref.dtype)


def _gmlp(zt, gv, ws, bs, mix, l, t):
    s = zt.shape[0]
    return pl.pallas_call(
        functools.partial(_gmlp_kernel, n_chunks=t // CHUNK),
        grid=(s // t,),
        in_specs=[pl.BlockSpec((t, GMLP_WIDTH), lambda i: (i, ZT_GU // GMLP_WIDTH)),
                  pl.BlockSpec((t, GMLP_WIDTH), lambda i: (i, ZT_GV // GMLP_WIDTH)),
                  _layer_spec(gv, l, 1), _layer_spec(ws, l, 1), _layer_spec(bs, l, 1),
                  pl.BlockSpec((None, 1, GMLP_WIDTH),
                               lambda i: (l, 0, (RET_WIDTH + MLA_WIDTH) // GMLP_WIDTH))],
        out_specs=pl.BlockSpec((t, GMLP_WIDTH), lambda i: (i, 0)),
        out_shape=jax.ShapeDtypeStruct((s, GMLP_WIDTH), BF16),
        compiler_params=_cparams("arbitrary"),
        name="gmlp",
    )(zt, zt, gv, ws, bs, mix)


def _out_proj_kernel(x_ref, yr_ref, ym_ref, yg_ref, w_ref, o_ref):
    y = jnp.concatenate([yr_ref[...], ym_ref[...], yg_ref[...]], axis=-1)
    o_ref[...] = x_ref[...] + jnp.dot(y, w_ref[...], preferred_element_type=F32)


def _out_proj(x, yr, ym, yg, w, l, tm):
    s, d = x.shape
    blk = lambda n: pl.BlockSpec((tm, n), lambda i: (i, 0))
    return pl.pallas_call(
        _out_proj_kernel,
        grid=(s // tm,),
        in_specs=[blk(d), blk(RET_WIDTH), blk(MLA_WIDTH), blk(GMLP_WIDTH), _layer_spec(w, l, 1)],
        out_specs=blk(d),
        out_shape=jax.ShapeDtypeStruct((s, d), F32),
        compiler_params=_cparams("arbitrary"),
        name="out_proj",
    )(x, yr, ym, yg, w)


def _ffn_kernel(x_ref, g_ref, wg_ref, wu_ref, wd_ref, fn_ref, o_ref, h_sc, *, final_norm):
    f = pl.program_id(1)

    @pl.when(f == 0)
    def _():
        x = x_ref[...]
        h_sc[...] = _rms(x, g_ref[...]).astype(BF16)
        o_ref[...] = x

    h = h_sc[...]
    gate = jnp.dot(h, wg_ref[...], preferred_element_type=F32)
    up = jnp.dot(h, wu_ref[...], preferred_element_type=F32)
    act = (gate / (1.0 + jnp.exp(-gate)) * up).astype(BF16)
    o_ref[...] += jnp.dot(act, wd_ref[...], preferred_element_type=F32)

    if final_norm:
        @pl.when(f == pl.num_programs(1) - 1)
        def _():
            o_ref[...] = _rms(o_ref[...], fn_ref[...])


def _ffn_cast_kernel(x_ref, g_ref, wg_ref, wu_ref, wd_ref, fn_ref, ng_ref, nu_ref, nd_ref,
                     o_ref, cg_ref, cu_ref, cd_ref, h_sc):
    cg_ref[...] = ng_ref[...].astype(BF16)
    cu_ref[...] = nu_ref[...].astype(BF16)
    cd_ref[...] = nd_ref[...].astype(BF16)
    _ffn_kernel(x_ref, g_ref, wg_ref, wu_ref, wd_ref, fn_ref, o_ref, h_sc, final_norm=False)


def _ffn(x, gain, wg, wu, wd, fn_gain, l, tm, tf, final_norm, next_f32=None):
    s, d = x.shape
    dff = wg.shape[1]
    ni, nf = s // tm, dff // tf
    in_specs = [pl.BlockSpec((tm, d), lambda i, f: (i, 0)),
                _layer_spec(gain, l, 2),
                pl.BlockSpec((d, tf), lambda i, f: (0, f)),
                pl.BlockSpec((d, tf), lambda i, f: (0, f)),
                pl.BlockSpec((tf, d), lambda i, f: (f, 0)),
                pl.BlockSpec((1, d), lambda i, f: (0, 0))]
    x_spec = pl.BlockSpec((tm, d), lambda i, f: (i, 0))
    x_shape = jax.ShapeDtypeStruct((s, d), F32)
    scratch = [pltpu.VMEM((tm, d), BF16)]
    params = _cparams("arbitrary", "arbitrary")
    if next_f32 is None:
        return pl.pallas_call(
            functools.partial(_ffn_kernel, final_norm=final_norm),
            grid=(ni, nf), in_specs=in_specs, out_specs=x_spec, out_shape=x_shape,
            scratch_shapes=scratch, compiler_params=params, name="ffn",
        )(x, gain, wg, wu, wd, fn_gain)
    assert not final_norm and d % ni == 0 and dff % nf == 0
    rb, cb = d // ni, dff // nf
    in_specs += [pl.BlockSpec((None, rb, cb), lambda i, f: (l + 1, i, f)),
                 pl.BlockSpec((None, rb, cb), lambda i, f: (l + 1, i, f)),
                 pl.BlockSpec((None, cb, rb), lambda i, f: (l + 1, f, i))]
    out_specs = [x_spec,
                 pl.BlockSpec((rb, cb), lambda i, f: (i, f)),
                 pl.BlockSpec((rb, cb), lambda i, f: (i, f)),
                 pl.BlockSpec((cb, rb), lambda i, f: (f, i))]
    out_shape = [x_shape, jax.ShapeDtypeStruct((d, dff), BF16), jax.ShapeDtypeStruct((d, dff), BF16),
                 jax.ShapeDtypeStruct((dff, d), BF16)]
    return pl.pallas_call(
        _ffn_cast_kernel,
        grid=(ni, nf), in_specs=in_specs, out_specs=out_specs, out_shape=out_shape,
        scratch_shapes=scratch, name="ffn_cast",
        compiler_params=_cparams("arbitrary", "arbitrary", vmem_limit=VMEM_LIMIT_FFN_CAST),
    )(x, gain, wg, wu, wd, fn_gain, *next_f32)


def _spread_rope_cols(w):
    z = jnp.zeros(w.shape[:-1] + (32,), w.dtype)
    return jnp.concatenate([w[..., :32], z, w[..., 32:], z], axis=-1)


def _prep_w_in(w_in):
    w_in_b = w_in.astype(BF16)
    kpe = _spread_rope_cols(w_in_b[..., Z_MAIN:Z_MAIN + QK_ROPE_DIM])
    tail = jnp.concatenate([w_in_b[..., Z_MAIN + QK_ROPE_DIM:], kpe], axis=-1)
    return w_in_b, tail


def _prep_w_uq(w_uq):
    l, r, _ = w_uq.shape
    w = w_uq.reshape(l, r, N_MLA_HEADS, QK_HEAD_DIM)
    w = jnp.concatenate([w[..., :QK_NOPE_DIM], _spread_rope_cols(w[..., QK_NOPE_DIM:])], axis=-1)
    return w.reshape(l, r, N_MLA_HEADS * QK_PAD).astype(BF16)


def _prep_w_ukv(w_ukv):
    l, r, _ = w_ukv.shape
    w = w_ukv.reshape(l, r, N_MLA_HEADS, 2 * HEAD_DIM)
    k_nope = w[..., :HEAD_DIM].reshape(l, r, MLA_WIDTH)
    v = w[..., HEAD_DIM:].reshape(l, r, MLA_WIDTH)
    return jnp.concatenate([k_nope, v], axis=-1).astype(BF16)


def _tiles(s):
    pick = lambda pref: pref if s % pref == 0 else s
    attn = pick(512)
    return dict(in_tm=pick(1024), ret_t=pick(512), ret_chunk=128, mla_tm=attn, attn_tq=attn,
                gmlp_t=pick(512), out_tm=pick(512), ffn_tm=pick(1024), ffn_tf=512)


def kernel(x, positions, attn_norm, w_in, mla_q_norm, w_uq, mla_kv_norm, w_ukv, gmlp_v_norm, gmlp_w_s,
           gmlp_b_s, mix_norm, w_out, ffn_norm, w_gate, w_up, w_down, final_norm):
    b, s, d = x.shape
    assert b == 1, "batch is folded away; only BATCH == 1 is supported"
    depth = w_in.shape[0]
    t = _tiles(s)

    w_in_b, w_in_tail = _prep_w_in(w_in)
    w_uq_b = _prep_w_uq(w_uq)
    w_ukv_b = _prep_w_ukv(w_ukv)
    w_out_b = w_out.astype(BF16)
    ffn_w = tuple(w[0].astype(BF16) for w in (w_gate, w_up, w_down))
    rows = lambda a: a.reshape(depth, 1, -1)
    attn_g, q_g, kv_g, gv_g, mix_g, ffn_g = map(
        rows, (attn_norm, mla_q_norm, mla_kv_norm, gmlp_v_norm, mix_norm, ffn_norm))
    b_s = gmlp_b_s[..., None]
    fn_g = final_norm.reshape(1, d)

    cos_r, sin_r, cos_m, sin_m = _rope_tables(positions)
    ret_tables = _retention_tables(t["ret_chunk"])
    xs = x.reshape(s, d)
    for l in range(depth):
        z, zt = _in_proj(xs, attn_g, w_in_b, w_in_tail, l, t["in_tm"])
        y_ret = _retention(z, cos_r, sin_r, ret_tables, mix_g, l, t["ret_t"], t["ret_chunk"])
        qt, k, vt = _mla_proj(z, zt, q_g, kv_g, w_uq_b, w_ukv_b, cos_m, sin_m, l, t["mla_tm"])
        y_mla = _attention(qt, k, vt, mix_g, l, t["attn_tq"])
        y_gm = _gmlp(zt, gv_g, gmlp_w_s, b_s, mix_g, l, t["gmlp_t"])
        xs = _out_proj(xs, y_ret, y_mla, y_gm, w_out_b, l, t["out_tm"])
        if l < depth - 1:
            xs, *ffn_w = _ffn(xs, ffn_g, *ffn_w, fn_g, l, t["ffn_tm"], t["ffn_tf"], final_norm=False,
                              next_f32=(w_gate, w_up, w_down))
        else:
            xs = _ffn(xs, ffn_g, *ffn_w, fn_g, l, t["ffn_tm"], t["ffn_tf"], final_norm=True)
    return xs.reshape(b, s, d)
```

```python
import functools

import jax
import jax.numpy as jnp
from jax import lax
from jax.experimental import pallas as pl
from jax.experimental.pallas import tpu as pltpu

F32 = jnp.float32
BF16 = jnp.bfloat16

HEAD_DIM = 128
N_RET_HEADS = 6
N_MLA_HEADS = 6
N_GMLP_GROUPS = 4
RET_WIDTH = N_RET_HEADS * HEAD_DIM
MLA_WIDTH = N_MLA_HEADS * HEAD_DIM
GMLP_WIDTH = N_GMLP_GROUPS * HEAD_DIM
Q_LORA_RANK = 512
KV_LORA_RANK = 512
QK_NOPE_DIM = 128
QK_ROPE_DIM = 64
QK_HEAD_DIM = QK_NOPE_DIM + QK_ROPE_DIM
CHUNK = 128
ROPE_BASE = 10000.0
EPS = 1e-6

LANES = 128
QK_PAD = 2 * LANES
ATTN_HEADS_PER_STEP = 3
VT_ROWS = HEAD_DIM + 16
LOG2_E = 1.4426950408889634

Z_CQ = 4 * RET_WIDTH
Z_CKV = Z_CQ + Q_LORA_RANK
Z_MAIN = Z_CKV + KV_LORA_RANK
Z_TN = 1024
ZT_GU = 0
ZT_GV = ZT_GU + GMLP_WIDTH
ZT_KPE = ZT_GV + GMLP_WIDTH
Z_TAIL = ZT_KPE + LANES

VMEM_LIMIT = 56 * 1024 * 1024
VMEM_LIMIT_FFN_CAST = 60 * 1024 * 1024


def _cparams(*sem, vmem_limit=VMEM_LIMIT):
    return pltpu.CompilerParams(dimension_semantics=sem, vmem_limit_bytes=vmem_limit)


def _layer_spec(arr, l, ngrid):
    zeros = (0,) * (arr.ndim - 1)
    return pl.BlockSpec((None,) + arr.shape[1:], lambda *_: (l,) + zeros)


def _rms(x, gain):
    return x * lax.rsqrt(jnp.mean(x * x, axis=-1, keepdims=True) + EPS) * gain


def _rope(x, cos, sin):
    return x * cos + pltpu.roll(x, 64, 1) * sin


def _rope_table_kernel(pos_ref, inv_r_ref, sgn_r_ref, inv_m_ref, cm_ref, sm_ref,
                       cos_r_ref, sin_r_ref, cos_m_ref, sin_m_ref):
    pos = pos_ref[...].astype(F32)
    ang_r = pos * inv_r_ref[...]
    cos_r_ref[...] = jnp.cos(ang_r)
    sin_r_ref[...] = jnp.sin(ang_r) * sgn_r_ref[...]
    ang_m = pos * inv_m_ref[...]
    cos_m_ref[...] = jnp.cos(ang_m) * cm_ref[...]
    sin_m_ref[...] = jnp.sin(ang_m) * sm_ref[...]


def _rope_tables(positions):
    s = positions.shape[-1]
    ts = min(s, 1024)
    pos = positions.reshape(s, 1)
    inv_r = 1.0 / (ROPE_BASE ** (jnp.arange(0, HEAD_DIM, 2, dtype=F32) / HEAD_DIM))
    inv_m = 1.0 / (ROPE_BASE ** (jnp.arange(0, QK_ROPE_DIM, 2, dtype=F32) / QK_ROPE_DIM))
    z32 = jnp.zeros((32,), F32)
    o32 = jnp.ones((32,), F32)
    inv_r_full = jnp.concatenate([inv_r, inv_r]).reshape(1, LANES)
    sgn_r = jnp.concatenate([-jnp.ones((64,), F32), jnp.ones((64,), F32)]).reshape(1, LANES)
    inv_m_full = jnp.concatenate([inv_m, z32, inv_m, z32]).reshape(1, LANES)
    cmask = jnp.concatenate([o32, z32, o32, z32]).reshape(1, LANES)
    smask = jnp.concatenate([-o32, z32, o32, z32]).reshape(1, LANES)
    row = pl.BlockSpec((1, LANES), lambda i: (0, 0))
    tab = pl.BlockSpec((ts, LANES), lambda i: (i, 0))
    out = jax.ShapeDtypeStruct((s, LANES), F32)
    return pl.pallas_call(
        _rope_table_kernel,
        grid=(s // ts,),
        in_specs=[pl.BlockSpec((ts, 1), lambda i: (i, 0)), row, row, row, row, row],
        out_specs=[tab, tab, tab, tab],
        out_shape=[out, out, out, out],
        compiler_params=_cparams("arbitrary"),
        name="rope_tables",
    )(pos, inv_r_full, sgn_r, inv_m_full, cmask, smask)


def _in_proj_kernel(x_ref, g_ref, wm_ref, wt_ref, zm_ref, zt_ref, h_sc, *, n_main):
    j = pl.program_id(1)

    @pl.when(j == 0)
    def _():
        h_sc[...] = _rms(x_ref[...], g_ref[...]).astype(BF16)

    @pl.when(j < n_main)
    def _():
        zm_ref[...] = jnp.dot(h_sc[...], wm_ref[...], preferred_element_type=F32).astype(zm_ref.dtype)

    @pl.when(j == n_main)
    def _():
        zt_ref[...] = jnp.dot(h_sc[...], wt_ref[...], preferred_element_type=F32).astype(zt_ref.dtype)


def _in_proj(x, gain, w_in_b, w_tail, l, tm):
    s, d = x.shape
    n_main = Z_MAIN // Z_TN
    last = n_main - 1
    return pl.pallas_call(
        functools.partial(_in_proj_kernel, n_main=n_main),
        grid=(s // tm, n_main + 1),
        in_specs=[pl.BlockSpec((tm, d), lambda i, j: (i, 0)),
                  _layer_spec(gain, l, 2),
                  pl.BlockSpec((None, d, Z_TN), lambda i, j: (l, 0, jnp.minimum(j, last))),
                  _layer_spec(w_tail, l, 2)],
        out_specs=[pl.BlockSpec((tm, Z_TN), lambda i, j: (i, jnp.minimum(j, last))),
                   pl.BlockSpec((tm, Z_TAIL), lambda i, j: (i, 0))],
        out_shape=[jax.ShapeDtypeStruct((s, Z_MAIN), BF16),
                   jax.ShapeDtypeStruct((s, Z_TAIL), BF16)],
        scratch_shapes=[pltpu.VMEM((tm, d), BF16)],
        compiler_params=_cparams("arbitrary", "arbitrary"),
        name="in_proj",
    )(x, gain, w_in_b, w_tail)


def _retention_kernel(q_ref, k_ref, v_ref, g_ref, cos_ref, sin_ref, inner_ref, qdec_ref, kdec_ref,
                      cdec_ref, gain_ref, o_ref, state_sc, *, chunk, n_chunks):
    @pl.when(pl.program_id(0) == 0)
    def _():
        state_sc[...] = jnp.zeros_like(state_sc)

    for c in range(n_chunks):
        rows = slice(c * chunk, (c + 1) * chunk)
        cos = cos_ref[rows, :]
        sin = sin_ref[rows, :]
        for h in range(N_RET_HEADS):
            cols = slice(h * HEAD_DIM, (h + 1) * HEAD_DIM)
            q = _rope(q_ref[rows, cols].astype(F32), cos, sin)
            k = _rope(k_ref[rows, cols].astype(F32), cos, sin) * (HEAD_DIM ** -0.5)
            v = v_ref[rows, cols]
            state = state_sc[h]
            scores = lax.dot_general(q.astype(BF16), k.astype(BF16), (((1,), (1,)), ((), ())),
                                     preferred_element_type=F32) * inner_ref[h]
            out = jnp.dot(scores.astype(BF16), v, preferred_element_type=F32)
            out += jnp.dot((q * qdec_ref[:, cols]).astype(BF16), state.astype(BF16),
                           preferred_element_type=F32)
            kt = (k * kdec_ref[:, cols]).T.astype(BF16)
            state_sc[h] = cdec_ref[:, cols] * state + jnp.dot(kt, v, preferred_element_type=F32)
            y = _rms(out, gain_ref[:, cols])
            gate = g_ref[rows, cols].astype(F32)
            o_ref[rows, cols] = (y * (gate / (1.0 + jnp.exp(-gate)))).astype(o_ref.dtype)


def _retention_tables(chunk):
    h = N_RET_HEADS
    log_gamma = jnp.log1p(-jnp.exp2(-5.0 - jnp.arange(h, dtype=F32)))
    idx = jnp.arange(chunk, dtype=F32)
    rel = idx[:, None] - idx[None, :]
    inner = jnp.where(rel >= 0, jnp.exp(log_gamma[:, None, None] * jnp.maximum(rel, 0.0)), 0.0)
    qdec = jnp.exp(log_gamma[None, :] * (idx[:, None] + 1.0))
    kdec = jnp.exp(log_gamma[None, :] * (chunk - 1.0 - idx[:, None]))
    cdec = jnp.exp(log_gamma * chunk)
    rep = lambda a: jnp.repeat(a, HEAD_DIM, axis=-1)
    return inner, rep(qdec), rep(kdec), rep(cdec[None, :])


def _retention(z, cos_r, sin_r, tables, mix, l, t, chunk):
    s = z.shape[0]
    zspec = lambda c: pl.BlockSpec((t, RET_WIDTH), lambda i: (i, c))
    tab = pl.BlockSpec((t, LANES), lambda i: (i, 0))
    full = lambda a: pl.BlockSpec(a.shape, lambda i: (0,) * a.ndim)
    return pl.pallas_call(
        functools.partial(_retention_kernel, chunk=chunk, n_chunks=t // chunk),
        grid=(s // t,),
        in_specs=[zspec(0), zspec(1), zspec(2), zspec(3), tab, tab] + [full(a) for a in tables]
        + [pl.BlockSpec((None, 1, RET_WIDTH), lambda i: (l, 0, 0))],
        out_specs=pl.BlockSpec((t, RET_WIDTH), lambda i: (i, 0)),
        out_shape=jax.ShapeDtypeStruct((s, RET_WIDTH), BF16),
        scratch_shapes=[pltpu.VMEM((N_RET_HEADS, HEAD_DIM, HEAD_DIM), F32)],
        compiler_params=_cparams("arbitrary"),
        name="retention",
    )(z, z, z, z, cos_r, sin_r, *tables, mix)


def _mla_proj_kernel(cq_ref, ckv_ref, kpe_ref, gq_ref, gkv_ref, wuq_ref, wukv_ref, cos_ref, sin_ref,
                     qt_ref, k_ref, vt_ref):
    cos = cos_ref[...]
    sin = sin_ref[...]
    scale = QK_HEAD_DIM ** -0.5 * LOG2_E
    cq = _rms(cq_ref[...].astype(F32), gq_ref[...]).astype(BF16)
    mq = jnp.dot(cq, wuq_ref[...], preferred_element_type=F32)
    ckv = _rms(ckv_ref[...].astype(F32), gkv_ref[...]).astype(BF16)
    mkv = jnp.dot(ckv, wukv_ref[...], preferred_element_type=F32)
    kpe = _rope(kpe_ref[...].astype(F32), cos, sin).astype(BF16)
    for h in range(N_MLA_HEADS):
        nope = slice(h * QK_PAD, h * QK_PAD + LANES)
        pe = slice(h * QK_PAD + LANES, (h + 1) * QK_PAD)
        qt_ref[nope, :] = (mq[:, nope] * scale).T.astype(BF16)
        qt_ref[pe, :] = (_rope(mq[:, pe], cos, sin) * scale).T.astype(BF16)
        k_ref[:, nope] = mkv[:, h * LANES:(h + 1) * LANES].astype(BF16)
        k_ref[:, pe] = kpe
    vt = mkv[:, MLA_WIDTH:].T.astype(BF16)
    tm = vt.shape[1]
    extra = lax.broadcasted_iota(jnp.int32, (VT_ROWS - HEAD_DIM, tm), 0)
    ones_row = jnp.where(extra == 0, 1.0, 0.0).astype(BF16)
    for h in range(N_MLA_HEADS):
        vt_ref[h * VT_ROWS:h * VT_ROWS + HEAD_DIM, :] = vt[h * HEAD_DIM:(h + 1) * HEAD_DIM, :]
        vt_ref[h * VT_ROWS + HEAD_DIM:(h + 1) * VT_ROWS, :] = ones_row


def _mla_proj(z, zt, gq, gkv, wuq, wukv, cos_m, sin_m, l, tm):
    s = z.shape[0]
    tab = pl.BlockSpec((tm, LANES), lambda i: (i, 0))
    qk = N_MLA_HEADS * QK_PAD
    return pl.pallas_call(
        _mla_proj_kernel,
        grid=(s // tm,),
        in_specs=[pl.BlockSpec((tm, Q_LORA_RANK), lambda i: (i, Z_CQ // Q_LORA_RANK)),
                  pl.BlockSpec((tm, KV_LORA_RANK), lambda i: (i, Z_CKV // KV_LORA_RANK)),
                  pl.BlockSpec((tm, LANES), lambda i: (i, ZT_KPE // LANES)),
                  _layer_spec(gq, l, 1), _layer_spec(gkv, l, 1),
                  _layer_spec(wuq, l, 1), _layer_spec(wukv, l, 1), tab, tab],
        out_specs=[pl.BlockSpec((qk, tm), lambda i: (0, i)),
                   pl.BlockSpec((tm, qk), lambda i: (i, 0)),
                   pl.BlockSpec((None, N_MLA_HEADS * VT_ROWS, tm), lambda i: (i, 0, 0))],
        out_shape=[jax.ShapeDtypeStruct((qk, s), BF16),
                   jax.ShapeDtypeStruct((s, qk), BF16),
                   jax.ShapeDtypeStruct((s // tm, N_MLA_HEADS * VT_ROWS, tm), BF16)],
        compiler_params=_cparams("arbitrary"),
        name="mla_proj",
    )(z, z, zt, gq, gkv, wuq, wukv, cos_m, sin_m)


def _attn_kernel(qt_ref, k_ref, vt_ref, gain_ref, o_ref, m_sc, acc_sc, st0_sc, st1_sc, *, tq, hp):
    qi = pl.program_id(1)
    m_sc[...] = jnp.full_like(m_sc, -jnp.inf)
    acc_sc[...] = jnp.zeros_like(acc_sc)

    def scores(kb, st_sc):
        start = pl.multiple_of(kb * tq, tq)
        for j in range(hp):
            qk_cols = slice(j * QK_PAD, (j + 1) * QK_PAD)
            st_sc[j] = jnp.dot(k_ref[pl.ds(start, tq), qk_cols], qt_ref[qk_cols, :],
                               preferred_element_type=F32)

    def softmax_pv(kb, st_sc, masked):
        for j in range(hp):
            vt = vt_ref[kb, j * VT_ROWS:(j + 1) * VT_ROWS, :]
            st = st_sc[j]
            if masked:
                kv_pos = lax.broadcasted_iota(jnp.int32, st.shape, 0)
                q_pos = lax.broadcasted_iota(jnp.int32, st.shape, 1)
                st = jnp.where(kv_pos <= q_pos, st, -jnp.inf)
            m_prev = m_sc[j]
            m_new = jnp.maximum(m_prev, jnp.max(st, axis=0, keepdims=True))
            alpha = jnp.exp2(m_prev - m_new)
            p = jnp.exp2(st - m_new).astype(BF16)
            acc_sc[j] = alpha * acc_sc[j] + jnp.dot(vt, p, preferred_element_type=F32)
            m_sc[j] = m_new

    scores(0, st0_sc)

    def pair(i, carry):
        kb = 2 * i
        scores(kb + 1, st1_sc)
        softmax_pv(kb, st0_sc, False)
        scores(kb + 2, st0_sc)
        softmax_pv(kb + 1, st1_sc, False)
        return carry

    lax.fori_loop(0, qi // 2, pair, 0)

    @pl.when(qi % 2 == 0)
    def _():
        softmax_pv(qi, st0_sc, True)

    @pl.when(qi % 2 == 1)
    def _():
        scores(qi, st1_sc)
        softmax_pv(qi - 1, st0_sc, False)
        softmax_pv(qi, st1_sc, True)

    for j in range(hp):
        cols = slice(j * HEAD_DIM, (j + 1) * HEAD_DIM)
        denom = acc_sc[j, HEAD_DIM:HEAD_DIM + 1, :]
        o = (acc_sc[j, :HEAD_DIM, :] * (1.0 / denom)).T
        o_ref[:, cols] = _rms(o, gain_ref[:, cols]).astype(o_ref.dtype)


def _attention(qt, k, vt, mix, l, tq):
    s = k.shape[0]
    hp = ATTN_HEADS_PER_STEP
    width = hp * HEAD_DIM
    return pl.pallas_call(
        functools.partial(_attn_kernel, tq=tq, hp=hp),
        grid=(N_MLA_HEADS // hp, s // tq),
        in_specs=[pl.BlockSpec((hp * QK_PAD, tq), lambda h, i: (h, i)),
                  pl.BlockSpec((s, hp * QK_PAD), lambda h, i: (0, h)),
                  pl.BlockSpec((s // tq, hp * VT_ROWS, tq), lambda h, i: (0, h, 0)),
                  pl.BlockSpec((None, 1, width), lambda h, i: (l, 0, RET_WIDTH // width + h))],
        out_specs=pl.BlockSpec((tq, width), lambda h, i: (i, h)),
        out_shape=jax.ShapeDtypeStruct((s, MLA_WIDTH), BF16),
        scratch_shapes=[pltpu.VMEM((hp, 1, tq), F32),
                        pltpu.VMEM((hp, VT_ROWS, tq), F32),
                        pltpu.VMEM((hp, tq, tq), F32), pltpu.VMEM((hp, tq, tq), F32)],
        compiler_params=_cparams("arbitrary", "arbitrary"),
        name="mla_attention",
    )(qt, k, vt, mix)


def _gmlp_kernel(u_ref, v_ref, gv_ref, ws_ref, bs_ref, gm_ref, o_ref, *, n_chunks):
    row = lax.broadcasted_iota(jnp.int32, (CHUNK, CHUNK), 0)
    col = lax.broadcasted_iota(jnp.int32, (CHUNK, CHUNK), 1)
    for g in range(N_GMLP_GROUPS):
        cols = slice(g * HEAD_DIM, (g + 1) * HEAD_DIM)
        w = jnp.where(col <= row, ws_ref[g], 0.0).astype(BF16)
        bias = bs_ref[g]
        for c in range(n_chunks):
            rows = slice(c * CHUNK, (c + 1) * CHUNK)
            u = jax.nn.gelu(u_ref[rows, cols].astype(F32))
            v = jax.nn.gelu(v_ref[rows, cols].astype(F32))
            vn = _rms(v, gv_ref[:, cols]).astype(BF16)
            sg = jnp.dot(w, vn, preferred_element_type=F32) + bias
            o_ref[rows, cols] = _rms(u * sg, gm_ref[:, cols]).astype(o_ref.dtype)


def _gmlp(zt, gv, ws, bs, mix, l, t):
    s = zt.shape[0]
    return pl.pallas_call(
        functools.partial(_gmlp_kernel, n_chunks=t // CHUNK),
        grid=(s // t,),
        in_specs=[pl.BlockSpec((t, GMLP_WIDTH), lambda i: (i, ZT_GU // GMLP_WIDTH)),
                  pl.BlockSpec((t, GMLP_WIDTH), lambda i: (i, ZT_GV // GMLP_WIDTH)),
                  _layer_spec(gv, l, 1), _layer_spec(ws, l, 1), _layer_spec(bs, l, 1),
                  pl.BlockSpec((None, 1, GMLP_WIDTH),
                               lambda i: (l, 0, (RET_WIDTH + MLA_WIDTH) // GMLP_WIDTH))],
        out_specs=pl.BlockSpec((t, GMLP_WIDTH), lambda i: (i, 0)),
        out_shape=jax.ShapeDtypeStruct((s, GMLP_WIDTH), BF16),
        compiler_params=_cparams("arbitrary"),
        name="gmlp",
    )(zt, zt, gv, ws, bs, mix)


def _out_proj_kernel(x_ref, yr_ref, ym_ref, yg_ref, w_ref, o_ref):
    y = jnp.concatenate([yr_ref[...], ym_ref[...], yg_ref[...]], axis=-1)
    o_ref[...] = x_ref[...] + jnp.dot(y, w_ref[...], preferred_element_type=F32)


def _out_proj(x, yr, ym, yg, w, l, tm):
    s, d = x.shape
    blk = lambda n: pl.BlockSpec((tm, n), lambda i: (i, 0))
    return pl.pallas_call(
        _out_proj_kernel,
        grid=(s // tm,),
        in_specs=[blk(d), blk(RET_WIDTH), blk(MLA_WIDTH), blk(GMLP_WIDTH), _layer_spec(w, l, 1)],
        out_specs=blk(d),
        out_shape=jax.ShapeDtypeStruct((s, d), F32),
        compiler_params=_cparams("arbitrary"),
        name="out_proj",
    )(x, yr, ym, yg, w)


def _ffn_kernel(x_ref, g_ref, wg_ref, wu_ref, wd_ref, fn_ref, o_ref, h_sc, *, final_norm):
    f = pl.program_id(1)

    @pl.when(f == 0)
    def _():
        x = x_ref[...]
        h_sc[...] = _rms(x, g_ref[...]).astype(BF16)
        o_ref[...] = x

    h = h_sc[...]
    gate = jnp.dot(h, wg_ref[...], preferred_element_type=F32)
    up = jnp.dot(h, wu_ref[...], preferred_element_type=F32)
    act = (gate / (1.0 + jnp.exp(-gate)) * up).astype(BF16)
    o_ref[...] += jnp.dot(act, wd_ref[...], preferred_element_type=F32)

    if final_norm:
        @pl.when(f == pl.num_programs(1) - 1)
        def _():
            o_ref[...] = _rms(o_ref[...], fn_ref[...])


def _ffn_cast_kernel(x_ref, g_ref, wg_ref, wu_ref, wd_ref, fn_ref, ng_ref, nu_ref, nd_ref,
                     o_ref, cg_ref, cu_ref, cd_ref, h_sc):
    cg_ref[...] = ng_ref[...].astype(BF16)
    cu_ref[...] = nu_ref[...].astype(BF16)
    cd_ref[...] = nd_ref[...].astype(BF16)
    _ffn_kernel(x_ref, g_ref, wg_ref, wu_ref, wd_ref, fn_ref, o_ref, h_sc, final_norm=False)


def _ffn(x, gain, wg, wu, wd, fn_gain, l, tm, tf, final_norm, next_f32=None):
    s, d = x.shape
    dff = wg.shape[1]
    ni, nf = s // tm, dff // tf
    in_specs = [pl.BlockSpec((tm, d), lambda i, f: (i, 0)),
                _layer_spec(gain, l, 2),
                pl.BlockSpec((d, tf), lambda i, f: (0, f)),
                pl.BlockSpec((d, tf), lambda i, f: (0, f)),
                pl.BlockSpec((tf, d), lambda i, f: (f, 0)),
                pl.BlockSpec((1, d), lambda i, f: (0, 0))]
    x_spec = pl.BlockSpec((tm, d), lambda i, f: (i, 0))
    x_shape = jax.ShapeDtypeStruct((s, d), F32)
    scratch = [pltpu.VMEM((tm, d), BF16)]
    params = _cparams("arbitrary", "arbitrary")
    if next_f32 is None:
        return pl.pallas_call(
            functools.partial(_ffn_kernel, final_norm=final_norm),
            grid=(ni, nf), in_specs=in_specs, out_specs=x_spec, out_shape=x_shape,
            scratch_shapes=scratch, compiler_params=params, name="ffn",
        )(x, gain, wg, wu, wd, fn_gain)
    assert not final_norm and d % ni == 0 and dff % nf == 0
    rb, cb = d // ni, dff // nf
    in_specs += [pl.BlockSpec((None, rb, cb), lambda i, f: (l + 1, i, f)),
                 pl.BlockSpec((None, rb, cb), lambda i, f: (l + 1, i, f)),
                 pl.BlockSpec((None, cb, rb), lambda i, f: (l + 1, f, i))]
    out_specs = [x_spec,
                 pl.BlockSpec((rb, cb), lambda i, f: (i, f)),
                 pl.BlockSpec((rb, cb), lambda i, f: (i, f)),
                 pl.BlockSpec((cb, rb), lambda i, f: (f, i))]
    out_shape = [x_shape, jax.ShapeDtypeStruct((d, dff), BF16), jax.ShapeDtypeStruct((d, dff), BF16),
                 jax.ShapeDtypeStruct((dff, d), BF16)]
    return pl.pallas_call(
        _ffn_cast_kernel,
        grid=(ni, nf), in_specs=in_specs, out_specs=out_specs, out_shape=out_shape,
        scratch_shapes=scratch, name="ffn_cast",
        compiler_params=_cparams("arbitrary", "arbitrary", vmem_limit=VMEM_LIMIT_FFN_CAST),
    )(x, gain, wg, wu, wd, fn_gain, *next_f32)


def _spread_rope_cols(w):
    z = jnp.zeros(w.shape[:-1] + (32,), w.dtype)
    return jnp.concatenate([w[..., :32], z, w[..., 32:], z], axis=-1)


def _prep_w_in(w_in):
    w_in_b = w_in.astype(BF16)
    kpe = _spread_rope_cols(w_in_b[..., Z_MAIN:Z_MAIN + QK_ROPE_DIM])
    tail = jnp.concatenate([w_in_b[..., Z_MAIN + QK_ROPE_DIM:], kpe], axis=-1)
    return w_in_b, tail


def _prep_w_uq(w_uq):
    l, r, _ = w_uq.shape
    w = w_uq.reshape(l, r, N_MLA_HEADS, QK_HEAD_DIM)
    w = jnp.concatenate([w[..., :QK_NOPE_DIM], _spread_rope_cols(w[..., QK_NOPE_DIM:])], axis=-1)
    return w.reshape(l, r, N_MLA_HEADS * QK_PAD).astype(BF16)


def _prep_w_ukv(w_ukv):
    l, r, _ = w_ukv.shape
    w = w_ukv.reshape(l, r, N_MLA_HEADS, 2 * HEAD_DIM)
    k_nope = w[..., :HEAD_DIM].reshape(l, r, MLA_WIDTH)
    v = w[..., HEAD_DIM:].reshape(l, r, MLA_WIDTH)
    return jnp.concatenate([k_nope, v], axis=-1).astype(BF16)


def _tiles(s):
    pick = lambda pref: pref if s % pref == 0 else s
    attn = pick(512)
    return dict(in_tm=pick(1024), ret_t=pick(1024), ret_chunk=128, mla_tm=attn, attn_tq=attn,
                gmlp_t=pick(512), out_tm=pick(512), ffn_tm=pick(1024), ffn_tf=512)


def kernel(x, positions, attn_norm, w_in, mla_q_norm, w_uq, mla_kv_norm, w_ukv, gmlp_v_norm, gmlp_w_s,
           gmlp_b_s, mix_norm, w_out, ffn_norm, w_gate, w_up, w_down, final_norm):
    b, s, d = x.shape
    assert b == 1, "batch is folded away; only BATCH == 1 is supported"
    depth = w_in.shape[0]
    t = _tiles(s)

    w_in_b, w_in_tail = _prep_w_in(w_in)
    w_uq_b = _prep_w_uq(w_uq)
    w_ukv_b = _prep_w_ukv(w_ukv)
    w_out_b = w_out.astype(BF16)
    ffn_w = tuple(w[0].astype(BF16) for w in (w_gate, w_up, w_down))
    rows = lambda a: a.reshape(depth, 1, -1)
    attn_g, q_g, kv_g, gv_g, mix_g, ffn_g = map(
        rows, (attn_norm, mla_q_norm, mla_kv_norm, gmlp_v_norm, mix_norm, ffn_norm))
    b_s = gmlp_b_s[..., None]
    fn_g = final_norm.reshape(1, d)

    cos_r, sin_r, cos_m, sin_m = _rope_tables(positions)
    ret_tables = _retention_tables(t["ret_chunk"])
    xs = x.reshape(s, d)
    for l in range(depth):
        z, zt = _in_proj(xs, attn_g, w_in_b, w_in_tail, l, t["in_tm"])
        y_ret = _retention(z, cos_r, sin_r, ret_tables, mix_g, l, t["ret_t"], t["ret_chunk"])
        qt, k, vt = _mla_proj(z, zt, q_g, kv_g, w_uq_b, w_ukv_b, cos_m, sin_m, l, t["mla_tm"])
        y_mla = _attention(qt, k, vt, mix_g, l, t["attn_tq"])
        y_gm = _gmlp(zt, gv_g, gmlp_w_s, b_s, mix_g, l, t["gmlp_t"])
        xs = _out_proj(xs, y_ret, y_mla, y_gm, w_out_b, l, t["out_tm"])
        if l < depth - 1:
            xs, *ffn_w = _ffn(xs, ffn_g, *ffn_w, fn_g, l, t["ffn_tm"], t["ffn_tf"], final_norm=False,
                              next_f32=(w_gate, w_up, w_down))
        else:
            xs = _ffn(xs, ffn_g, *ffn_w, fn_g, l, t["ffn_tm"], t["ffn_tf"], final_norm=True)
    return xs.reshape(b, s, d)
```

```python
import functools

import jax
import jax.numpy as jnp
from jax import lax
from jax.experimental import pallas as pl
from jax.experimental.pallas import tpu as pltpu

F32 = jnp.float32
BF16 = jnp.bfloat16

HEAD_DIM = 128
N_RET_HEADS = 6
N_MLA_HEADS = 6
N_GMLP_GROUPS = 4
RET_WIDTH = N_RET_HEADS * HEAD_DIM
MLA_WIDTH = N_MLA_HEADS * HEAD_DIM
GMLP_WIDTH = N_GMLP_GROUPS * HEAD_DIM
Q_LORA_RANK = 512
KV_LORA_RANK = 512
QK_NOPE_DIM = 128
QK_ROPE_DIM = 64
QK_HEAD_DIM = QK_NOPE_DIM + QK_ROPE_DIM
CHUNK = 128
ROPE_BASE = 10000.0
EPS = 1e-6

LANES = 128
QK_PAD = 2 * LANES
ATTN_HEADS_PER_STEP = 3
VT_ROWS = HEAD_DIM + 16
LOG2_E = 1.4426950408889634

Z_CQ = 4 * RET_WIDTH
Z_CKV = Z_CQ + Q_LORA_RANK
Z_MAIN = Z_CKV + KV_LORA_RANK
Z_TN = 1024
ZT_GU = 0
ZT_GV = ZT_GU + GMLP_WIDTH
ZT_KPE = ZT_GV + GMLP_WIDTH
Z_TAIL = ZT_KPE + LANES

VMEM_LIMIT = 56 * 1024 * 1024
VMEM_LIMIT_FFN_CAST = 60 * 1024 * 1024


def _cparams(*sem, vmem_limit=VMEM_LIMIT):
    return pltpu.CompilerParams(dimension_semantics=sem, vmem_limit_bytes=vmem_limit)


def _layer_spec(arr, l, ngrid):
    zeros = (0,) * (arr.ndim - 1)
    return pl.BlockSpec((None,) + arr.shape[1:], lambda *_: (l,) + zeros)


def _rms(x, gain):
    return x * lax.rsqrt(jnp.mean(x * x, axis=-1, keepdims=True) + EPS) * gain


def _rope(x, cos, sin):
    return x * cos + pltpu.roll(x, 64, 1) * sin


def _rope_table_kernel(pos_ref, inv_r_ref, sgn_r_ref, inv_m_ref, cm_ref, sm_ref,
                       cos_r_ref, sin_r_ref, cos_m_ref, sin_m_ref):
    pos = pos_ref[...].astype(F32)
    ang_r = pos * inv_r_ref[...]
    cos_r_ref[...] = jnp.cos(ang_r)
    sin_r_ref[...] = jnp.sin(ang_r) * sgn_r_ref[...]
    ang_m = pos * inv_m_ref[...]
    cos_m_ref[...] = jnp.cos(ang_m) * cm_ref[...]
    sin_m_ref[...] = jnp.sin(ang_m) * sm_ref[...]


def _rope_tables(positions):
    s = positions.shape[-1]
    ts = min(s, 1024)
    pos = positions.reshape(s, 1)
    inv_r = 1.0 / (ROPE_BASE ** (jnp.arange(0, HEAD_DIM, 2, dtype=F32) / HEAD_DIM))
    inv_m = 1.0 / (ROPE_BASE ** (jnp.arange(0, QK_ROPE_DIM, 2, dtype=F32) / QK_ROPE_DIM))
    z32 = jnp.zeros((32,), F32)
    o32 = jnp.ones((32,), F32)
    inv_r_full = jnp.concatenate([inv_r, inv_r]).reshape(1, LANES)
    sgn_r = jnp.concatenate([-jnp.ones((64,), F32), jnp.ones((64,), F32)]).reshape(1, LANES)
    inv_m_full = jnp.concatenate([inv_m, z32, inv_m, z32]).reshape(1, LANES)
    cmask = jnp.concatenate([o32, z32, o32, z32]).reshape(1, LANES)
    smask = jnp.concatenate([-o32, z32, o32, z32]).reshape(1, LANES)
    row = pl.BlockSpec((1, LANES), lambda i: (0, 0))
    tab = pl.BlockSpec((ts, LANES), lambda i: (i, 0))
    out = jax.ShapeDtypeStruct((s, LANES), F32)
    return pl.pallas_call(
        _rope_table_kernel,
        grid=(s // ts,),
        in_specs=[pl.BlockSpec((ts, 1), lambda i: (i, 0)), row, row, row, row, row],
        out_specs=[tab, tab, tab, tab],
        out_shape=[out, out, out, out],
        compiler_params=_cparams("arbitrary"),
        name="rope_tables",
    )(pos, inv_r_full, sgn_r, inv_m_full, cmask, smask)


def _in_proj_kernel(x_ref, g_ref, wm_ref, wt_ref, zm_ref, zt_ref, h_sc, *, n_main):
    j = pl.program_id(1)

    @pl.when(j == 0)
    def _():
        h_sc[...] = _rms(x_ref[...], g_ref[...]).astype(BF16)

    @pl.when(j < n_main)
    def _():
        zm_ref[...] = jnp.dot(h_sc[...], wm_ref[...], preferred_element_type=F32).astype(zm_ref.dtype)

    @pl.when(j == n_main)
    def _():
        zt_ref[...] = jnp.dot(h_sc[...], wt_ref[...], preferred_element_type=F32).astype(zt_ref.dtype)


def _in_proj(x, gain, w_in_b, w_tail, l, tm):
    s, d = x.shape
    n_main = Z_MAIN // Z_TN
    last = n_main - 1
    return pl.pallas_call(
        functools.partial(_in_proj_kernel, n_main=n_main),
        grid=(s // tm, n_main + 1),
        in_specs=[pl.BlockSpec((tm, d), lambda i, j: (i, 0)),
                  _layer_spec(gain, l, 2),
                  pl.BlockSpec((None, d, Z_TN), lambda i, j: (l, 0, jnp.minimum(j, last))),
                  _layer_spec(w_tail, l, 2)],
        out_specs=[pl.BlockSpec((tm, Z_TN), lambda i, j: (i, jnp.minimum(j, last))),
                   pl.BlockSpec((tm, Z_TAIL), lambda i, j: (i, 0))],
        out_shape=[jax.ShapeDtypeStruct((s, Z_MAIN), BF16),
                   jax.ShapeDtypeStruct((s, Z_TAIL), BF16)],
        scratch_shapes=[pltpu.VMEM((tm, d), BF16)],
        compiler_params=_cparams("arbitrary", "arbitrary"),
        name="in_proj",
    )(x, gain, w_in_b, w_tail)


def _retention_kernel(q_ref, k_ref, v_ref, g_ref, cos_ref, sin_ref, inner_ref, qdec_ref, kdec_ref,
                      cdec_ref, gain_ref, o_ref, state_sc, *, chunk, n_chunks):
    @pl.when(pl.program_id(0) == 0)
    def _():
        state_sc[...] = jnp.zeros_like(state_sc)

    for c in range(n_chunks):
        rows = slice(c * chunk, (c + 1) * chunk)
        cos = cos_ref[rows, :]
        sin = sin_ref[rows, :]
        for h in range(N_RET_HEADS):
            cols = slice(h * HEAD_DIM, (h + 1) * HEAD_DIM)
            q = _rope(q_ref[rows, cols].astype(F32), cos, sin)
            k = _rope(k_ref[rows, cols].astype(F32), cos, sin) * (HEAD_DIM ** -0.5)
            v = v_ref[rows, cols]
            state = state_sc[h]
            scores = lax.dot_general(q.astype(BF16), k.astype(BF16), (((1,), (1,)), ((), ())),
                                     preferred_element_type=F32) * inner_ref[h]
            out = jnp.dot(scores.astype(BF16), v, preferred_element_type=F32)
            out += jnp.dot((q * qdec_ref[:, cols]).astype(BF16), state.astype(BF16),
                           preferred_element_type=F32)
            kt = (k * kdec_ref[:, cols]).T.astype(BF16)
            state_sc[h] = cdec_ref[:, cols] * state + jnp.dot(kt, v, preferred_element_type=F32)
            y = _rms(out, gain_ref[:, cols])
            gate = g_ref[rows, cols].astype(F32)
            o_ref[rows, cols] = (y * (gate / (1.0 + jnp.exp(-gate)))).astype(o_ref.dtype)


def _retention_tables(chunk):
    h = N_RET_HEADS
    log_gamma = jnp.log1p(-jnp.exp2(-5.0 - jnp.arange(h, dtype=F32)))
    idx = jnp.arange(chunk, dtype=F32)
    rel = idx[:, None] - idx[None, :]
    inner = jnp.where(rel >= 0, jnp.exp(log_gamma[:, None, None] * jnp.maximum(rel, 0.0)), 0.0)
    qdec = jnp.exp(log_gamma[None, :] * (idx[:, None] + 1.0))
    kdec = jnp.exp(log_gamma[None, :] * (chunk - 1.0 - idx[:, None]))
    cdec = jnp.exp(log_gamma * chunk)
    rep = lambda a: jnp.repeat(a, HEAD_DIM, axis=-1)
    return inner, rep(qdec), rep(kdec), rep(cdec[None, :])


def _retention(z, cos_r, sin_r, tables, mix, l, t, chunk):
    s = z.shape[0]
    zspec = lambda c: pl.BlockSpec((t, RET_WIDTH), lambda i: (i, c))
    tab = pl.BlockSpec((t, LANES), lambda i: (i, 0))
    full = lambda a: pl.BlockSpec(a.shape, lambda i: (0,) * a.ndim)
    return pl.pallas_call(
        functools.partial(_retention_kernel, chunk=chunk, n_chunks=t // chunk),
        grid=(s // t,),
        in_specs=[zspec(0), zspec(1), zspec(2), zspec(3), tab, tab] + [full(a) for a in tables]
        + [pl.BlockSpec((None, 1, RET_WIDTH), lambda i: (l, 0, 0))],
        out_specs=pl.BlockSpec((t, RET_WIDTH), lambda i: (i, 0)),
        out_shape=jax.ShapeDtypeStruct((s, RET_WIDTH), BF16),
        scratch_shapes=[pltpu.VMEM((N_RET_HEADS, HEAD_DIM, HEAD_DIM), F32)],
        compiler_params=_cparams("arbitrary"),
        name="retention",
    )(z, z, z, z, cos_r, sin_r, *tables, mix)


def _mla_proj_kernel(cq_ref, ckv_ref, kpe_ref, gq_ref, gkv_ref, wuq_ref, wukv_ref, cos_ref, sin_ref,
                     qt_ref, k_ref, vt_ref):
    cos = cos_ref[...]
    sin = sin_ref[...]
    scale = QK_HEAD_DIM ** -0.5 * LOG2_E
    cq = _rms(cq_ref[...].astype(F32), gq_ref[...]).astype(BF16)
    mq = jnp.dot(cq, wuq_ref[...], preferred_element_type=F32)
    ckv = _rms(ckv_ref[...].astype(F32), gkv_ref[...]).astype(BF16)
    mkv = jnp.dot(ckv, wukv_ref[...], preferred_element_type=F32)
    kpe = _rope(kpe_ref[...].astype(F32), cos, sin).astype(BF16)
    for h in range(N_MLA_HEADS):
        nope = slice(h * QK_PAD, h * QK_PAD + LANES)
        pe = slice(h * QK_PAD + LANES, (h + 1) * QK_PAD)
        qt_ref[nope, :] = (mq[:, nope] * scale).T.astype(BF16)
        qt_ref[pe, :] = (_rope(mq[:, pe], cos, sin) * scale).T.astype(BF16)
        k_ref[:, nope] = mkv[:, h * LANES:(h + 1) * LANES].astype(BF16)
        k_ref[:, pe] = kpe
    vt = mkv[:, MLA_WIDTH:].T.astype(BF16)
    tm = vt.shape[1]
    extra = lax.broadcasted_iota(jnp.int32, (VT_ROWS - HEAD_DIM, tm), 0)
    ones_row = jnp.where(extra == 0, 1.0, 0.0).astype(BF16)
    for h in range(N_MLA_HEADS):
        vt_ref[h * VT_ROWS:h * VT_ROWS + HEAD_DIM, :] = vt[h * HEAD_DIM:(h + 1) * HEAD_DIM, :]
        vt_ref[h * VT_ROWS + HEAD_DIM:(h + 1) * VT_ROWS, :] = ones_row


def _mla_proj(z, zt, gq, gkv, wuq, wukv, cos_m, sin_m, l, tm):
    s = z.shape[0]
    tab = pl.BlockSpec((tm, LANES), lambda i: (i, 0))
    qk = N_MLA_HEADS * QK_PAD
    return pl.pallas_call(
        _mla_proj_kernel,
        grid=(s // tm,),
        in_specs=[pl.BlockSpec((tm, Q_LORA_RANK), lambda i: (i, Z_CQ // Q_LORA_RANK)),
                  pl.BlockSpec((tm, KV_LORA_RANK), lambda i: (i, Z_CKV // KV_LORA_RANK)),
                  pl.BlockSpec((tm, LANES), lambda i: (i, ZT_KPE // LANES)),
                  _layer_spec(gq, l, 1), _layer_spec(gkv, l, 1),
                  _layer_spec(wuq, l, 1), _layer_spec(wukv, l, 1), tab, tab],
        out_specs=[pl.BlockSpec((None, qk, tm), lambda i: (i, 0, 0)),
                   pl.BlockSpec((tm, qk), lambda i: (i, 0)),
                   pl.BlockSpec((None, N_MLA_HEADS * VT_ROWS, tm), lambda i: (i, 0, 0))],
        out_shape=[jax.ShapeDtypeStruct((s // tm, qk, tm), BF16),
                   jax.ShapeDtypeStruct((s, qk), BF16),
                   jax.ShapeDtypeStruct((s // tm, N_MLA_HEADS * VT_ROWS, tm), BF16)],
        compiler_params=_cparams("arbitrary"),
        name="mla_proj",
    )(z, z, zt, gq, gkv, wuq, wukv, cos_m, sin_m)


def _attn_kernel(qt_ref, k_ref, vt_ref, gain_ref, o_ref, m_sc, acc_sc, st0_sc, st1_sc, *, tq, hp, n_q):
    lax.fori_loop(0, n_q, functools.partial(
        _attn_q_tile, qt_ref, k_ref, vt_ref, gain_ref, o_ref, m_sc, acc_sc, st0_sc, st1_sc, tq, hp), 0)


def _attn_q_tile(qt_ref, k_ref, vt_ref, gain_ref, o_ref, m_sc, acc_sc, st0_sc, st1_sc, tq, hp, qi, carry):
    m_sc[...] = jnp.full_like(m_sc, -jnp.inf)
    acc_sc[...] = jnp.zeros_like(acc_sc)

    def scores(kb, st_sc):
        start = pl.multiple_of(kb * tq, tq)
        for j in range(hp):
            qk_cols = slice(j * QK_PAD, (j + 1) * QK_PAD)
            st_sc[j] = jnp.dot(k_ref[pl.ds(start, tq), qk_cols], qt_ref[qi, qk_cols, :],
                               preferred_element_type=F32)

    def softmax_pv(kb, st_sc, masked):
        for j in range(hp):
            vt = vt_ref[kb, j * VT_ROWS:(j + 1) * VT_ROWS, :]
            st = st_sc[j]
            if masked:
                kv_pos = lax.broadcasted_iota(jnp.int32, st.shape, 0)
                q_pos = lax.broadcasted_iota(jnp.int32, st.shape, 1)
                st = jnp.where(kv_pos <= q_pos, st, -jnp.inf)
            m_prev = m_sc[j]
            m_new = jnp.maximum(m_prev, jnp.max(st, axis=0, keepdims=True))
            alpha = jnp.exp2(m_prev - m_new)
            p = jnp.exp2(st - m_new).astype(BF16)
            acc_sc[j] = alpha * acc_sc[j] + jnp.dot(vt, p, preferred_element_type=F32)
            m_sc[j] = m_new

    scores(0, st0_sc)

    def pair(i, carry):
        kb = 2 * i
        scores(kb + 1, st1_sc)
        softmax_pv(kb, st0_sc, False)
        scores(kb + 2, st0_sc)
        softmax_pv(kb + 1, st1_sc, False)
        return carry

    lax.fori_loop(0, qi // 2, pair, 0)

    @pl.when(qi % 2 == 0)
    def _():
        softmax_pv(qi, st0_sc, True)

    @pl.when(qi % 2 == 1)
    def _():
        scores(qi, st1_sc)
        softmax_pv(qi - 1, st0_sc, False)
        softmax_pv(qi, st1_sc, True)

    rows = pl.ds(pl.multiple_of(qi * tq, tq), tq)
    for j in range(hp):
        cols = slice(j * HEAD_DIM, (j + 1) * HEAD_DIM)
        denom = acc_sc[j, HEAD_DIM:HEAD_DIM + 1, :]
        o = (acc_sc[j, :HEAD_DIM, :] * (1.0 / denom)).T
        o_ref[rows, cols] = _rms(o, gain_ref[:, cols]).astype(o_ref.dtype)
    return carry


def _attention(qt, k, vt, mix, l, tq):
    s = k.shape[0]
    hp = ATTN_HEADS_PER_STEP
    width = hp * HEAD_DIM
    n_q = s // tq
    once = pl.Buffered(1)
    return pl.pallas_call(
        functools.partial(_attn_kernel, tq=tq, hp=hp, n_q=n_q),
        grid=(N_MLA_HEADS // hp,),
        in_specs=[pl.BlockSpec((n_q, hp * QK_PAD, tq), lambda h: (0, h, 0), pipeline_mode=once),
                  pl.BlockSpec((s, hp * QK_PAD), lambda h: (0, h), pipeline_mode=once),
                  pl.BlockSpec((n_q, hp * VT_ROWS, tq), lambda h: (0, h, 0), pipeline_mode=once),
                  pl.BlockSpec((None, 1, width), lambda h: (l, 0, RET_WIDTH // width + h))],
        out_specs=pl.BlockSpec((s, width), lambda h: (0, h), pipeline_mode=once),
        out_shape=jax.ShapeDtypeStruct((s, MLA_WIDTH), BF16),
        scratch_shapes=[pltpu.VMEM((hp, 1, tq), F32),
                        pltpu.VMEM((hp, VT_ROWS, tq), F32),
                        pltpu.VMEM((hp, tq, tq), F32), pltpu.VMEM((hp, tq, tq), F32)],
        compiler_params=_cparams("arbitrary"),
        name="mla_attention",
    )(qt, k, vt, mix)


def _gmlp_kernel(u_ref, v_ref, gv_ref, ws_ref, bs_ref, gm_ref, o_ref, *, n_chunks):
    row = lax.broadcasted_iota(jnp.int32, (CHUNK, CHUNK), 0)
    col = lax.broadcasted_iota(jnp.int32, (CHUNK, CHUNK), 1)
    for g in range(N_GMLP_GROUPS):
        cols = slice(g * HEAD_DIM, (g + 1) * HEAD_DIM)
        w = jnp.where(col <= row, ws_ref[g], 0.0).astype(BF16)
        bias = bs_ref[g]
        for c in range(n_chunks):
            rows = slice(c * CHUNK, (c + 1) * CHUNK)
            u = jax.nn.gelu(u_ref[rows, cols].astype(F32))
            v = jax.nn.gelu(v_ref[rows, cols].astype(F32))
            vn = _rms(v, gv_ref[:, cols]).astype(BF16)
            sg = jnp.dot(w, vn, preferred_element_type=F32) + bias
            o_ref[rows, cols] = _rms(u * sg, gm_ref[:, cols]).astype(o_ref.dtype)


def _gmlp(zt, gv, ws, bs, mix, l, t):
    s = zt.shape[0]
    return pl.pallas_call(
        functools.partial(_gmlp_kernel, n_chunks=t // CHUNK),
        grid=(s // t,),
        in_specs=[pl.BlockSpec((t, GMLP_WIDTH), lambda i: (i, ZT_GU // GMLP_WIDTH)),
                  pl.BlockSpec((t, GMLP_WIDTH), lambda i: (i, ZT_GV // GMLP_WIDTH)),
                  _layer_spec(gv, l, 1), _layer_spec(ws, l, 1), _layer_spec(bs, l, 1),
                  pl.BlockSpec((None, 1, GMLP_WIDTH),
                               lambda i: (l, 0, (RET_WIDTH + MLA_WIDTH) // GMLP_WIDTH))],
        out_specs=pl.BlockSpec((t, GMLP_WIDTH), lambda i: (i, 0)),
        out_shape=jax.ShapeDtypeStruct((s, GMLP_WIDTH), BF16),
        compiler_params=_cparams("arbitrary"),
        name="gmlp",
    )(zt, zt, gv, ws, bs, mix)


def _out_proj_kernel(x_ref, yr_ref, ym_ref, yg_ref, w_ref, o_ref):
    y = jnp.concatenate([yr_ref[...], ym_ref[...], yg_ref[...]], axis=-1)
    o_ref[...] = x_ref[...] + jnp.dot(y, w_ref[...], preferred_element_type=F32)


def _out_proj(x, yr, ym, yg, w, l, tm):
    s, d = x.shape
    blk = lambda n: pl.BlockSpec((tm, n), lambda i: (i, 0))
    return pl.pallas_call(
        _out_proj_kernel,
        grid=(s // tm,),
        in_specs=[blk(d), blk(RET_WIDTH), blk(MLA_WIDTH), blk(GMLP_WIDTH), _layer_spec(w, l, 1)],
        out_specs=blk(d),
        out_shape=jax.ShapeDtypeStruct((s, d), F32),
        compiler_params=_cparams("arbitrary"),
        name="out_proj",
    )(x, yr, ym, yg, w)


def _ffn_kernel(x_ref, g_ref, wg_ref, wu_ref, wd_ref, fn_ref, o_ref, h_sc, *, final_norm):
    f = pl.program_id(1)

    @pl.when(f == 0)
    def _():
        x = x_ref[...]
        h_sc[...] = _rms(x, g_ref[...]).astype(BF16)
        o_ref[...] = x

    h = h_sc[...]
    gate = jnp.dot(h, wg_ref[...], preferred_element_type=F32)
    up = jnp.dot(h, wu_ref[...], preferred_element_type=F32)
    act = (gate / (1.0 + jnp.exp(-gate)) * up).astype(BF16)
    o_ref[...] += jnp.dot(act, wd_ref[...], preferred_element_type=F32)

    if final_norm:
        @pl.when(f == pl.num_programs(1) - 1)
        def _():
            o_ref[...] = _rms(o_ref[...], fn_ref[...])


def _ffn_cast_kernel(x_ref, g_ref, wg_ref, wu_ref, wd_ref, fn_ref, ng_ref, nu_ref, nd_ref,
                     o_ref, cg_ref, cu_ref, cd_ref, h_sc):
    cg_ref[...] = ng_ref[...].astype(BF16)
    cu_ref[...] = nu_ref[...].astype(BF16)
    cd_ref[...] = nd_ref[...].astype(BF16)
    _ffn_kernel(x_ref, g_ref, wg_ref, wu_ref, wd_ref, fn_ref, o_ref, h_sc, final_norm=False)


def _ffn(x, gain, wg, wu, wd, fn_gain, l, tm, tf, final_norm, next_f32=None):
    s, d = x.shape
    dff = wg.shape[1]
    ni, nf = s // tm, dff // tf
    in_specs = [pl.BlockSpec((tm, d), lambda i, f: (i, 0)),
                _layer_spec(gain, l, 2),
                pl.BlockSpec((d, tf), lambda i, f: (0, f)),
                pl.BlockSpec((d, tf), lambda i, f: (0, f)),
                pl.BlockSpec((tf, d), lambda i, f: (f, 0)),
                pl.BlockSpec((1, d), lambda i, f: (0, 0))]
    x_spec = pl.BlockSpec((tm, d), lambda i, f: (i, 0))
    x_shape = jax.ShapeDtypeStruct((s, d), F32)
    scratch = [pltpu.VMEM((tm, d), BF16)]
    params = _cparams("arbitrary", "arbitrary")
    if next_f32 is None:
        return pl.pallas_call(
            functools.partial(_ffn_kernel, final_norm=final_norm),
            grid=(ni, nf), in_specs=in_specs, out_specs=x_spec, out_shape=x_shape,
            scratch_shapes=scratch, compiler_params=params, name="ffn",
        )(x, gain, wg, wu, wd, fn_gain)
    assert not final_norm and d % ni == 0 and dff % nf == 0
    rb, cb = d // ni, dff // nf
    in_specs += [pl.BlockSpec((None, rb, cb), lambda i, f: (l + 1, i, f)),
                 pl.BlockSpec((None, rb, cb), lambda i, f: (l + 1, i, f)),
                 pl.BlockSpec((None, cb, rb), lambda i, f: (l + 1, f, i))]
    out_specs = [x_spec,
                 pl.BlockSpec((rb, cb), lambda i, f: (i, f)),
                 pl.BlockSpec((rb, cb), lambda i, f: (i, f)),
                 pl.BlockSpec((cb, rb), lambda i, f: (f, i))]
    out_shape = [x_shape, jax.ShapeDtypeStruct((d, dff), BF16), jax.ShapeDtypeStruct((d, dff), BF16),
                 jax.ShapeDtypeStruct((dff, d), BF16)]
    return pl.pallas_call(
        _ffn_cast_kernel,
        grid=(ni, nf), in_specs=in_specs, out_specs=out_specs, out_shape=out_shape,
        scratch_shapes=scratch, name="ffn_cast",
        compiler_params=_cparams("arbitrary", "arbitrary", vmem_limit=VMEM_LIMIT_FFN_CAST),
    )(x, gain, wg, wu, wd, fn_gain, *next_f32)


def _spread_rope_cols(w):
    z = jnp.zeros(w.shape[:-1] + (32,), w.dtype)
    return jnp.concatenate([w[..., :32], z, w[..., 32:], z], axis=-1)


def _prep_w_in(w_in):
    w_in_b = w_in.astype(BF16)
    kpe = _spread_rope_cols(w_in_b[..., Z_MAIN:Z_MAIN + QK_ROPE_DIM])
    tail = jnp.concatenate([w_in_b[..., Z_MAIN + QK_ROPE_DIM:], kpe], axis=-1)
    return w_in_b, tail


def _prep_w_uq(w_uq):
    l, r, _ = w_uq.shape
    w = w_uq.reshape(l, r, N_MLA_HEADS, QK_HEAD_DIM)
    w = jnp.concatenate([w[..., :QK_NOPE_DIM], _spread_rope_cols(w[..., QK_NOPE_DIM:])], axis=-1)
    return w.reshape(l, r, N_MLA_HEADS * QK_PAD).astype(BF16)


def _prep_w_ukv(w_ukv):
    l, r, _ = w_ukv.shape
    w = w_ukv.reshape(l, r, N_MLA_HEADS, 2 * HEAD_DIM)
    k_nope = w[..., :HEAD_DIM].reshape(l, r, MLA_WIDTH)
    v = w[..., HEAD_DIM:].reshape(l, r, MLA_WIDTH)
    return jnp.concatenate([k_nope, v], axis=-1).astype(BF16)


def _tiles(s):
    pick = lambda pref: pref if s % pref == 0 else s
    attn = pick(512)
    return dict(in_tm=pick(1024), ret_t=pick(1024), ret_chunk=128, mla_tm=attn, attn_tq=attn,
                gmlp_t=pick(512), out_tm=pick(512), ffn_tm=pick(1024), ffn_tf=512)


def kernel(x, positions, attn_norm, w_in, mla_q_norm, w_uq, mla_kv_norm, w_ukv, gmlp_v_norm, gmlp_w_s,
           gmlp_b_s, mix_norm, w_out, ffn_norm, w_gate, w_up, w_down, final_norm):
    b, s, d = x.shape
    assert b == 1, "batch is folded away; only BATCH == 1 is supported"
    depth = w_in.shape[0]
    t = _tiles(s)

    w_in_b, w_in_tail = _prep_w_in(w_in)
    w_uq_b = _prep_w_uq(w_uq)
    w_ukv_b = _prep_w_ukv(w_ukv)
    w_out_b = w_out.astype(BF16)
    ffn_w = tuple(w[0].astype(BF16) for w in (w_gate, w_up, w_down))
    rows = lambda a: a.reshape(depth, 1, -1)
    attn_g, q_g, kv_g, gv_g, mix_g, ffn_g = map(
        rows, (attn_norm, mla_q_norm, mla_kv_norm, gmlp_v_norm, mix_norm, ffn_norm))
    b_s = gmlp_b_s[..., None]
    fn_g = final_norm.reshape(1, d)

    cos_r, sin_r, cos_m, sin_m = _rope_tables(positions)
    ret_tables = _retention_tables(t["ret_chunk"])
    xs = x.reshape(s, d)
    for l in range(depth):
        z, zt = _in_proj(xs, attn_g, w_in_b, w_in_tail, l, t["in_tm"])
        y_ret = _retention(z, cos_r, sin_r, ret_tables, mix_g, l, t["ret_t"], t["ret_chunk"])
        qt, k, vt = _mla_proj(z, zt, q_g, kv_g, w_uq_b, w_ukv_b, cos_m, sin_m, l, t["mla_tm"])
        y_mla = _attention(qt, k, vt, mix_g, l, t["attn_tq"])
        y_gm = _gmlp(zt, gv_g, gmlp_w_s, b_s, mix_g, l, t["gmlp_t"])
        xs = _out_proj(xs, y_ret, y_mla, y_gm, w_out_b, l, t["out_tm"])
        if l < depth - 1:
            xs, *ffn_w = _ffn(xs, ffn_g, *ffn_w, fn_g, l, t["ffn_tm"], t["ffn_tf"], final_norm=False,
                              next_f32=(w_gate, w_up, w_down))
        else:
            xs = _ffn(xs, ffn_g, *ffn_w, fn_g, l, t["ffn_tm"], t["ffn_tf"], final_norm=True)
    return xs.reshape(b, s, d)
```

```python
import functools

import jax
import jax.numpy as jnp
from jax import lax
from jax.experimental import pallas as pl
from jax.experimental.pallas import tpu as pltpu

F32 = jnp.float32
BF16 = jnp.bfloat16

HEAD_DIM = 128
N_RET_HEADS = 6
N_MLA_HEADS = 6
N_GMLP_GROUPS = 4
RET_WIDTH = N_RET_HEADS * HEAD_DIM
MLA_WIDTH = N_MLA_HEADS * HEAD_DIM
GMLP_WIDTH = N_GMLP_GROUPS * HEAD_DIM
Q_LORA_RANK = 512
KV_LORA_RANK = 512
QK_NOPE_DIM = 128
QK_ROPE_DIM = 64
QK_HEAD_DIM = QK_NOPE_DIM + QK_ROPE_DIM
CHUNK = 128
ROPE_BASE = 10000.0
EPS = 1e-6

LANES = 128
QK_PAD = 2 * LANES
ATTN_HEADS_PER_STEP = 3
VT_ROWS = HEAD_DIM + 16
LOG2_E = 1.4426950408889634

Z_CQ = 4 * RET_WIDTH
Z_CKV = Z_CQ + Q_LORA_RANK
Z_MAIN = Z_CKV + KV_LORA_RANK
Z_TN = 1024
ZT_GU = 0
ZT_GV = ZT_GU + GMLP_WIDTH
ZT_KPE = ZT_GV + GMLP_WIDTH
Z_TAIL = ZT_KPE + LANES

VMEM_LIMIT = 56 * 1024 * 1024
VMEM_LIMIT_FFN_CAST = 60 * 1024 * 1024


def _cparams(*sem, vmem_limit=VMEM_LIMIT):
    return pltpu.CompilerParams(dimension_semantics=sem, vmem_limit_bytes=vmem_limit)


def _layer_spec(arr, l, ngrid):
    zeros = (0,) * (arr.ndim - 1)
    return pl.BlockSpec((None,) + arr.shape[1:], lambda *_: (l,) + zeros)


def _rms(x, gain):
    return x * lax.rsqrt(jnp.mean(x * x, axis=-1, keepdims=True) + EPS) * gain


def _rope(x, cos, sin):
    return x * cos + pltpu.roll(x, 64, 1) * sin


def _rope_table_kernel(pos_ref, inv_r_ref, sgn_r_ref, inv_m_ref, cm_ref, sm_ref,
                       cos_r_ref, sin_r_ref, cos_m_ref, sin_m_ref):
    pos = pos_ref[...].astype(F32)
    ang_r = pos * inv_r_ref[...]
    cos_r_ref[...] = jnp.cos(ang_r)
    sin_r_ref[...] = jnp.sin(ang_r) * sgn_r_ref[...]
    ang_m = pos * inv_m_ref[...]
    cos_m_ref[...] = jnp.cos(ang_m) * cm_ref[...]
    sin_m_ref[...] = jnp.sin(ang_m) * sm_ref[...]


def _rope_tables(positions):
    s = positions.shape[-1]
    ts = min(s, 1024)
    pos = positions.reshape(s, 1)
    inv_r = 1.0 / (ROPE_BASE ** (jnp.arange(0, HEAD_DIM, 2, dtype=F32) / HEAD_DIM))
    inv_m = 1.0 / (ROPE_BASE ** (jnp.arange(0, QK_ROPE_DIM, 2, dtype=F32) / QK_ROPE_DIM))
    z32 = jnp.zeros((32,), F32)
    o32 = jnp.ones((32,), F32)
    inv_r_full = jnp.concatenate([inv_r, inv_r]).reshape(1, LANES)
    sgn_r = jnp.concatenate([-jnp.ones((64,), F32), jnp.ones((64,), F32)]).reshape(1, LANES)
    inv_m_full = jnp.concatenate([inv_m, z32, inv_m, z32]).reshape(1, LANES)
    cmask = jnp.concatenate([o32, z32, o32, z32]).reshape(1, LANES)
    smask = jnp.concatenate([-o32, z32, o32, z32]).reshape(1, LANES)
    row = pl.BlockSpec((1, LANES), lambda i: (0, 0))
    tab = pl.BlockSpec((ts, LANES), lambda i: (i, 0))
    out = jax.ShapeDtypeStruct((s, LANES), F32)
    return pl.pallas_call(
        _rope_table_kernel,
        grid=(s // ts,),
        in_specs=[pl.BlockSpec((ts, 1), lambda i: (i, 0)), row, row, row, row, row],
        out_specs=[tab, tab, tab, tab],
        out_shape=[out, out, out, out],
        compiler_params=_cparams("arbitrary"),
        name="rope_tables",
    )(pos, inv_r_full, sgn_r, inv_m_full, cmask, smask)


def _in_proj_kernel(x_ref, g_ref, wm_ref, wt_ref, zm_ref, zt_ref, h_sc, *, n_main):
    j = pl.program_id(1)

    @pl.when(j == 0)
    def _():
        h_sc[...] = _rms(x_ref[...], g_ref[...]).astype(BF16)

    @pl.when(j < n_main)
    def _():
        zm_ref[...] = jnp.dot(h_sc[...], wm_ref[...], preferred_element_type=F32).astype(zm_ref.dtype)

    @pl.when(j == n_main)
    def _():
        zt_ref[...] = jnp.dot(h_sc[...], wt_ref[...], preferred_element_type=F32).astype(zt_ref.dtype)


def _in_proj(x, gain, w_in_b, w_tail, l, tm):
    s, d = x.shape
    n_main = Z_MAIN // Z_TN
    last = n_main - 1
    return pl.pallas_call(
        functools.partial(_in_proj_kernel, n_main=n_main),
        grid=(s // tm, n_main + 1),
        in_specs=[pl.BlockSpec((tm, d), lambda i, j: (i, 0)),
                  _layer_spec(gain, l, 2),
                  pl.BlockSpec((None, d, Z_TN), lambda i, j: (l, 0, jnp.minimum(j, last))),
                  _layer_spec(w_tail, l, 2)],
        out_specs=[pl.BlockSpec((tm, Z_TN), lambda i, j: (i, jnp.minimum(j, last))),
                   pl.BlockSpec((tm, Z_TAIL), lambda i, j: (i, 0))],
        out_shape=[jax.ShapeDtypeStruct((s, Z_MAIN), BF16),
                   jax.ShapeDtypeStruct((s, Z_TAIL), BF16)],
        scratch_shapes=[pltpu.VMEM((tm, d), BF16)],
        compiler_params=_cparams("arbitrary", "arbitrary"),
        name="in_proj",
    )(x, gain, w_in_b, w_tail)


def _retention_kernel(q_ref, k_ref, v_ref, g_ref, cos_ref, sin_ref, inner_ref, qdec_ref, kdec_ref,
                      cdec_ref, gain_ref, o_ref, state_sc, *, chunk, n_chunks):
    @pl.when(pl.program_id(0) == 0)
    def _():
        state_sc[...] = jnp.zeros_like(state_sc)

    for c in range(n_chunks):
        rows = slice(c * chunk, (c + 1) * chunk)
        cos = cos_ref[rows, :]
        sin = sin_ref[rows, :]
        for h in range(N_RET_HEADS):
            cols = slice(h * HEAD_DIM, (h + 1) * HEAD_DIM)
            q = _rope(q_ref[rows, cols].astype(F32), cos, sin)
            k = _rope(k_ref[rows, cols].astype(F32), cos, sin) * (HEAD_DIM ** -0.5)
            v = v_ref[rows, cols]
            state = state_sc[h]
            scores = lax.dot_general(q.astype(BF16), k.astype(BF16), (((1,), (1,)), ((), ())),
                                     preferred_element_type=F32) * inner_ref[h]
            out = jnp.dot(scores.astype(BF16), v, preferred_element_type=F32)
            out += jnp.dot((q * qdec_ref[:, cols]).astype(BF16), state.astype(BF16),
                           preferred_element_type=F32)
            kt = (k * kdec_ref[:, cols]).T.astype(BF16)
            state_sc[h] = cdec_ref[:, cols] * state + jnp.dot(kt, v, preferred_element_type=F32)
            y = _rms(out, gain_ref[:, cols])
            gate = g_ref[rows, cols].astype(F32)
            o_ref[rows, cols] = (y * (gate / (1.0 + jnp.exp(-gate)))).astype(o_ref.dtype)


def _retention_tables(chunk):
    h = N_RET_HEADS
    log_gamma = jnp.log1p(-jnp.exp2(-5.0 - jnp.arange(h, dtype=F32)))
    idx = jnp.arange(chunk, dtype=F32)
    rel = idx[:, None] - idx[None, :]
    inner = jnp.where(rel >= 0, jnp.exp(log_gamma[:, None, None] * jnp.maximum(rel, 0.0)), 0.0)
    qdec = jnp.exp(log_gamma[None, :] * (idx[:, None] + 1.0))
    kdec = jnp.exp(log_gamma[None, :] * (chunk - 1.0 - idx[:, None]))
    cdec = jnp.exp(log_gamma * chunk)
    rep = lambda a: jnp.repeat(a, HEAD_DIM, axis=-1)
    return inner, rep(qdec), rep(kdec), rep(cdec[None, :])


def _retention_part(z, cos_r, sin_r, tables, mix, l, t, chunk):
    s = z.shape[0]
    zspec = lambda c: pl.BlockSpec((t, RET_WIDTH), lambda i: (i, c))
    tab = pl.BlockSpec((t, LANES), lambda i: (i, 0))
    full = lambda a: pl.BlockSpec(a.shape, lambda i: (0,) * a.ndim)
    return (functools.partial(_retention_kernel, chunk=chunk, n_chunks=t // chunk),
            [z, z, z, z, cos_r, sin_r, *tables, mix],
            [zspec(0), zspec(1), zspec(2), zspec(3), tab, tab] + [full(a) for a in tables]
            + [pl.BlockSpec((None, 1, RET_WIDTH), lambda i: (l, 0, 0))],
            [pl.BlockSpec((t, RET_WIDTH), lambda i: (i, 0))],
            [jax.ShapeDtypeStruct((s, RET_WIDTH), BF16)],
            [pltpu.VMEM((N_RET_HEADS, HEAD_DIM, HEAD_DIM), F32)])


def _mla_proj_kernel(cq_ref, ckv_ref, kpe_ref, gq_ref, gkv_ref, wuq_ref, wukv_ref, cos_ref, sin_ref,
                     qt_ref, k_ref, vt_ref):
    cos = cos_ref[...]
    sin = sin_ref[...]
    scale = QK_HEAD_DIM ** -0.5 * LOG2_E
    cq = _rms(cq_ref[...].astype(F32), gq_ref[...]).astype(BF16)
    mq = jnp.dot(cq, wuq_ref[...], preferred_element_type=F32)
    ckv = _rms(ckv_ref[...].astype(F32), gkv_ref[...]).astype(BF16)
    mkv = jnp.dot(ckv, wukv_ref[...], preferred_element_type=F32)
    kpe = _rope(kpe_ref[...].astype(F32), cos, sin).astype(BF16)
    for h in range(N_MLA_HEADS):
        nope = slice(h * QK_PAD, h * QK_PAD + LANES)
        pe = slice(h * QK_PAD + LANES, (h + 1) * QK_PAD)
        qt_ref[nope, :] = (mq[:, nope] * scale).T.astype(BF16)
        qt_ref[pe, :] = (_rope(mq[:, pe], cos, sin) * scale).T.astype(BF16)
        k_ref[:, nope] = mkv[:, h * LANES:(h + 1) * LANES].astype(BF16)
        k_ref[:, pe] = kpe
    vt = mkv[:, MLA_WIDTH:].T.astype(BF16)
    tm = vt.shape[1]
    extra = lax.broadcasted_iota(jnp.int32, (VT_ROWS - HEAD_DIM, tm), 0)
    ones_row = jnp.where(extra == 0, 1.0, 0.0).astype(BF16)
    for h in range(N_MLA_HEADS):
        vt_ref[h * VT_ROWS:h * VT_ROWS + HEAD_DIM, :] = vt[h * HEAD_DIM:(h + 1) * HEAD_DIM, :]
        vt_ref[h * VT_ROWS + HEAD_DIM:(h + 1) * VT_ROWS, :] = ones_row


def _mla_proj_part(z, zt, gq, gkv, wuq, wukv, cos_m, sin_m, l, tm):
    s = z.shape[0]
    tab = pl.BlockSpec((tm, LANES), lambda i: (i, 0))
    qk = N_MLA_HEADS * QK_PAD
    return (_mla_proj_kernel,
            [z, z, zt, gq, gkv, wuq, wukv, cos_m, sin_m],
            [pl.BlockSpec((tm, Q_LORA_RANK), lambda i: (i, Z_CQ // Q_LORA_RANK)),
             pl.BlockSpec((tm, KV_LORA_RANK), lambda i: (i, Z_CKV // KV_LORA_RANK)),
             pl.BlockSpec((tm, LANES), lambda i: (i, ZT_KPE // LANES)),
             _layer_spec(gq, l, 1), _layer_spec(gkv, l, 1),
             _layer_spec(wuq, l, 1), _layer_spec(wukv, l, 1), tab, tab],
            [pl.BlockSpec((qk, tm), lambda i: (0, i)),
             pl.BlockSpec((tm, qk), lambda i: (i, 0)),
             pl.BlockSpec((None, N_MLA_HEADS * VT_ROWS, tm), lambda i: (i, 0, 0))],
            [jax.ShapeDtypeStruct((qk, s), BF16),
             jax.ShapeDtypeStruct((s, qk), BF16),
             jax.ShapeDtypeStruct((s // tm, N_MLA_HEADS * VT_ROWS, tm), BF16)],
            [])


def _attn_kernel(qt_ref, k_ref, vt_ref, gain_ref, o_ref, m_sc, acc_sc, st0_sc, st1_sc, *, tq, hp):
    qi = pl.program_id(1)
    m_sc[...] = jnp.full_like(m_sc, -jnp.inf)
    acc_sc[...] = jnp.zeros_like(acc_sc)

    def scores(kb, st_sc):
        start = pl.multiple_of(kb * tq, tq)
        for j in range(hp):
            qk_cols = slice(j * QK_PAD, (j + 1) * QK_PAD)
            st_sc[j] = jnp.dot(k_ref[pl.ds(start, tq), qk_cols], qt_ref[qk_cols, :],
                               preferred_element_type=F32)

    def softmax_pv(kb, st_sc, masked):
        for j in range(hp):
            vt = vt_ref[kb, j * VT_ROWS:(j + 1) * VT_ROWS, :]
            st = st_sc[j]
            if masked:
                kv_pos = lax.broadcasted_iota(jnp.int32, st.shape, 0)
                q_pos = lax.broadcasted_iota(jnp.int32, st.shape, 1)
                st = jnp.where(kv_pos <= q_pos, st, -jnp.inf)
            m_prev = m_sc[j]
            m_new = jnp.maximum(m_prev, jnp.max(st, axis=0, keepdims=True))
            alpha = jnp.exp2(m_prev - m_new)
            p = jnp.exp2(st - m_new).astype(BF16)
            acc_sc[j] = alpha * acc_sc[j] + jnp.dot(vt, p, preferred_element_type=F32)
            m_sc[j] = m_new

    scores(0, st0_sc)

    def pair(i, carry):
        kb = 2 * i
        scores(kb + 1, st1_sc)
        softmax_pv(kb, st0_sc, False)
        scores(kb + 2, st0_sc)
        softmax_pv(kb + 1, st1_sc, False)
        return carry

    lax.fori_loop(0, qi // 2, pair, 0)

    @pl.when(qi % 2 == 0)
    def _():
        softmax_pv(qi, st0_sc, True)

    @pl.when(qi % 2 == 1)
    def _():
        scores(qi, st1_sc)
        softmax_pv(qi - 1, st0_sc, False)
        softmax_pv(qi, st1_sc, True)

    for j in range(hp):
        cols = slice(j * HEAD_DIM, (j + 1) * HEAD_DIM)
        denom = acc_sc[j, HEAD_DIM:HEAD_DIM + 1, :]
        o = (acc_sc[j, :HEAD_DIM, :] * (1.0 / denom)).T
        o_ref[:, cols] = _rms(o, gain_ref[:, cols]).astype(o_ref.dtype)


def _attention(qt, k, vt, mix, l, tq):
    s = k.shape[0]
    hp = ATTN_HEADS_PER_STEP
    width = hp * HEAD_DIM
    return pl.pallas_call(
        functools.partial(_attn_kernel, tq=tq, hp=hp),
        grid=(N_MLA_HEADS // hp, s // tq),
        in_specs=[pl.BlockSpec((hp * QK_PAD, tq), lambda h, i: (h, i)),
                  pl.BlockSpec((s, hp * QK_PAD), lambda h, i: (0, h)),
                  pl.BlockSpec((s // tq, hp * VT_ROWS, tq), lambda h, i: (0, h, 0)),
                  pl.BlockSpec((None, 1, width), lambda h, i: (l, 0, RET_WIDTH // width + h))],
        out_specs=pl.BlockSpec((tq, width), lambda h, i: (i, h)),
        out_shape=jax.ShapeDtypeStruct((s, MLA_WIDTH), BF16),
        scratch_shapes=[pltpu.VMEM((hp, 1, tq), F32),
                        pltpu.VMEM((hp, VT_ROWS, tq), F32),
                        pltpu.VMEM((hp, tq, tq), F32), pltpu.VMEM((hp, tq, tq), F32)],
        compiler_params=_cparams("arbitrary", "arbitrary"),
        name="mla_attention",
    )(qt, k, vt, mix)


def _gmlp_kernel(u_ref, v_ref, gv_ref, ws_ref, bs_ref, gm_ref, o_ref, *, n_chunks):
    row = lax.broadcasted_iota(jnp.int32, (CHUNK, CHUNK), 0)
    col = lax.broadcasted_iota(jnp.int32, (CHUNK, CHUNK), 1)
    for g in range(N_GMLP_GROUPS):
        cols = slice(g * HEAD_DIM, (g + 1) * HEAD_DIM)
        w = jnp.where(col <= row, ws_ref[g], 0.0).astype(BF16)
        bias = bs_ref[g]
        for c in range(n_chunks):
            rows = slice(c * CHUNK, (c + 1) * CHUNK)
            u = jax.nn.gelu(u_ref[rows, cols].astype(F32))
            v = jax.nn.gelu(v_ref[rows, cols].astype(F32))
            vn = _rms(v, gv_ref[:, cols]).astype(BF16)
            sg = jnp.dot(w, vn, preferred_element_type=F32) + bias
            o_ref[rows, cols] = _rms(u * sg, gm_ref[:, cols]).astype(o_ref.dtype)


def _gmlp_part(zt, gv, ws, bs, mix, l, t):
    s = zt.shape[0]
    return (functools.partial(_gmlp_kernel, n_chunks=t // CHUNK),
            [zt, zt, gv, ws, bs, mix],
            [pl.BlockSpec((t, GMLP_WIDTH), lambda i: (i, ZT_GU // GMLP_WIDTH)),
             pl.BlockSpec((t, GMLP_WIDTH), lambda i: (i, ZT_GV // GMLP_WIDTH)),
             _layer_spec(gv, l, 1), _layer_spec(ws, l, 1), _layer_spec(bs, l, 1),
             pl.BlockSpec((None, 1, GMLP_WIDTH),
                          lambda i: (l, 0, (RET_WIDTH + MLA_WIDTH) // GMLP_WIDTH))],
            [pl.BlockSpec((t, GMLP_WIDTH), lambda i: (i, 0))],
            [jax.ShapeDtypeStruct((s, GMLP_WIDTH), BF16)],
            [])


def _mixers_kernel(*refs, bodies, n_in, n_out, n_scratch):
    ins, outs, scr = [], [], []
    pos = 0
    for group, counts in ((ins, n_in), (outs, n_out), (scr, n_scratch)):
        for c in counts:
            group.append(refs[pos:pos + c])
            pos += c
    for body, i, o, sc in zip(bodies, ins, outs, scr):
        body(*i, *o, *sc)


def _mixers(parts, n_tiles):
    bodies = [p[0] for p in parts]
    flat = lambda k: [a for p in parts for a in p[k]]
    return pl.pallas_call(
        functools.partial(_mixers_kernel, bodies=bodies, n_in=[len(p[1]) for p in parts],
                          n_out=[len(p[3]) for p in parts], n_scratch=[len(p[5]) for p in parts]),
        grid=(n_tiles,),
        in_specs=flat(2), out_specs=flat(3), out_shape=flat(4), scratch_shapes=flat(5),
        compiler_params=_cparams("arbitrary"),
        name="mixers",
    )(*flat(1))


def _out_proj_kernel(x_ref, yr_ref, ym_ref, yg_ref, w_ref, o_ref):
    y = jnp.concatenate([yr_ref[...], ym_ref[...], yg_ref[...]], axis=-1)
    o_ref[...] = x_ref[...] + jnp.dot(y, w_ref[...], preferred_element_type=F32)


def _out_proj(x, yr, ym, yg, w, l, tm):
    s, d = x.shape
    blk = lambda n: pl.BlockSpec((tm, n), lambda i: (i, 0))
    return pl.pallas_call(
        _out_proj_kernel,
        grid=(s // tm,),
        in_specs=[blk(d), blk(RET_WIDTH), blk(MLA_WIDTH), blk(GMLP_WIDTH), _layer_spec(w, l, 1)],
        out_specs=blk(d),
        out_shape=jax.ShapeDtypeStruct((s, d), F32),
        compiler_params=_cparams("arbitrary"),
        name="out_proj",
    )(x, yr, ym, yg, w)


def _ffn_kernel(x_ref, g_ref, wg_ref, wu_ref, wd_ref, fn_ref, o_ref, h_sc, *, final_norm):
    f = pl.program_id(1)

    @pl.when(f == 0)
    def _():
        x = x_ref[...]
        h_sc[...] = _rms(x, g_ref[...]).astype(BF16)
        o_ref[...] = x

    h = h_sc[...]
    gate = jnp.dot(h, wg_ref[...], preferred_element_type=F32)
    up = jnp.dot(h, wu_ref[...], preferred_element_type=F32)
    act = (gate / (1.0 + jnp.exp(-gate)) * up).astype(BF16)
    o_ref[...] += jnp.dot(act, wd_ref[...], preferred_element_type=F32)

    if final_norm:
        @pl.when(f == pl.num_programs(1) - 1)
        def _():
            o_ref[...] = _rms(o_ref[...], fn_ref[...])


def _ffn_cast_kernel(x_ref, g_ref, wg_ref, wu_ref, wd_ref, fn_ref, ng_ref, nu_ref, nd_ref,
                     o_ref, cg_ref, cu_ref, cd_ref, h_sc):
    cg_ref[...] = ng_ref[...].astype(BF16)
    cu_ref[...] = nu_ref[...].astype(BF16)
    cd_ref[...] = nd_ref[...].astype(BF16)
    _ffn_kernel(x_ref, g_ref, wg_ref, wu_ref, wd_ref, fn_ref, o_ref, h_sc, final_norm=False)


def _ffn(x, gain, wg, wu, wd, fn_gain, l, tm, tf, final_norm, next_f32=None):
    s, d = x.shape
    dff = wg.shape[1]
    ni, nf = s // tm, dff // tf
    in_specs = [pl.BlockSpec((tm, d), lambda i, f: (i, 0)),
                _layer_spec(gain, l, 2),
                pl.BlockSpec((d, tf), lambda i, f: (0, f)),
                pl.BlockSpec((d, tf), lambda i, f: (0, f)),
                pl.BlockSpec((tf, d), lambda i, f: (f, 0)),
                pl.BlockSpec((1, d), lambda i, f: (0, 0))]
    x_spec = pl.BlockSpec((tm, d), lambda i, f: (i, 0))
    x_shape = jax.ShapeDtypeStruct((s, d), F32)
    scratch = [pltpu.VMEM((tm, d), BF16)]
    params = _cparams("arbitrary", "arbitrary")
    if next_f32 is None:
        return pl.pallas_call(
            functools.partial(_ffn_kernel, final_norm=final_norm),
            grid=(ni, nf), in_specs=in_specs, out_specs=x_spec, out_shape=x_shape,
            scratch_shapes=scratch, compiler_params=params, name="ffn",
        )(x, gain, wg, wu, wd, fn_gain)
    assert not final_norm and d % ni == 0 and dff % nf == 0
    rb, cb = d // ni, dff // nf
    in_specs += [pl.BlockSpec((None, rb, cb), lambda i, f: (l + 1, i, f)),
                 pl.BlockSpec((None, rb, cb), lambda i, f: (l + 1, i, f)),
                 pl.BlockSpec((None, cb, rb), lambda i, f: (l + 1, f, i))]
    out_specs = [x_spec,
                 pl.BlockSpec((rb, cb), lambda i, f: (i, f)),
                 pl.BlockSpec((rb, cb), lambda i, f: (i, f)),
                 pl.BlockSpec((cb, rb), lambda i, f: (f, i))]
    out_shape = [x_shape, jax.ShapeDtypeStruct((d, dff), BF16), jax.ShapeDtypeStruct((d, dff), BF16),
                 jax.ShapeDtypeStruct((dff, d), BF16)]
    return pl.pallas_call(
        _ffn_cast_kernel,
        grid=(ni, nf), in_specs=in_specs, out_specs=out_specs, out_shape=out_shape,
        scratch_shapes=scratch, name="ffn_cast",
        compiler_params=_cparams("arbitrary", "arbitrary", vmem_limit=VMEM_LIMIT_FFN_CAST),
    )(x, gain, wg, wu, wd, fn_gain, *next_f32)


def _spread_rope_cols(w):
    z = jnp.zeros(w.shape[:-1] + (32,), w.dtype)
    return jnp.concatenate([w[..., :32], z, w[..., 32:], z], axis=-1)


def _prep_w_in(w_in):
    w_in_b = w_in.astype(BF16)
    kpe = _spread_rope_cols(w_in_b[..., Z_MAIN:Z_MAIN + QK_ROPE_DIM])
    tail = jnp.concatenate([w_in_b[..., Z_MAIN + QK_ROPE_DIM:], kpe], axis=-1)
    return w_in_b, tail


def _prep_w_uq(w_uq):
    l, r, _ = w_uq.shape
    w = w_uq.reshape(l, r, N_MLA_HEADS, QK_HEAD_DIM)
    w = jnp.concatenate([w[..., :QK_NOPE_DIM], _spread_rope_cols(w[..., QK_NOPE_DIM:])], axis=-1)
    return w.reshape(l, r, N_MLA_HEADS * QK_PAD).astype(BF16)


def _prep_w_ukv(w_ukv):
    l, r, _ = w_ukv.shape
    w = w_ukv.reshape(l, r, N_MLA_HEADS, 2 * HEAD_DIM)
    k_nope = w[..., :HEAD_DIM].reshape(l, r, MLA_WIDTH)
    v = w[..., HEAD_DIM:].reshape(l, r, MLA_WIDTH)
    return jnp.concatenate([k_nope, v], axis=-1).astype(BF16)


def _tiles(s):
    pick = lambda pref: pref if s % pref == 0 else s
    attn = pick(512)
    return dict(in_tm=pick(1024), mix_tm=attn, ret_chunk=128, attn_tq=attn,
                out_tm=pick(512), ffn_tm=pick(1024), ffn_tf=512)


def kernel(x, positions, attn_norm, w_in, mla_q_norm, w_uq, mla_kv_norm, w_ukv, gmlp_v_norm, gmlp_w_s,
           gmlp_b_s, mix_norm, w_out, ffn_norm, w_gate, w_up, w_down, final_norm):
    b, s, d = x.shape
    assert b == 1, "batch is folded away; only BATCH == 1 is supported"
    depth = w_in.shape[0]
    t = _tiles(s)

    w_in_b, w_in_tail = _prep_w_in(w_in)
    w_uq_b = _prep_w_uq(w_uq)
    w_ukv_b = _prep_w_ukv(w_ukv)
    w_out_b = w_out.astype(BF16)
    ffn_w = tuple(w[0].astype(BF16) for w in (w_gate, w_up, w_down))
    rows = lambda a: a.reshape(depth, 1, -1)
    attn_g, q_g, kv_g, gv_g, mix_g, ffn_g = map(
        rows, (attn_norm, mla_q_norm, mla_kv_norm, gmlp_v_norm, mix_norm, ffn_norm))
    b_s = gmlp_b_s[..., None]
    fn_g = final_norm.reshape(1, d)

    cos_r, sin_r, cos_m, sin_m = _rope_tables(positions)
    ret_tables = _retention_tables(t["ret_chunk"])
    xs = x.reshape(s, d)
    for l in range(depth):
        z, zt = _in_proj(xs, attn_g, w_in_b, w_in_tail, l, t["in_tm"])
        tm = t["mix_tm"]
        y_ret, qt, k, vt, y_gm = _mixers(
            [_retention_part(z, cos_r, sin_r, ret_tables, mix_g, l, tm, t["ret_chunk"]),
             _mla_proj_part(z, zt, q_g, kv_g, w_uq_b, w_ukv_b, cos_m, sin_m, l, tm),
             _gmlp_part(zt, gv_g, gmlp_w_s, b_s, mix_g, l, tm)], s // tm)
        y_mla = _attention(qt, k, vt, mix_g, l, t["attn_tq"])
        xs = _out_proj(xs, y_ret, y_mla, y_gm, w_out_b, l, t["out_tm"])
        if l < depth - 1:
            xs, *ffn_w = _ffn(xs, ffn_g, *ffn_w, fn_g, l, t["ffn_tm"], t["ffn_tf"], final_norm=False,
                              next_f32=(w_gate, w_up, w_down))
        else:
            xs = _ffn(xs, ffn_g, *ffn_w, fn_g, l, t["ffn_tm"], t["ffn_tf"], final_norm=True)
    return xs.reshape(b, s, d)
```

```python
import functools

import jax
import jax.numpy as jnp
from jax import lax
from jax.experimental import pallas as pl
from jax.experimental.pallas import tpu as pltpu

F32 = jnp.float32
BF16 = jnp.bfloat16

HEAD_DIM = 128
N_RET_HEADS = 6
N_MLA_HEADS = 6
N_GMLP_GROUPS = 4
RET_WIDTH = N_RET_HEADS * HEAD_DIM
MLA_WIDTH = N_MLA_HEADS * HEAD_DIM
GMLP_WIDTH = N_GMLP_GROUPS * HEAD_DIM
Q_LORA_RANK = 512
KV_LORA_RANK = 512
QK_NOPE_DIM = 128
QK_ROPE_DIM = 64
QK_HEAD_DIM = QK_NOPE_DIM + QK_ROPE_DIM
CHUNK = 128
ROPE_BASE = 10000.0
EPS = 1e-6

LANES = 128
QK_PAD = 2 * LANES
ATTN_HEADS_PER_STEP = 3
VT_ROWS = HEAD_DIM + 16
LOG2_E = 1.4426950408889634

Z_CQ = 4 * RET_WIDTH
Z_CKV = Z_CQ + Q_LORA_RANK
Z_MAIN = Z_CKV + KV_LORA_RANK
Z_TN = 1024
ZT_GU = 0
ZT_GV = ZT_GU + GMLP_WIDTH
ZT_KPE = ZT_GV + GMLP_WIDTH
Z_TAIL = ZT_KPE + LANES

VMEM_LIMIT = 56 * 1024 * 1024
VMEM_LIMIT_FFN_CAST = 60 * 1024 * 1024


def _cparams(*sem, vmem_limit=VMEM_LIMIT):
    return pltpu.CompilerParams(dimension_semantics=sem, vmem_limit_bytes=vmem_limit)


def _layer_spec(arr, l, ngrid):
    zeros = (0,) * (arr.ndim - 1)
    return pl.BlockSpec((None,) + arr.shape[1:], lambda *_: (l,) + zeros)


def _rms(x, gain):
    return x * lax.rsqrt(jnp.mean(x * x, axis=-1, keepdims=True) + EPS) * gain


def _rope(x, cos, sin):
    return x * cos + pltpu.roll(x, 64, 1) * sin


def _rope_table_kernel(pos_ref, inv_r_ref, sgn_r_ref, inv_m_ref, cm_ref, sm_ref,
                       cos_r_ref, sin_r_ref, cos_m_ref, sin_m_ref):
    pos = pos_ref[...].astype(F32)
    ang_r = pos * inv_r_ref[...]
    cos_r_ref[...] = jnp.cos(ang_r)
    sin_r_ref[...] = jnp.sin(ang_r) * sgn_r_ref[...]
    ang_m = pos * inv_m_ref[...]
    cos_m_ref[...] = jnp.cos(ang_m) * cm_ref[...]
    sin_m_ref[...] = jnp.sin(ang_m) * sm_ref[...]


def _rope_tables(positions):
    s = positions.shape[-1]
    ts = min(s, 1024)
    pos = positions.reshape(s, 1)
    inv_r = 1.0 / (ROPE_BASE ** (jnp.arange(0, HEAD_DIM, 2, dtype=F32) / HEAD_DIM))
    inv_m = 1.0 / (ROPE_BASE ** (jnp.arange(0, QK_ROPE_DIM, 2, dtype=F32) / QK_ROPE_DIM))
    z32 = jnp.zeros((32,), F32)
    o32 = jnp.ones((32,), F32)
    inv_r_full = jnp.concatenate([inv_r, inv_r]).reshape(1, LANES)
    sgn_r = jnp.concatenate([-jnp.ones((64,), F32), jnp.ones((64,), F32)]).reshape(1, LANES)
    inv_m_full = jnp.concatenate([inv_m, z32, inv_m, z32]).reshape(1, LANES)
    cmask = jnp.concatenate([o32, z32, o32, z32]).reshape(1, LANES)
    smask = jnp.concatenate([-o32, z32, o32, z32]).reshape(1, LANES)
    row = pl.BlockSpec((1, LANES), lambda i: (0, 0))
    tab = pl.BlockSpec((ts, LANES), lambda i: (i, 0))
    out = jax.ShapeDtypeStruct((s, LANES), F32)
    return pl.pallas_call(
        _rope_table_kernel,
        grid=(s // ts,),
        in_specs=[pl.BlockSpec((ts, 1), lambda i: (i, 0)), row, row, row, row, row],
        out_specs=[tab, tab, tab, tab],
        out_shape=[out, out, out, out],
        compiler_params=_cparams("arbitrary"),
        name="rope_tables",
    )(pos, inv_r_full, sgn_r, inv_m_full, cmask, smask)


def _in_proj_kernel(x_ref, g_ref, wm_ref, wt_ref, zm_ref, zt_ref, h_sc, *, n_main):
    j = pl.program_id(1)

    @pl.when(j == 0)
    def _():
        h_sc[...] = _rms(x_ref[...], g_ref[...]).astype(BF16)

    @pl.when(j < n_main)
    def _():
        zm_ref[...] = jnp.dot(h_sc[...], wm_ref[...], preferred_element_type=F32).astype(zm_ref.dtype)

    @pl.when(j == n_main)
    def _():
        zt_ref[...] = jnp.dot(h_sc[...], wt_ref[...], preferred_element_type=F32).astype(zt_ref.dtype)


def _in_proj(x, gain, w_in_b, w_tail, l, tm):
    s, d = x.shape
    n_main = Z_MAIN // Z_TN
    last = n_main - 1
    return pl.pallas_call(
        functools.partial(_in_proj_kernel, n_main=n_main),
        grid=(s // tm, n_main + 1),
        in_specs=[pl.BlockSpec((tm, d), lambda i, j: (i, 0)),
                  _layer_spec(gain, l, 2),
                  pl.BlockSpec((None, d, Z_TN), lambda i, j: (l, 0, jnp.minimum(j, last))),
                  _layer_spec(w_tail, l, 2)],
        out_specs=[pl.BlockSpec((tm, Z_TN), lambda i, j: (i, jnp.minimum(j, last))),
                   pl.BlockSpec((tm, Z_TAIL), lambda i, j: (i, 0))],
        out_shape=[jax.ShapeDtypeStruct((s, Z_MAIN), BF16),
                   jax.ShapeDtypeStruct((s, Z_TAIL), BF16)],
        scratch_shapes=[pltpu.VMEM((tm, d), BF16)],
        compiler_params=_cparams("arbitrary", "arbitrary"),
        name="in_proj",
    )(x, gain, w_in_b, w_tail)


def _retention_kernel(q_ref, k_ref, v_ref, g_ref, cos_ref, sin_ref, inner_ref, qdec_ref, kdec_ref,
                      cdec_ref, gain_ref, o_ref, state_sc, *, chunk, n_chunks):
    @pl.when(pl.program_id(0) == 0)
    def _():
        state_sc[...] = jnp.zeros_like(state_sc)

    for c in range(n_chunks):
        rows = slice(c * chunk, (c + 1) * chunk)
        cos = cos_ref[rows, :]
        sin = sin_ref[rows, :]
        for h in range(N_RET_HEADS):
            cols = slice(h * HEAD_DIM, (h + 1) * HEAD_DIM)
            q = _rope(q_ref[rows, cols].astype(F32), cos, sin)
            k = _rope(k_ref[rows, cols].astype(F32), cos, sin) * (HEAD_DIM ** -0.5)
            v = v_ref[rows, cols]
            state = state_sc[h]
            scores = lax.dot_general(q.astype(BF16), k.astype(BF16), (((1,), (1,)), ((), ())),
                                     preferred_element_type=F32) * inner_ref[h]
            out = jnp.dot(scores.astype(BF16), v, preferred_element_type=F32)
            out += jnp.dot((q * qdec_ref[:, cols]).astype(BF16), state.astype(BF16),
                           preferred_element_type=F32)
            kt = (k * kdec_ref[:, cols]).T.astype(BF16)
            state_sc[h] = cdec_ref[:, cols] * state + jnp.dot(kt, v, preferred_element_type=F32)
            y = _rms(out, gain_ref[:, cols])
            gate = g_ref[rows, cols].astype(F32)
            o_ref[rows, cols] = (y * (gate / (1.0 + jnp.exp(-gate)))).astype(o_ref.dtype)


def _retention_tables(chunk):
    h = N_RET_HEADS
    log_gamma = jnp.log1p(-jnp.exp2(-5.0 - jnp.arange(h, dtype=F32)))
    idx = jnp.arange(chunk, dtype=F32)
    rel = idx[:, None] - idx[None, :]
    inner = jnp.where(rel >= 0, jnp.exp(log_gamma[:, None, None] * jnp.maximum(rel, 0.0)), 0.0)
    qdec = jnp.exp(log_gamma[None, :] * (idx[:, None] + 1.0))
    kdec = jnp.exp(log_gamma[None, :] * (chunk - 1.0 - idx[:, None]))
    cdec = jnp.exp(log_gamma * chunk)
    rep = lambda a: jnp.repeat(a, HEAD_DIM, axis=-1)
    return inner, rep(qdec), rep(kdec), rep(cdec[None, :])


def _retention_part(z, cos_r, sin_r, tables, mix, l, t, chunk):
    s = z.shape[0]
    zspec = lambda c: pl.BlockSpec((t, RET_WIDTH), lambda i: (i, c))
    tab = pl.BlockSpec((t, LANES), lambda i: (i, 0))
    full = lambda a: pl.BlockSpec(a.shape, lambda i: (0,) * a.ndim)
    return (functools.partial(_retention_kernel, chunk=chunk, n_chunks=t // chunk),
            [z, z, z, z, cos_r, sin_r, *tables, mix],
            [zspec(0), zspec(1), zspec(2), zspec(3), tab, tab] + [full(a) for a in tables]
            + [pl.BlockSpec((None, 1, RET_WIDTH), lambda i: (l, 0, 0))],
            [pl.BlockSpec((t, RET_WIDTH), lambda i: (i, 0))],
            [jax.ShapeDtypeStruct((s, RET_WIDTH), BF16)],
            [pltpu.VMEM((N_RET_HEADS, HEAD_DIM, HEAD_DIM), F32)])


def _mla_proj_kernel(cq_ref, ckv_ref, kpe_ref, gq_ref, gkv_ref, wuq_ref, wukv_ref, cos_ref, sin_ref,
                     qt_ref, k_ref, vt_ref):
    cos = cos_ref[...]
    sin = sin_ref[...]
    scale = QK_HEAD_DIM ** -0.5 * LOG2_E
    cq = _rms(cq_ref[...].astype(F32), gq_ref[...]).astype(BF16)
    mq = jnp.dot(cq, wuq_ref[...], preferred_element_type=F32)
    ckv = _rms(ckv_ref[...].astype(F32), gkv_ref[...]).astype(BF16)
    mkv = jnp.dot(ckv, wukv_ref[...], preferred_element_type=F32)
    kpe = _rope(kpe_ref[...].astype(F32), cos, sin).astype(BF16)
    for h in range(N_MLA_HEADS):
        nope = slice(h * QK_PAD, h * QK_PAD + LANES)
        pe = slice(h * QK_PAD + LANES, (h + 1) * QK_PAD)
        qt_ref[nope, :] = (mq[:, nope] * scale).T.astype(BF16)
        qt_ref[pe, :] = (_rope(mq[:, pe], cos, sin) * scale).T.astype(BF16)
        k_ref[:, nope] = mkv[:, h * LANES:(h + 1) * LANES].astype(BF16)
        k_ref[:, pe] = kpe
    vt = mkv[:, MLA_WIDTH:].T.astype(BF16)
    tm = vt.shape[1]
    extra = lax.broadcasted_iota(jnp.int32, (VT_ROWS - HEAD_DIM, tm), 0)
    ones_row = jnp.where(extra == 0, 1.0, 0.0).astype(BF16)
    for h in range(N_MLA_HEADS):
        vt_ref[h * VT_ROWS:h * VT_ROWS + HEAD_DIM, :] = vt[h * HEAD_DIM:(h + 1) * HEAD_DIM, :]
        vt_ref[h * VT_ROWS + HEAD_DIM:(h + 1) * VT_ROWS, :] = ones_row


def _mla_proj_part(z, zt, gq, gkv, wuq, wukv, cos_m, sin_m, l, tm):
    s = z.shape[0]
    tab = pl.BlockSpec((tm, LANES), lambda i: (i, 0))
    qk = N_MLA_HEADS * QK_PAD
    return (_mla_proj_kernel,
            [z, z, zt, gq, gkv, wuq, wukv, cos_m, sin_m],
            [pl.BlockSpec((tm, Q_LORA_RANK), lambda i: (i, Z_CQ // Q_LORA_RANK)),
             pl.BlockSpec((tm, KV_LORA_RANK), lambda i: (i, Z_CKV // KV_LORA_RANK)),
             pl.BlockSpec((tm, LANES), lambda i: (i, ZT_KPE // LANES)),
             _layer_spec(gq, l, 1), _layer_spec(gkv, l, 1),
             _layer_spec(wuq, l, 1), _layer_spec(wukv, l, 1), tab, tab],
            [pl.BlockSpec((qk, tm), lambda i: (0, i)),
             pl.BlockSpec((tm, qk), lambda i: (i, 0)),
             pl.BlockSpec((None, N_MLA_HEADS * VT_ROWS, tm), lambda i: (i, 0, 0))],
            [jax.ShapeDtypeStruct((qk, s), BF16),
             jax.ShapeDtypeStruct((s, qk), BF16),
             jax.ShapeDtypeStruct((s // tm, N_MLA_HEADS * VT_ROWS, tm), BF16)],
            [])


def _attn_kernel(qt_ref, k_ref, vt_ref, gain_ref, o_ref, m_sc, acc_sc, st0_sc, st1_sc, *, tq, hp):
    qi = pl.program_id(1)
    m_sc[...] = jnp.full_like(m_sc, -jnp.inf)
    acc_sc[...] = jnp.zeros_like(acc_sc)

    def scores(kb, st_sc):
        start = pl.multiple_of(kb * tq, tq)
        for j in range(hp):
            qk_cols = slice(j * QK_PAD, (j + 1) * QK_PAD)
            st_sc[j] = jnp.dot(k_ref[pl.ds(start, tq), qk_cols], qt_ref[qk_cols, :],
                               preferred_element_type=F32)

    def softmax_pv(kb, st_sc, masked):
        for j in range(hp):
            vt = vt_ref[kb, j * VT_ROWS:(j + 1) * VT_ROWS, :]
            st = st_sc[j]
            if masked:
                kv_pos = lax.broadcasted_iota(jnp.int32, st.shape, 0)
                q_pos = lax.broadcasted_iota(jnp.int32, st.shape, 1)
                st = jnp.where(kv_pos <= q_pos, st, -jnp.inf)
            m_prev = m_sc[j]
            m_new = jnp.maximum(m_prev, jnp.max(st, axis=0, keepdims=True))
            alpha = jnp.exp2(m_prev - m_new)
            p = jnp.exp2(st - m_new).astype(BF16)
            acc_sc[j] = alpha * acc_sc[j] + jnp.dot(vt, p, preferred_element_type=F32)
            m_sc[j] = m_new

    scores(0, st0_sc)

    def pair(i, carry):
        kb = 2 * i
        scores(kb + 1, st1_sc)
        softmax_pv(kb, st0_sc, False)
        scores(kb + 2, st0_sc)
        softmax_pv(kb + 1, st1_sc, False)
        return carry

    lax.fori_loop(0, qi // 2, pair, 0)

    @pl.when(qi % 2 == 0)
    def _():
        softmax_pv(qi, st0_sc, True)

    @pl.when(qi % 2 == 1)
    def _():
        scores(qi, st1_sc)
        softmax_pv(qi - 1, st0_sc, False)
        softmax_pv(qi, st1_sc, True)

    for j in range(hp):
        cols = slice(j * HEAD_DIM, (j + 1) * HEAD_DIM)
        denom = acc_sc[j, HEAD_DIM:HEAD_DIM + 1, :]
        o = (acc_sc[j, :HEAD_DIM, :] * (1.0 / denom)).T
        o_ref[:, cols] = _rms(o, gain_ref[:, cols]).astype(o_ref.dtype)


def _attention(qt, k, vt, mix, l, tq):
    s = k.shape[0]
    hp = ATTN_HEADS_PER_STEP
    width = hp * HEAD_DIM
    return pl.pallas_call(
        functools.partial(_attn_kernel, tq=tq, hp=hp),
        grid=(N_MLA_HEADS // hp, s // tq),
        in_specs=[pl.BlockSpec((hp * QK_PAD, tq), lambda h, i: (h, i)),
                  pl.BlockSpec((s, hp * QK_PAD), lambda h, i: (0, h)),
                  pl.BlockSpec((s // tq, hp * VT_ROWS, tq), lambda h, i: (0, h, 0)),
                  pl.BlockSpec((None, 1, width), lambda h, i: (l, 0, RET_WIDTH // width + h))],
        out_specs=pl.BlockSpec((tq, width), lambda h, i: (i, h)),
        out_shape=jax.ShapeDtypeStruct((s, MLA_WIDTH), BF16),
        scratch_shapes=[pltpu.VMEM((hp, 1, tq), F32),
                        pltpu.VMEM((hp, VT_ROWS, tq), F32),
                        pltpu.VMEM((hp, tq, tq), F32), pltpu.VMEM((hp, tq, tq), F32)],
        compiler_params=_cparams("arbitrary", "arbitrary"),
        name="mla_attention",
    )(qt, k, vt, mix)


def _gmlp_kernel(u_ref, v_ref, gv_ref, ws_ref, bs_ref, gm_ref, o_ref, *, n_chunks):
    row = lax.broadcasted_iota(jnp.int32, (CHUNK, CHUNK), 0)
    col = lax.broadcasted_iota(jnp.int32, (CHUNK, CHUNK), 1)
    for g in range(N_GMLP_GROUPS):
        cols = slice(g * HEAD_DIM, (g + 1) * HEAD_DIM)
        w = jnp.where(col <= row, ws_ref[g], 0.0).astype(BF16)
        bias = bs_ref[g]
        for c in range(n_chunks):
            rows = slice(c * CHUNK, (c + 1) * CHUNK)
            u = jax.nn.gelu(u_ref[rows, cols].astype(F32))
            v = jax.nn.gelu(v_ref[rows, cols].astype(F32))
            vn = _rms(v, gv_ref[:, cols]).astype(BF16)
            sg = jnp.dot(w, vn, preferred_element_type=F32) + bias
            o_ref[rows, cols] = _rms(u * sg, gm_ref[:, cols]).astype(o_ref.dtype)


def _gmlp_part(zt, gv, ws, bs, mix, l, t):
    s = zt.shape[0]
    return (functools.partial(_gmlp_kernel, n_chunks=t // CHUNK),
            [zt, zt, gv, ws, bs, mix],
            [pl.BlockSpec((t, GMLP_WIDTH), lambda i: (i, ZT_GU // GMLP_WIDTH)),
             pl.BlockSpec((t, GMLP_WIDTH), lambda i: (i, ZT_GV // GMLP_WIDTH)),
             _layer_spec(gv, l, 1), _layer_spec(ws, l, 1), _layer_spec(bs, l, 1),
             pl.BlockSpec((None, 1, GMLP_WIDTH),
                          lambda i: (l, 0, (RET_WIDTH + MLA_WIDTH) // GMLP_WIDTH))],
            [pl.BlockSpec((t, GMLP_WIDTH), lambda i: (i, 0))],
            [jax.ShapeDtypeStruct((s, GMLP_WIDTH), BF16)],
            [])


def _cast_kernel(w_ref, o_ref):
    o_ref[...] = w_ref[...].astype(o_ref.dtype)


def _cast_part(w, l, n_tiles):
    _, r, c = w.shape
    assert r % n_tiles == 0
    rb = r // n_tiles
    return (_cast_kernel, [w],
            [pl.BlockSpec((None, rb, c), lambda i: (l, i, 0))],
            [pl.BlockSpec((rb, c), lambda i: (i, 0))],
            [jax.ShapeDtypeStruct((r, c), BF16)],
            [])


def _mixers_kernel(*refs, bodies, n_in, n_out, n_scratch):
    ins, outs, scr = [], [], []
    pos = 0
    for group, counts in ((ins, n_in), (outs, n_out), (scr, n_scratch)):
        for c in counts:
            group.append(refs[pos:pos + c])
            pos += c
    for body, i, o, sc in zip(bodies, ins, outs, scr):
        body(*i, *o, *sc)


def _mixers(parts, n_tiles):
    bodies = [p[0] for p in parts]
    flat = lambda k: [a for p in parts for a in p[k]]
    return pl.pallas_call(
        functools.partial(_mixers_kernel, bodies=bodies, n_in=[len(p[1]) for p in parts],
                          n_out=[len(p[3]) for p in parts], n_scratch=[len(p[5]) for p in parts]),
        grid=(n_tiles,),
        in_specs=flat(2), out_specs=flat(3), out_shape=flat(4), scratch_shapes=flat(5),
        compiler_params=_cparams("arbitrary"),
        name="mixers",
    )(*flat(1))


def _out_proj_kernel(x_ref, yr_ref, ym_ref, yg_ref, w_ref, o_ref):
    y = jnp.concatenate([yr_ref[...], ym_ref[...], yg_ref[...]], axis=-1)
    o_ref[...] = x_ref[...] + jnp.dot(y, w_ref[...], preferred_element_type=F32)


def _out_proj(x, yr, ym, yg, w, tm):
    s, d = x.shape
    blk = lambda n: pl.BlockSpec((tm, n), lambda i: (i, 0))
    return pl.pallas_call(
        _out_proj_kernel,
        grid=(s // tm,),
        in_specs=[blk(d), blk(RET_WIDTH), blk(MLA_WIDTH), blk(GMLP_WIDTH),
                  pl.BlockSpec(w.shape, lambda i: (0, 0))],
        out_specs=blk(d),
        out_shape=jax.ShapeDtypeStruct((s, d), F32),
        compiler_params=_cparams("arbitrary"),
        name="out_proj",
    )(x, yr, ym, yg, w)


def _ffn_kernel(x_ref, g_ref, wg_ref, wu_ref, wd_ref, fn_ref, o_ref, h_sc, *, final_norm):
    f = pl.program_id(1)

    @pl.when(f == 0)
    def _():
        x = x_ref[...]
        h_sc[...] = _rms(x, g_ref[...]).astype(BF16)
        o_ref[...] = x

    h = h_sc[...]
    gate = jnp.dot(h, wg_ref[...], preferred_element_type=F32)
    up = jnp.dot(h, wu_ref[...], preferred_element_type=F32)
    act = (gate / (1.0 + jnp.exp(-gate)) * up).astype(BF16)
    o_ref[...] += jnp.dot(act, wd_ref[...], preferred_element_type=F32)

    if final_norm:
        @pl.when(f == pl.num_programs(1) - 1)
        def _():
            o_ref[...] = _rms(o_ref[...], fn_ref[...])


def _ffn_cast_kernel(x_ref, g_ref, wg_ref, wu_ref, wd_ref, fn_ref, ng_ref, nu_ref, nd_ref,
                     o_ref, cg_ref, cu_ref, cd_ref, h_sc):
    cg_ref[...] = ng_ref[...].astype(BF16)
    cu_ref[...] = nu_ref[...].astype(BF16)
    cd_ref[...] = nd_ref[...].astype(BF16)
    _ffn_kernel(x_ref, g_ref, wg_ref, wu_ref, wd_ref, fn_ref, o_ref, h_sc, final_norm=False)


def _ffn(x, gain, wg, wu, wd, fn_gain, l, tm, tf, final_norm, next_f32=None):
    s, d = x.shape
    dff = wg.shape[1]
    ni, nf = s // tm, dff // tf
    in_specs = [pl.BlockSpec((tm, d), lambda i, f: (i, 0)),
                _layer_spec(gain, l, 2),
                pl.BlockSpec((d, tf), lambda i, f: (0, f)),
                pl.BlockSpec((d, tf), lambda i, f: (0, f)),
                pl.BlockSpec((tf, d), lambda i, f: (f, 0)),
                pl.BlockSpec((1, d), lambda i, f: (0, 0))]
    x_spec = pl.BlockSpec((tm, d), lambda i, f: (i, 0))
    x_shape = jax.ShapeDtypeStruct((s, d), F32)
    scratch = [pltpu.VMEM((tm, d), BF16)]
    params = _cparams("arbitrary", "arbitrary")
    if next_f32 is None:
        return pl.pallas_call(
            functools.partial(_ffn_kernel, final_norm=final_norm),
            grid=(ni, nf), in_specs=in_specs, out_specs=x_spec, out_shape=x_shape,
            scratch_shapes=scratch, compiler_params=params, name="ffn",
        )(x, gain, wg, wu, wd, fn_gain)
    assert not final_norm and d % ni == 0 and dff % nf == 0
    rb, cb = d // ni, dff // nf
    in_specs += [pl.BlockSpec((None, rb, cb), lambda i, f: (l + 1, i, f)),
                 pl.BlockSpec((None, rb, cb), lambda i, f: (l + 1, i, f)),
                 pl.BlockSpec((None, cb, rb), lambda i, f: (l + 1, f, i))]
    out_specs = [x_spec,
                 pl.BlockSpec((rb, cb), lambda i, f: (i, f)),
                 pl.BlockSpec((rb, cb), lambda i, f: (i, f)),
                 pl.BlockSpec((cb, rb), lambda i, f: (f, i))]
    out_shape = [x_shape, jax.ShapeDtypeStruct((d, dff), BF16), jax.ShapeDtypeStruct((d, dff), BF16),
                 jax.ShapeDtypeStruct((dff, d), BF16)]
    return pl.pallas_call(
        _ffn_cast_kernel,
        grid=(ni, nf), in_specs=in_specs, out_specs=out_specs, out_shape=out_shape,
        scratch_shapes=scratch, name="ffn_cast",
        compiler_params=_cparams("arbitrary", "arbitrary", vmem_limit=VMEM_LIMIT_FFN_CAST),
    )(x, gain, wg, wu, wd, fn_gain, *next_f32)


def _spread_rope_cols(w):
    z = jnp.zeros(w.shape[:-1] + (32,), w.dtype)
    return jnp.concatenate([w[..., :32], z, w[..., 32:], z], axis=-1)


def _prep_w_in(w_in):
    w_in_b = w_in.astype(BF16)
    kpe = _spread_rope_cols(w_in_b[..., Z_MAIN:Z_MAIN + QK_ROPE_DIM])
    tail = jnp.concatenate([w_in_b[..., Z_MAIN + QK_ROPE_DIM:], kpe], axis=-1)
    return w_in_b, tail


def _prep_w_uq(w_uq):
    l, r, _ = w_uq.shape
    w = w_uq.reshape(l, r, N_MLA_HEADS, QK_HEAD_DIM)
    w = jnp.concatenate([w[..., :QK_NOPE_DIM], _spread_rope_cols(w[..., QK_NOPE_DIM:])], axis=-1)
    return w.reshape(l, r, N_MLA_HEADS * QK_PAD).astype(BF16)


def _prep_w_ukv(w_ukv):
    l, r, _ = w_ukv.shape
    w = w_ukv.reshape(l, r, N_MLA_HEADS, 2 * HEAD_DIM)
    k_nope = w[..., :HEAD_DIM].reshape(l, r, MLA_WIDTH)
    v = w[..., HEAD_DIM:].reshape(l, r, MLA_WIDTH)
    return jnp.concatenate([k_nope, v], axis=-1).astype(BF16)


def _tiles(s):
    pick = lambda pref: pref if s % pref == 0 else s
    attn = pick(512)
    return dict(in_tm=pick(1024), mix_tm=attn, ret_chunk=128, attn_tq=attn,
                out_tm=pick(512), ffn_tm=pick(1024), ffn_tf=512)


def kernel(x, positions, attn_norm, w_in, mla_q_norm, w_uq, mla_kv_norm, w_ukv, gmlp_v_norm, gmlp_w_s,
           gmlp_b_s, mix_norm, w_out, ffn_norm, w_gate, w_up, w_down, final_norm):
    b, s, d = x.shape
    assert b == 1, "batch is folded away; only BATCH == 1 is supported"
    depth = w_in.shape[0]
    t = _tiles(s)

    w_in_b, w_in_tail = _prep_w_in(w_in)
    w_uq_b = _prep_w_uq(w_uq)
    w_ukv_b = _prep_w_ukv(w_ukv)
    ffn_w = [w_gate[0].astype(BF16), w_up[0].astype(BF16), None]
    rows = lambda a: a.reshape(depth, 1, -1)
    attn_g, q_g, kv_g, gv_g, mix_g, ffn_g = map(
        rows, (attn_norm, mla_q_norm, mla_kv_norm, gmlp_v_norm, mix_norm, ffn_norm))
    b_s = gmlp_b_s[..., None]
    fn_g = final_norm.reshape(1, d)

    cos_r, sin_r, cos_m, sin_m = _rope_tables(positions)
    ret_tables = _retention_tables(t["ret_chunk"])
    xs = x.reshape(s, d)
    for l in range(depth):
        z, zt = _in_proj(xs, attn_g, w_in_b, w_in_tail, l, t["in_tm"])
        tm = t["mix_tm"]
        parts = [_retention_part(z, cos_r, sin_r, ret_tables, mix_g, l, tm, t["ret_chunk"]),
                 _mla_proj_part(z, zt, q_g, kv_g, w_uq_b, w_ukv_b, cos_m, sin_m, l, tm),
                 _gmlp_part(zt, gv_g, gmlp_w_s, b_s, mix_g, l, tm),
                 _cast_part(w_out, l, s // tm)]
        if l == 0:
            parts.append(_cast_part(w_down, 0, s // tm))
        y_ret, qt, k, vt, y_gm, w_out_l, *w_down_0 = _mixers(parts, s // tm)
        if l == 0:
            ffn_w[2] = w_down_0[0]
        y_mla = _attention(qt, k, vt, mix_g, l, t["attn_tq"])
        xs = _out_proj(xs, y_ret, y_mla, y_gm, w_out_l, t["out_tm"])
        if l < depth - 1:
            xs, *ffn_w = _ffn(xs, ffn_g, *ffn_w, fn_g, l, t["ffn_tm"], t["ffn_tf"], final_norm=False,
                              next_f32=(w_gate, w_up, w_down))
        else:
            xs = _ffn(xs, ffn_g, *ffn_w, fn_g, l, t["ffn_tm"], t["ffn_tf"], final_norm=True)
    return xs.reshape(b, s, d)
```

```python
import functools

import jax
import jax.numpy as jnp
from jax import lax
from jax.experimental import pallas as pl
from jax.experimental.pallas import tpu as pltpu

F32 = jnp.float32
BF16 = jnp.bfloat16

HEAD_DIM = 128
N_RET_HEADS = 6
N_MLA_HEADS = 6
N_GMLP_GROUPS = 4
RET_WIDTH = N_RET_HEADS * HEAD_DIM
MLA_WIDTH = N_MLA_HEADS * HEAD_DIM
GMLP_WIDTH = N_GMLP_GROUPS * HEAD_DIM
Q_LORA_RANK = 512
KV_LORA_RANK = 512
QK_NOPE_DIM = 128
QK_ROPE_DIM = 64
QK_HEAD_DIM = QK_NOPE_DIM + QK_ROPE_DIM
CHUNK = 128
ROPE_BASE = 10000.0
EPS = 1e-6

LANES = 128
QK_PAD = 2 * LANES
ATTN_HEADS_PER_STEP = 3
VT_ROWS = HEAD_DIM + 16
LOG2_E = 1.4426950408889634

Z_CQ = 4 * RET_WIDTH
Z_CKV = Z_CQ + Q_LORA_RANK
Z_MAIN = Z_CKV + KV_LORA_RANK
Z_TN = 1024
ZT_GU = 0
ZT_GV = ZT_GU + GMLP_WIDTH
ZT_KPE = ZT_GV + GMLP_WIDTH
Z_TAIL = ZT_KPE + LANES

VMEM_LIMIT = 56 * 1024 * 1024
VMEM_LIMIT_FFN_CAST = 60 * 1024 * 1024


def _cparams(*sem, vmem_limit=VMEM_LIMIT):
    return pltpu.CompilerParams(dimension_semantics=sem, vmem_limit_bytes=vmem_limit)


def _layer_spec(arr, l, ngrid):
    zeros = (0,) * (arr.ndim - 1)
    return pl.BlockSpec((None,) + arr.shape[1:], lambda *_: (l,) + zeros)


def _rms(x, gain):
    return x * lax.rsqrt(jnp.mean(x * x, axis=-1, keepdims=True) + EPS) * gain


def _rope(x, cos, sin):
    return x * cos + pltpu.roll(x, 64, 1) * sin


def _rope_table_kernel(pos_ref, inv_r_ref, sgn_r_ref, inv_m_ref, cm_ref, sm_ref,
                       cos_r_ref, sin_r_ref, cos_m_ref, sin_m_ref):
    pos = pos_ref[...].astype(F32)
    ang_r = pos * inv_r_ref[...]
    cos_r_ref[...] = jnp.cos(ang_r)
    sin_r_ref[...] = jnp.sin(ang_r) * sgn_r_ref[...]
    ang_m = pos * inv_m_ref[...]
    cos_m_ref[...] = jnp.cos(ang_m) * cm_ref[...]
    sin_m_ref[...] = jnp.sin(ang_m) * sm_ref[...]


def _rope_tables_part(positions, n_tiles):
    s = positions.shape[-1]
    ts = s // n_tiles
    pos = positions.reshape(s, 1)
    inv_r = 1.0 / (ROPE_BASE ** (jnp.arange(0, HEAD_DIM, 2, dtype=F32) / HEAD_DIM))
    inv_m = 1.0 / (ROPE_BASE ** (jnp.arange(0, QK_ROPE_DIM, 2, dtype=F32) / QK_ROPE_DIM))
    z32 = jnp.zeros((32,), F32)
    o32 = jnp.ones((32,), F32)
    inv_r_full = jnp.concatenate([inv_r, inv_r]).reshape(1, LANES)
    sgn_r = jnp.concatenate([-jnp.ones((64,), F32), jnp.ones((64,), F32)]).reshape(1, LANES)
    inv_m_full = jnp.concatenate([inv_m, z32, inv_m, z32]).reshape(1, LANES)
    cmask = jnp.concatenate([o32, z32, o32, z32]).reshape(1, LANES)
    smask = jnp.concatenate([-o32, z32, o32, z32]).reshape(1, LANES)
    row = pl.BlockSpec((1, LANES), lambda i: (0, 0))
    tab = pl.BlockSpec((ts, LANES), lambda i: (i, 0))
    out = jax.ShapeDtypeStruct((s, LANES), F32)
    return (_rope_table_kernel,
            [pos, inv_r_full, sgn_r, inv_m_full, cmask, smask],
            [pl.BlockSpec((ts, 1), lambda i: (i, 0)), row, row, row, row, row],
            [tab, tab, tab, tab],
            [out, out, out, out],
            [])


def _in_proj_kernel(x_ref, g_ref, wm_ref, wt_ref, zm_ref, zt_ref, h_sc, *, n_main):
    j = pl.program_id(1)

    @pl.when(j == 0)
    def _():
        h_sc[...] = _rms(x_ref[...], g_ref[...]).astype(BF16)

    @pl.when(j < n_main)
    def _():
        zm_ref[...] = jnp.dot(h_sc[...], wm_ref[...], preferred_element_type=F32).astype(zm_ref.dtype)

    @pl.when(j == n_main)
    def _():
        zt_ref[...] = jnp.dot(h_sc[...], wt_ref[...], preferred_element_type=F32).astype(zt_ref.dtype)


def _in_proj(x, gain, w_in_b, w_tail, l, tm):
    s, d = x.shape
    n_main = Z_MAIN // Z_TN
    last = n_main - 1
    return pl.pallas_call(
        functools.partial(_in_proj_kernel, n_main=n_main),
        grid=(s // tm, n_main + 1),
        in_specs=[pl.BlockSpec((tm, d), lambda i, j: (i, 0)),
                  _layer_spec(gain, l, 2),
                  pl.BlockSpec((None, d, Z_TN), lambda i, j: (l, 0, jnp.minimum(j, last))),
                  _layer_spec(w_tail, l, 2)],
        out_specs=[pl.BlockSpec((tm, Z_TN), lambda i, j: (i, jnp.minimum(j, last))),
                   pl.BlockSpec((tm, Z_TAIL), lambda i, j: (i, 0))],
        out_shape=[jax.ShapeDtypeStruct((s, Z_MAIN), BF16),
                   jax.ShapeDtypeStruct((s, Z_TAIL), BF16)],
        scratch_shapes=[pltpu.VMEM((tm, d), BF16)],
        compiler_params=_cparams("arbitrary", "arbitrary"),
        name="in_proj",
    )(x, gain, w_in_b, w_tail)


def _retention_kernel(q_ref, k_ref, v_ref, g_ref, cos_ref, sin_ref, inner_ref, qdec_ref, kdec_ref,
                      cdec_ref, gain_ref, o_ref, state_sc, *, chunk, n_chunks):
    @pl.when(pl.program_id(0) == 0)
    def _():
        state_sc[...] = jnp.zeros_like(state_sc)

    for c in range(n_chunks):
        rows = slice(c * chunk, (c + 1) * chunk)
        cos = cos_ref[rows, :]
        sin = sin_ref[rows, :]
        for h in range(N_RET_HEADS):
            cols = slice(h * HEAD_DIM, (h + 1) * HEAD_DIM)
            q = _rope(q_ref[rows, cols].astype(F32), cos, sin)
            k = _rope(k_ref[rows, cols].astype(F32), cos, sin) * (HEAD_DIM ** -0.5)
            v = v_ref[rows, cols]
            state = state_sc[h]
            scores = lax.dot_general(q.astype(BF16), k.astype(BF16), (((1,), (1,)), ((), ())),
                                     preferred_element_type=F32) * inner_ref[h]
            out = jnp.dot(scores.astype(BF16), v, preferred_element_type=F32)
            out += jnp.dot((q * qdec_ref[:, cols]).astype(BF16), state.astype(BF16),
                           preferred_element_type=F32)
            kt = (k * kdec_ref[:, cols]).T.astype(BF16)
            state_sc[h] = cdec_ref[:, cols] * state + jnp.dot(kt, v, preferred_element_type=F32)
            y = _rms(out, gain_ref[:, cols])
            gate = g_ref[rows, cols].astype(F32)
            o_ref[rows, cols] = (y * (gate / (1.0 + jnp.exp(-gate)))).astype(o_ref.dtype)


def _retention_tables(chunk):
    h = N_RET_HEADS
    log_gamma = jnp.log1p(-jnp.exp2(-5.0 - jnp.arange(h, dtype=F32)))
    idx = jnp.arange(chunk, dtype=F32)
    rel = idx[:, None] - idx[None, :]
    inner = jnp.where(rel >= 0, jnp.exp(log_gamma[:, None, None] * jnp.maximum(rel, 0.0)), 0.0)
    qdec = jnp.exp(log_gamma[None, :] * (idx[:, None] + 1.0))
    kdec = jnp.exp(log_gamma[None, :] * (chunk - 1.0 - idx[:, None]))
    cdec = jnp.exp(log_gamma * chunk)
    rep = lambda a: jnp.repeat(a, HEAD_DIM, axis=-1)
    return inner, rep(qdec), rep(kdec), rep(cdec[None, :])


def _retention_part(z, cos_r, sin_r, tables, mix, l, t, chunk):
    s = z.shape[0]
    zspec = lambda c: pl.BlockSpec((t, RET_WIDTH), lambda i: (i, c))
    tab = pl.BlockSpec((t, LANES), lambda i: (i, 0))
    full = lambda a: pl.BlockSpec(a.shape, lambda i: (0,) * a.ndim)
    return (functools.partial(_retention_kernel, chunk=chunk, n_chunks=t // chunk),
            [z, z, z, z, cos_r, sin_r, *tables, mix],
            [zspec(0), zspec(1), zspec(2), zspec(3), tab, tab] + [full(a) for a in tables]
            + [pl.BlockSpec((None, 1, RET_WIDTH), lambda i: (l, 0, 0))],
            [pl.BlockSpec((t, RET_WIDTH), lambda i: (i, 0))],
            [jax.ShapeDtypeStruct((s, RET_WIDTH), BF16)],
            [pltpu.VMEM((N_RET_HEADS, HEAD_DIM, HEAD_DIM), F32)])


def _mla_proj_kernel(cq_ref, ckv_ref, kpe_ref, gq_ref, gkv_ref, wuq_ref, wukv_ref, cos_ref, sin_ref,
                     qt_ref, k_ref, vt_ref):
    cos = cos_ref[...]
    sin = sin_ref[...]
    scale = QK_HEAD_DIM ** -0.5 * LOG2_E
    cq = _rms(cq_ref[...].astype(F32), gq_ref[...]).astype(BF16)
    mq = jnp.dot(cq, wuq_ref[...], preferred_element_type=F32)
    ckv = _rms(ckv_ref[...].astype(F32), gkv_ref[...]).astype(BF16)
    mkv = jnp.dot(ckv, wukv_ref[...], preferred_element_type=F32)
    kpe = _rope(kpe_ref[...].astype(F32), cos, sin).astype(BF16)
    for h in range(N_MLA_HEADS):
        nope = slice(h * QK_PAD, h * QK_PAD + LANES)
        pe = slice(h * QK_PAD + LANES, (h + 1) * QK_PAD)
        qt_ref[nope, :] = (mq[:, nope] * scale).T.astype(BF16)
        qt_ref[pe, :] = (_rope(mq[:, pe], cos, sin) * scale).T.astype(BF16)
        k_ref[:, nope] = mkv[:, h * LANES:(h + 1) * LANES].astype(BF16)
        k_ref[:, pe] = kpe
    vt = mkv[:, MLA_WIDTH:].T.astype(BF16)
    tm = vt.shape[1]
    extra = lax.broadcasted_iota(jnp.int32, (VT_ROWS - HEAD_DIM, tm), 0)
    ones_row = jnp.where(extra == 0, 1.0, 0.0).astype(BF16)
    for h in range(N_MLA_HEADS):
        vt_ref[h * VT_ROWS:h * VT_ROWS + HEAD_DIM, :] = vt[h * HEAD_DIM:(h + 1) * HEAD_DIM, :]
        vt_ref[h * VT_ROWS + HEAD_DIM:(h + 1) * VT_ROWS, :] = ones_row


def _mla_proj_part(z, zt, gq, gkv, wuq, wukv, cos_m, sin_m, l, tm):
    s = z.shape[0]
    tab = pl.BlockSpec((tm, LANES), lambda i: (i, 0))
    qk = N_MLA_HEADS * QK_PAD
    return (_mla_proj_kernel,
            [z, z, zt, gq, gkv, wuq, wukv, cos_m, sin_m],
            [pl.BlockSpec((tm, Q_LORA_RANK), lambda i: (i, Z_CQ // Q_LORA_RANK)),
             pl.BlockSpec((tm, KV_LORA_RANK), lambda i: (i, Z_CKV // KV_LORA_RANK)),
             pl.BlockSpec((tm, LANES), lambda i: (i, ZT_KPE // LANES)),
             _layer_spec(gq, l, 1), _layer_spec(gkv, l, 1),
             _layer_spec(wuq, l, 1), _layer_spec(wukv, l, 1), tab, tab],
            [pl.BlockSpec((qk, tm), lambda i: (0, i)),
             pl.BlockSpec((tm, qk), lambda i: (i, 0)),
             pl.BlockSpec((None, N_MLA_HEADS * VT_ROWS, tm), lambda i: (i, 0, 0))],
            [jax.ShapeDtypeStruct((qk, s), BF16),
             jax.ShapeDtypeStruct((s, qk), BF16),
             jax.ShapeDtypeStruct((s // tm, N_MLA_HEADS * VT_ROWS, tm), BF16)],
            [])


def _attn_kernel(qt_ref, k_ref, vt_ref, gain_ref, o_ref, m_sc, acc_sc, st0_sc, st1_sc, *, tq, hp):
    qi = pl.program_id(1)
    m_sc[...] = jnp.full_like(m_sc, -jnp.inf)
    acc_sc[...] = jnp.zeros_like(acc_sc)

    def scores(kb, st_sc):
        start = pl.multiple_of(kb * tq, tq)
        for j in range(hp):
            qk_cols = slice(j * QK_PAD, (j + 1) * QK_PAD)
            st_sc[j] = jnp.dot(k_ref[pl.ds(start, tq), qk_cols], qt_ref[qk_cols, :],
                               preferred_element_type=F32)

    def softmax_pv(kb, st_sc, masked):
        for j in range(hp):
            vt = vt_ref[kb, j * VT_ROWS:(j + 1) * VT_ROWS, :]
            st = st_sc[j]
            if masked:
                kv_pos = lax.broadcasted_iota(jnp.int32, st.shape, 0)
                q_pos = lax.broadcasted_iota(jnp.int32, st.shape, 1)
                st = jnp.where(kv_pos <= q_pos, st, -jnp.inf)
            m_prev = m_sc[j]
            m_new = jnp.maximum(m_prev, jnp.max(st, axis=0, keepdims=True))
            alpha = jnp.exp2(m_prev - m_new)
            p = jnp.exp2(st - m_new).astype(BF16)
            acc_sc[j] = alpha * acc_sc[j] + jnp.dot(vt, p, preferred_element_type=F32)
            m_sc[j] = m_new

    scores(0, st0_sc)

    def pair(i, carry):
        kb = 2 * i
        scores(kb + 1, st1_sc)
        softmax_pv(kb, st0_sc, False)
        scores(kb + 2, st0_sc)
        softmax_pv(kb + 1, st1_sc, False)
        return carry

    lax.fori_loop(0, qi // 2, pair, 0)

    @pl.when(qi % 2 == 0)
    def _():
        softmax_pv(qi, st0_sc, True)

    @pl.when(qi % 2 == 1)
    def _():
        scores(qi, st1_sc)
        softmax_pv(qi - 1, st0_sc, False)
        softmax_pv(qi, st1_sc, True)

    for j in range(hp):
        cols = slice(j * HEAD_DIM, (j + 1) * HEAD_DIM)
        denom = acc_sc[j, HEAD_DIM:HEAD_DIM + 1, :]
        o = (acc_sc[j, :HEAD_DIM, :] * (1.0 / denom)).T
        o_ref[:, cols] = _rms(o, gain_ref[:, cols]).astype(o_ref.dtype)


def _attention(qt, k, vt, mix, l, tq):
    s = k.shape[0]
    hp = ATTN_HEADS_PER_STEP
    width = hp * HEAD_DIM
    return pl.pallas_call(
        functools.partial(_attn_kernel, tq=tq, hp=hp),
        grid=(N_MLA_HEADS // hp, s // tq),
        in_specs=[pl.BlockSpec((hp * QK_PAD, tq), lambda h, i: (h, i)),
                  pl.BlockSpec((s, hp * QK_PAD), lambda h, i: (0, h)),
                  pl.BlockSpec((s // tq, hp * VT_ROWS, tq), lambda h, i: (0, h, 0)),
                  pl.BlockSpec((None, 1, width), lambda h, i: (l, 0, RET_WIDTH // width + h))],
        out_specs=pl.BlockSpec((tq, width), lambda h, i: (i, h)),
        out_shape=jax.ShapeDtypeStruct((s, MLA_WIDTH), BF16),
        scratch_shapes=[pltpu.VMEM((hp, 1, tq), F32),
                        pltpu.VMEM((hp, VT_ROWS, tq), F32),
                        pltpu.VMEM((hp, tq, tq), F32), pltpu.VMEM((hp, tq, tq), F32)],
        compiler_params=_cparams("arbitrary", "arbitrary"),
        name="mla_attention",
    )(qt, k, vt, mix)


def _gmlp_kernel(u_ref, v_ref, gv_ref, ws_ref, bs_ref, gm_ref, o_ref, *, n_chunks):
    row = lax.broadcasted_iota(jnp.int32, (CHUNK, CHUNK), 0)
    col = lax.broadcasted_iota(jnp.int32, (CHUNK, CHUNK), 1)
    for g in range(N_GMLP_GROUPS):
        cols = slice(g * HEAD_DIM, (g + 1) * HEAD_DIM)
        w = jnp.where(col <= row, ws_ref[g], 0.0).astype(BF16)
        bias = bs_ref[g]
        for c in range(n_chunks):
            rows = slice(c * CHUNK, (c + 1) * CHUNK)
            u = jax.nn.gelu(u_ref[rows, cols].astype(F32))
            v = jax.nn.gelu(v_ref[rows, cols].astype(F32))
            vn = _rms(v, gv_ref[:, cols]).astype(BF16)
            sg = jnp.dot(w, vn, preferred_element_type=F32) + bias
            o_ref[rows, cols] = _rms(u * sg, gm_ref[:, cols]).astype(o_ref.dtype)


def _gmlp_part(zt, gv, ws, bs, mix, l, t):
    s = zt.shape[0]
    return (functools.partial(_gmlp_kernel, n_chunks=t // CHUNK),
            [zt, zt, gv, ws, bs, mix],
            [pl.BlockSpec((t, GMLP_WIDTH), lambda i: (i, ZT_GU // GMLP_WIDTH)),
             pl.BlockSpec((t, GMLP_WIDTH), lambda i: (i, ZT_GV // GMLP_WIDTH)),
             _layer_spec(gv, l, 1), _layer_spec(ws, l, 1), _layer_spec(bs, l, 1),
             pl.BlockSpec((None, 1, GMLP_WIDTH),
                          lambda i: (l, 0, (RET_WIDTH + MLA_WIDTH) // GMLP_WIDTH))],
            [pl.BlockSpec((t, GMLP_WIDTH), lambda i: (i, 0))],
            [jax.ShapeDtypeStruct((s, GMLP_WIDTH), BF16)],
            [])


def _cast_kernel(w_ref, o_ref):
    o_ref[...] = w_ref[...].astype(o_ref.dtype)


def _cast_part(w, l, n_tiles):
    _, r, c = w.shape
    assert r % n_tiles == 0
    rb = r // n_tiles
    return (_cast_kernel, [w],
            [pl.BlockSpec((None, rb, c), lambda i: (l, i, 0))],
            [pl.BlockSpec((rb, c), lambda i: (i, 0))],
            [jax.ShapeDtypeStruct((r, c), BF16)],
            [])


def _mixers_kernel(*refs, bodies, n_in, n_out, n_scratch):
    ins, outs, scr = [], [], []
    pos = 0
    for group, counts in ((ins, n_in), (outs, n_out), (scr, n_scratch)):
        for c in counts:
            group.append(refs[pos:pos + c])
            pos += c
    for body, i, o, sc in zip(bodies, ins, outs, scr):
        body(*i, *o, *sc)


def _mixers(parts, n_tiles, name="mixers"):
    bodies = [p[0] for p in parts]
    flat = lambda k: [a for p in parts for a in p[k]]
    return pl.pallas_call(
        functools.partial(_mixers_kernel, bodies=bodies, n_in=[len(p[1]) for p in parts],
                          n_out=[len(p[3]) for p in parts], n_scratch=[len(p[5]) for p in parts]),
        grid=(n_tiles,),
        in_specs=flat(2), out_specs=flat(3), out_shape=flat(4), scratch_shapes=flat(5),
        compiler_params=_cparams("arbitrary"),
        name=name,
    )(*flat(1))


def _out_proj_kernel(x_ref, yr_ref, ym_ref, yg_ref, w_ref, o_ref):
    y = jnp.concatenate([yr_ref[...], ym_ref[...], yg_ref[...]], axis=-1)
    o_ref[...] = x_ref[...] + jnp.dot(y, w_ref[...], preferred_element_type=F32)


def _out_proj(x, yr, ym, yg, w, tm):
    s, d = x.shape
    blk = lambda n: pl.BlockSpec((tm, n), lambda i: (i, 0))
    return pl.pallas_call(
        _out_proj_kernel,
        grid=(s // tm,),
        in_specs=[blk(d), blk(RET_WIDTH), blk(MLA_WIDTH), blk(GMLP_WIDTH),
                  pl.BlockSpec(w.shape, lambda i: (0, 0))],
        out_specs=blk(d),
        out_shape=jax.ShapeDtypeStruct((s, d), F32),
        compiler_params=_cparams("arbitrary"),
        name="out_proj",
    )(x, yr, ym, yg, w)


def _ffn_kernel(x_ref, g_ref, wg_ref, wu_ref, wd_ref, fn_ref, o_ref, h_sc, *, final_norm):
    f = pl.program_id(1)

    @pl.when(f == 0)
    def _():
        x = x_ref[...]
        h_sc[...] = _rms(x, g_ref[...]).astype(BF16)
        o_ref[...] = x

    h = h_sc[...]
    gate = jnp.dot(h, wg_ref[...], preferred_element_type=F32)
    up = jnp.dot(h, wu_ref[...], preferred_element_type=F32)
    act = (gate / (1.0 + jnp.exp(-gate)) * up).astype(BF16)
    o_ref[...] += jnp.dot(act, wd_ref[...], preferred_element_type=F32)

    if final_norm:
        @pl.when(f == pl.num_programs(1) - 1)
        def _():
            o_ref[...] = _rms(o_ref[...], fn_ref[...])


def _ffn_cast_kernel(x_ref, g_ref, wg_ref, wu_ref, wd_ref, fn_ref, ng_ref, nu_ref, nd_ref,
                     o_ref, cg_ref, cu_ref, cd_ref, h_sc):
    cg_ref[...] = ng_ref[...].astype(BF16)
    cu_ref[...] = nu_ref[...].astype(BF16)
    cd_ref[...] = nd_ref[...].astype(BF16)
    _ffn_kernel(x_ref, g_ref, wg_ref, wu_ref, wd_ref, fn_ref, o_ref, h_sc, final_norm=False)


def _ffn(x, gain, wg, wu, wd, fn_gain, l, tm, tf, final_norm, next_f32=None):
    s, d = x.shape
    dff = wg.shape[1]
    ni, nf = s // tm, dff // tf
    in_specs = [pl.BlockSpec((tm, d), lambda i, f: (i, 0)),
                _layer_spec(gain, l, 2),
                pl.BlockSpec((d, tf), lambda i, f: (0, f)),
                pl.BlockSpec((d, tf), lambda i, f: (0, f)),
                pl.BlockSpec((tf, d), lambda i, f: (f, 0)),
                pl.BlockSpec((1, d), lambda i, f: (0, 0))]
    x_spec = pl.BlockSpec((tm, d), lambda i, f: (i, 0))
    x_shape = jax.ShapeDtypeStruct((s, d), F32)
    scratch = [pltpu.VMEM((tm, d), BF16)]
    params = _cparams("arbitrary", "arbitrary")
    if next_f32 is None:
        return pl.pallas_call(
            functools.partial(_ffn_kernel, final_norm=final_norm),
            grid=(ni, nf), in_specs=in_specs, out_specs=x_spec, out_shape=x_shape,
            scratch_shapes=scratch, compiler_params=params, name="ffn",
        )(x, gain, wg, wu, wd, fn_gain)
    assert not final_norm and d % ni == 0 and dff % nf == 0
    rb, cb = d // ni, dff // nf
    in_specs += [pl.BlockSpec((None, rb, cb), lambda i, f: (l + 1, i, f)),
                 pl.BlockSpec((None, rb, cb), lambda i, f: (l + 1, i, f)),
                 pl.BlockSpec((None, cb, rb), lambda i, f: (l + 1, f, i))]
    out_specs = [x_spec,
                 pl.BlockSpec((rb, cb), lambda i, f: (i, f)),
                 pl.BlockSpec((rb, cb), lambda i, f: (i, f)),
                 pl.BlockSpec((cb, rb), lambda i, f: (f, i))]
    out_shape = [x_shape, jax.ShapeDtypeStruct((d, dff), BF16), jax.ShapeDtypeStruct((d, dff), BF16),
                 jax.ShapeDtypeStruct((dff, d), BF16)]
    return pl.pallas_call(
        _ffn_cast_kernel,
        grid=(ni, nf), in_specs=in_specs, out_specs=out_specs, out_shape=out_shape,
        scratch_shapes=scratch, name="ffn_cast",
        compiler_params=_cparams("arbitrary", "arbitrary", vmem_limit=VMEM_LIMIT_FFN_CAST),
    )(x, gain, wg, wu, wd, fn_gain, *next_f32)


def _spread_rope_cols(w):
    z = jnp.zeros(w.shape[:-1] + (32,), w.dtype)
    return jnp.concatenate([w[..., :32], z, w[..., 32:], z], axis=-1)


def _prep_w_in(w_in):
    w_in_b = w_in.astype(BF16)
    kpe = _spread_rope_cols(w_in_b[..., Z_MAIN:Z_MAIN + QK_ROPE_DIM])
    tail = jnp.concatenate([w_in_b[..., Z_MAIN + QK_ROPE_DIM:], kpe], axis=-1)
    return w_in_b, tail


def _prep_w_uq(w_uq):
    l, r, _ = w_uq.shape
    w = w_uq.reshape(l, r, N_MLA_HEADS, QK_HEAD_DIM)
    w = jnp.concatenate([w[..., :QK_NOPE_DIM], _spread_rope_cols(w[..., QK_NOPE_DIM:])], axis=-1)
    return w.reshape(l, r, N_MLA_HEADS * QK_PAD).astype(BF16)


def _prep_w_ukv(w_ukv):
    l, r, _ = w_ukv.shape
    w = w_ukv.reshape(l, r, N_MLA_HEADS, 2 * HEAD_DIM)
    k_nope = w[..., :HEAD_DIM].reshape(l, r, MLA_WIDTH)
    v = w[..., HEAD_DIM:].reshape(l, r, MLA_WIDTH)
    return jnp.concatenate([k_nope, v], axis=-1).astype(BF16)


def _tiles(s):
    pick = lambda pref: pref if s % pref == 0 else s
    attn = pick(512)
    return dict(in_tm=pick(1024), mix_tm=attn, ret_chunk=128, attn_tq=attn,
                out_tm=pick(512), ffn_tm=pick(1024), ffn_tf=512)


def kernel(x, positions, attn_norm, w_in, mla_q_norm, w_uq, mla_kv_norm, w_ukv, gmlp_v_norm, gmlp_w_s,
           gmlp_b_s, mix_norm, w_out, ffn_norm, w_gate, w_up, w_down, final_norm):
    b, s, d = x.shape
    assert b == 1, "batch is folded away; only BATCH == 1 is supported"
    depth = w_in.shape[0]
    t = _tiles(s)

    w_in_b, w_in_tail = _prep_w_in(w_in)
    w_uq_b = _prep_w_uq(w_uq)
    w_ukv_b = _prep_w_ukv(w_ukv)
    n_prep = 8 if s % 8 == 0 else 1
    cos_r, sin_r, cos_m, sin_m, w_gate_0, w_up_0 = _mixers(
        [_rope_tables_part(positions, n_prep), _cast_part(w_gate, 0, n_prep), _cast_part(w_up, 0, n_prep)],
        n_prep, name="prep")
    ffn_w = [w_gate_0, w_up_0, None]
    rows = lambda a: a.reshape(depth, 1, -1)
    attn_g, q_g, kv_g, gv_g, mix_g, ffn_g = map(
        rows, (attn_norm, mla_q_norm, mla_kv_norm, gmlp_v_norm, mix_norm, ffn_norm))
    b_s = gmlp_b_s[..., None]
    fn_g = final_norm.reshape(1, d)

    ret_tables = _retention_tables(t["ret_chunk"])
    xs = x.reshape(s, d)
    for l in range(depth):
        z, zt = _in_proj(xs, attn_g, w_in_b, w_in_tail, l, t["in_tm"])
        tm = t["mix_tm"]
        parts = [_retention_part(z, cos_r, sin_r, ret_tables, mix_g, l, tm, t["ret_chunk"]),
                 _mla_proj_part(z, zt, q_g, kv_g, w_uq_b, w_ukv_b, cos_m, sin_m, l, tm),
                 _gmlp_part(zt, gv_g, gmlp_w_s, b_s, mix_g, l, tm),
                 _cast_part(w_out, l, s // tm)]
        if l == 0:
            parts.append(_cast_part(w_down, 0, s // tm))
        y_ret, qt, k, vt, y_gm, w_out_l, *w_down_0 = _mixers(parts, s // tm)
        if l == 0:
            ffn_w[2] = w_down_0[0]
        y_mla = _attention(qt, k, vt, mix_g, l, t["attn_tq"])
        xs = _out_proj(xs, y_ret, y_mla, y_gm, w_out_l, t["out_tm"])
        if l < depth - 1:
            xs, *ffn_w = _ffn(xs, ffn_g, *ffn_w, fn_g, l, t["ffn_tm"], t["ffn_tf"], final_norm=False,
                              next_f32=(w_gate, w_up, w_down))
        else:
            xs = _ffn(xs, ffn_g, *ffn_w, fn_g, l, t["ffn_tm"], t["ffn_tf"], final_norm=True)
    return xs.reshape(b, s, d)
```

```python
import functools

import jax
import jax.numpy as jnp
from jax import lax
from jax.experimental import pallas as pl
from jax.experimental.pallas import tpu as pltpu

F32 = jnp.float32
BF16 = jnp.bfloat16

HEAD_DIM = 128
N_RET_HEADS = 6
N_MLA_HEADS = 6
N_GMLP_GROUPS = 4
RET_WIDTH = N_RET_HEADS * HEAD_DIM
MLA_WIDTH = N_MLA_HEADS * HEAD_DIM
GMLP_WIDTH = N_GMLP_GROUPS * HEAD_DIM
Q_LORA_RANK = 512
KV_LORA_RANK = 512
QK_NOPE_DIM = 128
QK_ROPE_DIM = 64
QK_HEAD_DIM = QK_NOPE_DIM + QK_ROPE_DIM
CHUNK = 128
ROPE_BASE = 10000.0
EPS = 1e-6

LANES = 128
QK_PAD = 2 * LANES
ATTN_HEADS_PER_STEP = 3
VT_ROWS = HEAD_DIM + 16
LOG2_E = 1.4426950408889634

Z_CQ = 4 * RET_WIDTH
Z_CKV = Z_CQ + Q_LORA_RANK
Z_MAIN = Z_CKV + KV_LORA_RANK
Z_TN = 1024
ZT_GU = 0
ZT_GV = ZT_GU + GMLP_WIDTH
ZT_KPE = ZT_GV + GMLP_WIDTH
Z_TAIL = ZT_KPE + LANES

VMEM_LIMIT = 56 * 1024 * 1024
VMEM_LIMIT_FFN_CAST = 60 * 1024 * 1024


def _cparams(*sem, vmem_limit=VMEM_LIMIT):
    return pltpu.CompilerParams(dimension_semantics=sem, vmem_limit_bytes=vmem_limit)


def _layer_spec(arr, l, ngrid):
    zeros = (0,) * (arr.ndim - 1)
    return pl.BlockSpec((None,) + arr.shape[1:], lambda *_: (l,) + zeros)


def _rms(x, gain):
    return x * lax.rsqrt(jnp.mean(x * x, axis=-1, keepdims=True) + EPS) * gain


def _rope(x, cos, sin):
    return x * cos + pltpu.roll(x, 64, 1) * sin


def _rope_table_kernel(pos_ref, inv_r_ref, sgn_r_ref, inv_m_ref, cm_ref, sm_ref,
                       cos_r_ref, sin_r_ref, cos_m_ref, sin_m_ref):
    pos = pos_ref[...].astype(F32)
    ang_r = pos * inv_r_ref[...]
    cos_r_ref[...] = jnp.cos(ang_r)
    sin_r_ref[...] = jnp.sin(ang_r) * sgn_r_ref[...]
    ang_m = pos * inv_m_ref[...]
    cos_m_ref[...] = jnp.cos(ang_m) * cm_ref[...]
    sin_m_ref[...] = jnp.sin(ang_m) * sm_ref[...]


def _rope_tables_part(positions, n_tiles):
    s = positions.shape[-1]
    ts = s // n_tiles
    pos = positions.reshape(s, 1)
    inv_r = 1.0 / (ROPE_BASE ** (jnp.arange(0, HEAD_DIM, 2, dtype=F32) / HEAD_DIM))
    inv_m = 1.0 / (ROPE_BASE ** (jnp.arange(0, QK_ROPE_DIM, 2, dtype=F32) / QK_ROPE_DIM))
    z32 = jnp.zeros((32,), F32)
    o32 = jnp.ones((32,), F32)
    inv_r_full = jnp.concatenate([inv_r, inv_r]).reshape(1, LANES)
    sgn_r = jnp.concatenate([-jnp.ones((64,), F32), jnp.ones((64,), F32)]).reshape(1, LANES)
    inv_m_full = jnp.concatenate([inv_m, z32, inv_m, z32]).reshape(1, LANES)
    cmask = jnp.concatenate([o32, z32, o32, z32]).reshape(1, LANES)
    smask = jnp.concatenate([-o32, z32, o32, z32]).reshape(1, LANES)
    row = pl.BlockSpec((1, LANES), lambda i: (0, 0))
    tab = pl.BlockSpec((ts, LANES), lambda i: (i, 0))
    out = jax.ShapeDtypeStruct((s, LANES), F32)
    return (_rope_table_kernel,
            [pos, inv_r_full, sgn_r, inv_m_full, cmask, smask],
            [pl.BlockSpec((ts, 1), lambda i: (i, 0)), row, row, row, row, row],
            [tab, tab, tab, tab],
            [out, out, out, out],
            [])


def _in_proj_kernel(x_ref, g_ref, wm_ref, wt_ref, zm_ref, zt_ref, h_sc, *, n_main):
    j = pl.program_id(1)

    @pl.when(j == 0)
    def _():
        h_sc[...] = _rms(x_ref[...], g_ref[...]).astype(BF16)

    @pl.when(j < n_main)
    def _():
        zm_ref[...] = jnp.dot(h_sc[...], wm_ref[...], preferred_element_type=F32).astype(zm_ref.dtype)

    @pl.when(j == n_main)
    def _():
        zt_ref[...] = jnp.dot(h_sc[...], wt_ref[...], preferred_element_type=F32).astype(zt_ref.dtype)


def _in_proj(x, gain, w_in_b, w_tail, l, tm):
    s, d = x.shape
    n_main = Z_MAIN // Z_TN
    last = n_main - 1
    return pl.pallas_call(
        functools.partial(_in_proj_kernel, n_main=n_main),
        grid=(s // tm, n_main + 1),
        in_specs=[pl.BlockSpec((tm, d), lambda i, j: (i, 0)),
                  _layer_spec(gain, l, 2),
                  pl.BlockSpec((None, d, Z_TN), lambda i, j: (l, 0, jnp.minimum(j, last))),
                  _layer_spec(w_tail, l, 2)],
        out_specs=[pl.BlockSpec((tm, Z_TN), lambda i, j: (i, jnp.minimum(j, last))),
                   pl.BlockSpec((tm, Z_TAIL), lambda i, j: (i, 0))],
        out_shape=[jax.ShapeDtypeStruct((s, Z_MAIN), BF16),
                   jax.ShapeDtypeStruct((s, Z_TAIL), BF16)],
        scratch_shapes=[pltpu.VMEM((tm, d), BF16)],
        compiler_params=_cparams("arbitrary", "arbitrary"),
        name="in_proj",
    )(x, gain, w_in_b, w_tail)


def _retention_kernel(q_ref, k_ref, v_ref, g_ref, cos_ref, sin_ref, inner_ref, qdec_ref, kdec_ref,
                      cdec_ref, gain_ref, o_ref, state_sc, *, chunk, n_chunks):
    @pl.when(pl.program_id(0) == 0)
    def _():
        state_sc[...] = jnp.zeros_like(state_sc)

    for c in range(n_chunks):
        rows = slice(c * chunk, (c + 1) * chunk)
        cos = cos_ref[rows, :]
        sin = sin_ref[rows, :]
        for h in range(N_RET_HEADS):
            cols = slice(h * HEAD_DIM, (h + 1) * HEAD_DIM)
            q = _rope(q_ref[rows, cols].astype(F32), cos, sin)
            k = _rope(k_ref[rows, cols].astype(F32), cos, sin)
            v = v_ref[rows, cols]
            state = state_sc[h]
            scores = lax.dot_general(q.astype(BF16), k.astype(BF16), (((1,), (1,)), ((), ())),
                                     preferred_element_type=F32) * inner_ref[h]
            out = jnp.dot(scores.astype(BF16), v, preferred_element_type=F32)
            out += jnp.dot((q * qdec_ref[:, cols]).astype(BF16), state.astype(BF16),
                           preferred_element_type=F32)
            kt = (k * kdec_ref[:, cols]).T.astype(BF16)
            state_sc[h] = cdec_ref[:, cols] * state + jnp.dot(kt, v, preferred_element_type=F32)
            y = _rms(out, gain_ref[:, cols])
            gate = g_ref[rows, cols].astype(F32)
            o_ref[rows, cols] = (y * (gate / (1.0 + jnp.exp(-gate)))).astype(o_ref.dtype)


def _retention_tables(chunk):
    h = N_RET_HEADS
    log_gamma = jnp.log1p(-jnp.exp2(-5.0 - jnp.arange(h, dtype=F32)))
    idx = jnp.arange(chunk, dtype=F32)
    rel = idx[:, None] - idx[None, :]
    inner = jnp.where(rel >= 0, jnp.exp(log_gamma[:, None, None] * jnp.maximum(rel, 0.0)), 0.0)
    qdec = jnp.exp(log_gamma[None, :] * (idx[:, None] + 1.0))
    kdec = jnp.exp(log_gamma[None, :] * (chunk - 1.0 - idx[:, None]))
    cdec = jnp.exp(log_gamma * chunk)
    rep = lambda a: jnp.repeat(a, HEAD_DIM, axis=-1)
    scale = HEAD_DIM ** -0.5
    return inner * scale, rep(qdec), rep(kdec) * scale, rep(cdec[None, :])


def _retention_part(z, cos_r, sin_r, tables, mix, l, t, chunk):
    s = z.shape[0]
    zspec = lambda c: pl.BlockSpec((t, RET_WIDTH), lambda i: (i, c))
    tab = pl.BlockSpec((t, LANES), lambda i: (i, 0))
    full = lambda a: pl.BlockSpec(a.shape, lambda i: (0,) * a.ndim)
    return (functools.partial(_retention_kernel, chunk=chunk, n_chunks=t // chunk),
            [z, z, z, z, cos_r, sin_r, *tables, mix],
            [zspec(0), zspec(1), zspec(2), zspec(3), tab, tab] + [full(a) for a in tables]
            + [pl.BlockSpec((None, 1, RET_WIDTH), lambda i: (l, 0, 0))],
            [pl.BlockSpec((t, RET_WIDTH), lambda i: (i, 0))],
            [jax.ShapeDtypeStruct((s, RET_WIDTH), BF16)],
            [pltpu.VMEM((N_RET_HEADS, HEAD_DIM, HEAD_DIM), F32)])


def _mla_proj_kernel(cq_ref, ckv_ref, kpe_ref, gq_ref, gkv_ref, wuq_ref, wukv_ref, cos_ref, sin_ref,
                     qt_ref, k_ref, vt_ref):
    cos = cos_ref[...]
    sin = sin_ref[...]
    scale = QK_HEAD_DIM ** -0.5 * LOG2_E
    cq = _rms(cq_ref[...].astype(F32), gq_ref[...]).astype(BF16)
    mq = jnp.dot(cq, wuq_ref[...], preferred_element_type=F32)
    ckv = _rms(ckv_ref[...].astype(F32), gkv_ref[...]).astype(BF16)
    mkv = jnp.dot(ckv, wukv_ref[...], preferred_element_type=F32)
    kpe = _rope(kpe_ref[...].astype(F32), cos, sin).astype(BF16)
    for h in range(N_MLA_HEADS):
        nope = slice(h * QK_PAD, h * QK_PAD + LANES)
        pe = slice(h * QK_PAD + LANES, (h + 1) * QK_PAD)
        qt_ref[nope, :] = (mq[:, nope] * scale).T.astype(BF16)
        qt_ref[pe, :] = (_rope(mq[:, pe], cos, sin) * scale).T.astype(BF16)
        k_ref[:, nope] = mkv[:, h * LANES:(h + 1) * LANES].astype(BF16)
        k_ref[:, pe] = kpe
    vt = mkv[:, MLA_WIDTH:].T.astype(BF16)
    tm = vt.shape[1]
    extra = lax.broadcasted_iota(jnp.int32, (VT_ROWS - HEAD_DIM, tm), 0)
    ones_row = jnp.where(extra == 0, 1.0, 0.0).astype(BF16)
    for h in range(N_MLA_HEADS):
        vt_ref[h * VT_ROWS:h * VT_ROWS + HEAD_DIM, :] = vt[h * HEAD_DIM:(h + 1) * HEAD_DIM, :]
        vt_ref[h * VT_ROWS + HEAD_DIM:(h + 1) * VT_ROWS, :] = ones_row


def _mla_proj_part(z, zt, gq, gkv, wuq, wukv, cos_m, sin_m, l, tm):
    s = z.shape[0]
    tab = pl.BlockSpec((tm, LANES), lambda i: (i, 0))
    qk = N_MLA_HEADS * QK_PAD
    return (_mla_proj_kernel,
            [z, z, zt, gq, gkv, wuq, wukv, cos_m, sin_m],
            [pl.BlockSpec((tm, Q_LORA_RANK), lambda i: (i, Z_CQ // Q_LORA_RANK)),
             pl.BlockSpec((tm, KV_LORA_RANK), lambda i: (i, Z_CKV // KV_LORA_RANK)),
             pl.BlockSpec((tm, LANES), lambda i: (i, ZT_KPE // LANES)),
             _layer_spec(gq, l, 1), _layer_spec(gkv, l, 1),
             _layer_spec(wuq, l, 1), _layer_spec(wukv, l, 1), tab, tab],
            [pl.BlockSpec((qk, tm), lambda i: (0, i)),
             pl.BlockSpec((tm, qk), lambda i: (i, 0)),
             pl.BlockSpec((None, N_MLA_HEADS * VT_ROWS, tm), lambda i: (i, 0, 0))],
            [jax.ShapeDtypeStruct((qk, s), BF16),
             jax.ShapeDtypeStruct((s, qk), BF16),
             jax.ShapeDtypeStruct((s // tm, N_MLA_HEADS * VT_ROWS, tm), BF16)],
            [])


def _attn_kernel(qt_ref, k_ref, vt_ref, gain_ref, o_ref, m_sc, acc_sc, st0_sc, st1_sc, *, tq, hp):
    qi = pl.program_id(1)
    m_sc[...] = jnp.full_like(m_sc, -jnp.inf)
    acc_sc[...] = jnp.zeros_like(acc_sc)

    def scores(kb, st_sc):
        start = pl.multiple_of(kb * tq, tq)
        for j in range(hp):
            qk_cols = slice(j * QK_PAD, (j + 1) * QK_PAD)
            st_sc[j] = jnp.dot(k_ref[pl.ds(start, tq), qk_cols], qt_ref[qk_cols, :],
                               preferred_element_type=F32)

    def softmax_pv(kb, st_sc, masked):
        for j in range(hp):
            vt = vt_ref[kb, j * VT_ROWS:(j + 1) * VT_ROWS, :]
            st = st_sc[j]
            if masked:
                kv_pos = lax.broadcasted_iota(jnp.int32, st.shape, 0)
                q_pos = lax.broadcasted_iota(jnp.int32, st.shape, 1)
                st = jnp.where(kv_pos <= q_pos, st, -jnp.inf)
            m_prev = m_sc[j]
            m_new = jnp.maximum(m_prev, jnp.max(st, axis=0, keepdims=True))
            alpha = jnp.exp2(m_prev - m_new)
            p = jnp.exp2(st - m_new).astype(BF16)
            acc_sc[j] = alpha * acc_sc[j] + jnp.dot(vt, p, preferred_element_type=F32)
            m_sc[j] = m_new

    scores(0, st0_sc)

    def pair(i, carry):
        kb = 2 * i
        scores(kb + 1, st1_sc)
        softmax_pv(kb, st0_sc, False)
        scores(kb + 2, st0_sc)
        softmax_pv(kb + 1, st1_sc, False)
        return carry

    lax.fori_loop(0, qi // 2, pair, 0)

    @pl.when(qi % 2 == 0)
    def _():
        softmax_pv(qi, st0_sc, True)

    @pl.when(qi % 2 == 1)
    def _():
        scores(qi, st1_sc)
        softmax_pv(qi - 1, st0_sc, False)
        softmax_pv(qi, st1_sc, True)

    for j in range(hp):
        cols = slice(j * HEAD_DIM, (j + 1) * HEAD_DIM)
        denom = acc_sc[j, HEAD_DIM:HEAD_DIM + 1, :]
        o = (acc_sc[j, :HEAD_DIM, :] * (1.0 / denom)).T
        o_ref[:, cols] = _rms(o, gain_ref[:, cols]).astype(o_ref.dtype)


def _attention(qt, k, vt, mix, l, tq):
    s = k.shape[0]
    hp = ATTN_HEADS_PER_STEP
    width = hp * HEAD_DIM
    return pl.pallas_call(
        functools.partial(_attn_kernel, tq=tq, hp=hp),
        grid=(N_MLA_HEADS // hp, s // tq),
        in_specs=[pl.BlockSpec((hp * QK_PAD, tq), lambda h, i: (h, i)),
                  pl.BlockSpec((s, hp * QK_PAD), lambda h, i: (0, h)),
                  pl.BlockSpec((s // tq, hp * VT_ROWS, tq), lambda h, i: (0, h, 0)),
                  pl.BlockSpec((None, 1, width), lambda h, i: (l, 0, RET_WIDTH // width + h))],
        out_specs=pl.BlockSpec((tq, width), lambda h, i: (i, h)),
        out_shape=jax.ShapeDtypeStruct((s, MLA_WIDTH), BF16),
        scratch_shapes=[pltpu.VMEM((hp, 1, tq), F32),
                        pltpu.VMEM((hp, VT_ROWS, tq), F32),
                        pltpu.VMEM((hp, tq, tq), F32), pltpu.VMEM((hp, tq, tq), F32)],
        compiler_params=_cparams("arbitrary", "arbitrary"),
        name="mla_attention",
    )(qt, k, vt, mix)


def _gmlp_kernel(u_ref, v_ref, gv_ref, ws_ref, bs_ref, gm_ref, o_ref, *, n_chunks):
    row = lax.broadcasted_iota(jnp.int32, (CHUNK, CHUNK), 0)
    col = lax.broadcasted_iota(jnp.int32, (CHUNK, CHUNK), 1)
    for g in range(N_GMLP_GROUPS):
        cols = slice(g * HEAD_DIM, (g + 1) * HEAD_DIM)
        w = jnp.where(col <= row, ws_ref[g], 0.0).astype(BF16)
        bias = bs_ref[g]
        for c in range(n_chunks):
            rows = slice(c * CHUNK, (c + 1) * CHUNK)
            u = jax.nn.gelu(u_ref[rows, cols].astype(F32))
            v = jax.nn.gelu(v_ref[rows, cols].astype(F32))
            vn = _rms(v, gv_ref[:, cols]).astype(BF16)
            sg = jnp.dot(w, vn, preferred_element_type=F32) + bias
            o_ref[rows, cols] = _rms(u * sg, gm_ref[:, cols]).astype(o_ref.dtype)


def _gmlp_part(zt, gv, ws, bs, mix, l, t):
    s = zt.shape[0]
    return (functools.partial(_gmlp_kernel, n_chunks=t // CHUNK),
            [zt, zt, gv, ws, bs, mix],
            [pl.BlockSpec((t, GMLP_WIDTH), lambda i: (i, ZT_GU // GMLP_WIDTH)),
             pl.BlockSpec((t, GMLP_WIDTH), lambda i: (i, ZT_GV // GMLP_WIDTH)),
             _layer_spec(gv, l, 1), _layer_spec(ws, l, 1), _layer_spec(bs, l, 1),
             pl.BlockSpec((None, 1, GMLP_WIDTH),
                          lambda i: (l, 0, (RET_WIDTH + MLA_WIDTH) // GMLP_WIDTH))],
            [pl.BlockSpec((t, GMLP_WIDTH), lambda i: (i, 0))],
            [jax.ShapeDtypeStruct((s, GMLP_WIDTH), BF16)],
            [])


def _cast_kernel(w_ref, o_ref):
    o_ref[...] = w_ref[...].astype(o_ref.dtype)


def _cast_part(w, l, n_tiles):
    _, r, c = w.shape
    assert r % n_tiles == 0
    rb = r // n_tiles
    return (_cast_kernel, [w],
            [pl.BlockSpec((None, rb, c), lambda i: (l, i, 0))],
            [pl.BlockSpec((rb, c), lambda i: (i, 0))],
            [jax.ShapeDtypeStruct((r, c), BF16)],
            [])


def _mixers_kernel(*refs, bodies, n_in, n_out, n_scratch):
    ins, outs, scr = [], [], []
    pos = 0
    for group, counts in ((ins, n_in), (outs, n_out), (scr, n_scratch)):
        for c in counts:
            group.append(refs[pos:pos + c])
            pos += c
    for body, i, o, sc in zip(bodies, ins, outs, scr):
        body(*i, *o, *sc)


def _mixers(parts, n_tiles, name="mixers"):
    bodies = [p[0] for p in parts]
    flat = lambda k: [a for p in parts for a in p[k]]
    return pl.pallas_call(
        functools.partial(_mixers_kernel, bodies=bodies, n_in=[len(p[1]) for p in parts],
                          n_out=[len(p[3]) for p in parts], n_scratch=[len(p[5]) for p in parts]),
        grid=(n_tiles,),
        in_specs=flat(2), out_specs=flat(3), out_shape=flat(4), scratch_shapes=flat(5),
        compiler_params=_cparams("arbitrary"),
        name=name,
    )(*flat(1))


def _out_proj_kernel(x_ref, yr_ref, ym_ref, yg_ref, w_ref, o_ref):
    y = jnp.concatenate([yr_ref[...], ym_ref[...], yg_ref[...]], axis=-1)
    o_ref[...] = x_ref[...] + jnp.dot(y, w_ref[...], preferred_element_type=F32)


def _out_proj(x, yr, ym, yg, w, tm):
    s, d = x.shape
    blk = lambda n: pl.BlockSpec((tm, n), lambda i: (i, 0))
    return pl.pallas_call(
        _out_proj_kernel,
        grid=(s // tm,),
        in_specs=[blk(d), blk(RET_WIDTH), blk(MLA_WIDTH), blk(GMLP_WIDTH),
                  pl.BlockSpec(w.shape, lambda i: (0, 0), pipeline_mode=pl.Buffered(1))],
        out_specs=blk(d),
        out_shape=jax.ShapeDtypeStruct((s, d), F32),
        compiler_params=_cparams("arbitrary"),
        name="out_proj",
    )(x, yr, ym, yg, w)


def _ffn_kernel(x_ref, g_ref, wg_ref, wu_ref, wd_ref, fn_ref, o_ref, h_sc, *, final_norm):
    f = pl.program_id(1)

    @pl.when(f == 0)
    def _():
        x = x_ref[...]
        h_sc[...] = _rms(x, g_ref[...]).astype(BF16)
        o_ref[...] = x

    h = h_sc[...]
    gate = jnp.dot(h, wg_ref[...], preferred_element_type=F32)
    up = jnp.dot(h, wu_ref[...], preferred_element_type=F32)
    act = (gate / (1.0 + jnp.exp(-gate)) * up).astype(BF16)
    o_ref[...] += jnp.dot(act, wd_ref[...], preferred_element_type=F32)

    if final_norm:
        @pl.when(f == pl.num_programs(1) - 1)
        def _():
            o_ref[...] = _rms(o_ref[...], fn_ref[...])


def _ffn_cast_kernel(x_ref, g_ref, wg_ref, wu_ref, wd_ref, fn_ref, ng_ref, nu_ref, nd_ref,
                     o_ref, cg_ref, cu_ref, cd_ref, h_sc):
    cg_ref[...] = ng_ref[...].astype(BF16)
    cu_ref[...] = nu_ref[...].astype(BF16)
    cd_ref[...] = nd_ref[...].astype(BF16)
    _ffn_kernel(x_ref, g_ref, wg_ref, wu_ref, wd_ref, fn_ref, o_ref, h_sc, final_norm=False)


def _ffn(x, gain, wg, wu, wd, fn_gain, l, tm, tf, final_norm, next_f32=None):
    s, d = x.shape
    dff = wg.shape[1]
    ni, nf = s // tm, dff // tf
    in_specs = [pl.BlockSpec((tm, d), lambda i, f: (i, 0)),
                _layer_spec(gain, l, 2),
                pl.BlockSpec((d, tf), lambda i, f: (0, f)),
                pl.BlockSpec((d, tf), lambda i, f: (0, f)),
                pl.BlockSpec((tf, d), lambda i, f: (f, 0)),
                pl.BlockSpec((1, d), lambda i, f: (0, 0))]
    x_spec = pl.BlockSpec((tm, d), lambda i, f: (i, 0))
    x_shape = jax.ShapeDtypeStruct((s, d), F32)
    scratch = [pltpu.VMEM((tm, d), BF16)]
    params = _cparams("arbitrary", "arbitrary")
    if next_f32 is None:
        return pl.pallas_call(
            functools.partial(_ffn_kernel, final_norm=final_norm),
            grid=(ni, nf), in_specs=in_specs, out_specs=x_spec, out_shape=x_shape,
            scratch_shapes=scratch, compiler_params=params, name="ffn",
        )(x, gain, wg, wu, wd, fn_gain)
    assert not final_norm and d % ni == 0 and dff % nf == 0
    rb, cb = d // ni, dff // nf
    in_specs += [pl.BlockSpec((None, rb, cb), lambda i, f: (l + 1, i, f)),
                 pl.BlockSpec((None, rb, cb), lambda i, f: (l + 1, i, f)),
                 pl.BlockSpec((None, cb, rb), lambda i, f: (l + 1, f, i))]
    out_specs = [x_spec,
                 pl.BlockSpec((rb, cb), lambda i, f: (i, f)),
                 pl.BlockSpec((rb, cb), lambda i, f: (i, f)),
                 pl.BlockSpec((cb, rb), lambda i, f: (f, i))]
    out_shape = [x_shape, jax.ShapeDtypeStruct((d, dff), BF16), jax.ShapeDtypeStruct((d, dff), BF16),
                 jax.ShapeDtypeStruct((dff, d), BF16)]
    return pl.pallas_call(
        _ffn_cast_kernel,
        grid=(ni, nf), in_specs=in_specs, out_specs=out_specs, out_shape=out_shape,
        scratch_shapes=scratch, name="ffn_cast",
        compiler_params=_cparams("arbitrary", "arbitrary", vmem_limit=VMEM_LIMIT_FFN_CAST),
    )(x, gain, wg, wu, wd, fn_gain, *next_f32)


def _spread_rope_cols(w):
    z = jnp.zeros(w.shape[:-1] + (32,), w.dtype)
    return jnp.concatenate([w[..., :32], z, w[..., 32:], z], axis=-1)


def _prep_w_in(w_in):
    w_in_b = w_in.astype(BF16)
    kpe = _spread_rope_cols(w_in_b[..., Z_MAIN:Z_MAIN + QK_ROPE_DIM])
    tail = jnp.concatenate([w_in_b[..., Z_MAIN + QK_ROPE_DIM:], kpe], axis=-1)
    return w_in_b, tail


def _prep_w_uq(w_uq):
    l, r, _ = w_uq.shape
    w = w_uq.reshape(l, r, N_MLA_HEADS, QK_HEAD_DIM)
    w = jnp.concatenate([w[..., :QK_NOPE_DIM], _spread_rope_cols(w[..., QK_NOPE_DIM:])], axis=-1)
    return w.reshape(l, r, N_MLA_HEADS * QK_PAD).astype(BF16)


def _prep_w_ukv(w_ukv):
    l, r, _ = w_ukv.shape
    w = w_ukv.reshape(l, r, N_MLA_HEADS, 2 * HEAD_DIM)
    k_nope = w[..., :HEAD_DIM].reshape(l, r, MLA_WIDTH)
    v = w[..., HEAD_DIM:].reshape(l, r, MLA_WIDTH)
    return jnp.concatenate([k_nope, v], axis=-1).astype(BF16)


def _tiles(s):
    pick = lambda pref: pref if s % pref == 0 else s
    attn = pick(512)
    return dict(in_tm=pick(1024), mix_tm=attn, ret_chunk=128, attn_tq=attn,
                out_tm=pick(1024), ffn_tm=pick(1024), ffn_tf=512)


def kernel(x, positions, attn_norm, w_in, mla_q_norm, w_uq, mla_kv_norm, w_ukv, gmlp_v_norm, gmlp_w_s,
           gmlp_b_s, mix_norm, w_out, ffn_norm, w_gate, w_up, w_down, final_norm):
    b, s, d = x.shape
    assert b == 1, "batch is folded away; only BATCH == 1 is supported"
    depth = w_in.shape[0]
    t = _tiles(s)

    w_in_b, w_in_tail = _prep_w_in(w_in)
    w_uq_b = _prep_w_uq(w_uq)
    w_ukv_b = _prep_w_ukv(w_ukv)
    n_prep = 8 if s % 8 == 0 else 1
    cos_r, sin_r, cos_m, sin_m, w_gate_0, w_up_0 = _mixers(
        [_rope_tables_part(positions, n_prep), _cast_part(w_gate, 0, n_prep), _cast_part(w_up, 0, n_prep)],
        n_prep, name="prep")
    ffn_w = [w_gate_0, w_up_0, None]
    rows = lambda a: a.reshape(depth, 1, -1)
    attn_g, q_g, kv_g, gv_g, mix_g, ffn_g = map(
        rows, (attn_norm, mla_q_norm, mla_kv_norm, gmlp_v_norm, mix_norm, ffn_norm))
    b_s = gmlp_b_s[..., None]
    fn_g = final_norm.reshape(1, d)

    ret_tables = _retention_tables(t["ret_chunk"])
    xs = x.reshape(s, d)
    for l in range(depth):
        z, zt = _in_proj(xs, attn_g, w_in_b, w_in_tail, l, t["in_tm"])
        tm = t["mix_tm"]
        parts = [_retention_part(z, cos_r, sin_r, ret_tables, mix_g, l, tm, t["ret_chunk"]),
                 _gmlp_part(zt, gv_g, gmlp_w_s, b_s, mix_g, l, tm),
                 _mla_proj_part(z, zt, q_g, kv_g, w_uq_b, w_ukv_b, cos_m, sin_m, l, tm),
                 _cast_part(w_out, l, s // tm)]
        if l == 0:
            parts.append(_cast_part(w_down, 0, s // tm))
        y_ret, y_gm, qt, k, vt, w_out_l, *w_down_0 = _mixers(parts, s // tm)
        if l == 0:
            ffn_w[2] = w_down_0[0]
        y_mla = _attention(qt, k, vt, mix_g, l, t["attn_tq"])
        xs = _out_proj(xs, y_ret, y_mla, y_gm, w_out_l, t["out_tm"])
        if l < depth - 1:
            xs, *ffn_w = _ffn(xs, ffn_g, *ffn_w, fn_g, l, t["ffn_tm"], t["ffn_tf"], final_norm=False,
                              next_f32=(w_gate, w_up, w_down))
        else:
            xs = _ffn(xs, ffn_g, *ffn_w, fn_g, l, t["ffn_tm"], t["ffn_tf"], final_norm=True)
    return xs.reshape(b, s, d)
```

```python
import functools
import inspect

import jax
import jax.numpy as jnp
from jax import lax
from jax.experimental import pallas as pl
from jax.experimental.pallas import tpu as pltpu

F32 = jnp.float32
BF16 = jnp.bfloat16

HEAD_DIM = 128
N_RET_HEADS = 6
N_MLA_HEADS = 6
N_GMLP_GROUPS = 4
RET_WIDTH = N_RET_HEADS * HEAD_DIM
MLA_WIDTH = N_MLA_HEADS * HEAD_DIM
GMLP_WIDTH = N_GMLP_GROUPS * HEAD_DIM
Q_LORA_RANK = 512
KV_LORA_RANK = 512
QK_NOPE_DIM = 128
QK_ROPE_DIM = 64
QK_HEAD_DIM = QK_NOPE_DIM + QK_ROPE_DIM
CHUNK = 128
ROPE_BASE = 10000.0
EPS = 1e-6

LANES = 128
QK_PAD = 2 * LANES
ATTN_HEADS_PER_STEP = 3
VT_ROWS = HEAD_DIM + 16
LOG2_E = 1.4426950408889634
_DONE = object()

Z_CQ = 4 * RET_WIDTH
Z_CKV = Z_CQ + Q_LORA_RANK
Z_MAIN = Z_CKV + KV_LORA_RANK
Z_TN = 1024
ZT_GU = 0
ZT_GV = ZT_GU + GMLP_WIDTH
ZT_KPE = ZT_GV + GMLP_WIDTH
Z_TAIL = ZT_KPE + LANES

VMEM_LIMIT = 56 * 1024 * 1024
VMEM_LIMIT_FFN_CAST = 60 * 1024 * 1024


def _cparams(*sem, vmem_limit=VMEM_LIMIT):
    return pltpu.CompilerParams(dimension_semantics=sem, vmem_limit_bytes=vmem_limit)


def _layer_spec(arr, l, ngrid):
    zeros = (0,) * (arr.ndim - 1)
    return pl.BlockSpec((None,) + arr.shape[1:], lambda *_: (l,) + zeros)


def _rms(x, gain):
    return x * lax.rsqrt(jnp.mean(x * x, axis=-1, keepdims=True) + EPS) * gain


def _rope(x, cos, sin):
    return x * cos + pltpu.roll(x, 64, 1) * sin


def _rope_table_kernel(pos_ref, inv_r_ref, sgn_r_ref, inv_m_ref, cm_ref, sm_ref,
                       cos_r_ref, sin_r_ref, cos_m_ref, sin_m_ref):
    pos = pos_ref[...].astype(F32)
    ang_r = pos * inv_r_ref[...]
    cos_r_ref[...] = jnp.cos(ang_r)
    sin_r_ref[...] = jnp.sin(ang_r) * sgn_r_ref[...]
    ang_m = pos * inv_m_ref[...]
    cos_m_ref[...] = jnp.cos(ang_m) * cm_ref[...]
    sin_m_ref[...] = jnp.sin(ang_m) * sm_ref[...]


def _rope_tables_part(positions, n_tiles):
    s = positions.shape[-1]
    ts = s // n_tiles
    pos = positions.reshape(s, 1)
    inv_r = 1.0 / (ROPE_BASE ** (jnp.arange(0, HEAD_DIM, 2, dtype=F32) / HEAD_DIM))
    inv_m = 1.0 / (ROPE_BASE ** (jnp.arange(0, QK_ROPE_DIM, 2, dtype=F32) / QK_ROPE_DIM))
    z32 = jnp.zeros((32,), F32)
    o32 = jnp.ones((32,), F32)
    inv_r_full = jnp.concatenate([inv_r, inv_r]).reshape(1, LANES)
    sgn_r = jnp.concatenate([-jnp.ones((64,), F32), jnp.ones((64,), F32)]).reshape(1, LANES)
    inv_m_full = jnp.concatenate([inv_m, z32, inv_m, z32]).reshape(1, LANES)
    cmask = jnp.concatenate([o32, z32, o32, z32]).reshape(1, LANES)
    smask = jnp.concatenate([-o32, z32, o32, z32]).reshape(1, LANES)
    row = pl.BlockSpec((1, LANES), lambda i: (0, 0))
    tab = pl.BlockSpec((ts, LANES), lambda i: (i, 0))
    out = jax.ShapeDtypeStruct((s, LANES), F32)
    return (_rope_table_kernel,
            [pos, inv_r_full, sgn_r, inv_m_full, cmask, smask],
            [pl.BlockSpec((ts, 1), lambda i: (i, 0)), row, row, row, row, row],
            [tab, tab, tab, tab],
            [out, out, out, out],
            [])


def _in_proj_kernel(x_ref, g_ref, wm_ref, wt_ref, zm_ref, zt_ref, h_sc, *, n_main):
    j = pl.program_id(1)

    @pl.when(j == 0)
    def _():
        h_sc[...] = _rms(x_ref[...], g_ref[...]).astype(BF16)

    @pl.when(j < n_main)
    def _():
        zm_ref[...] = jnp.dot(h_sc[...], wm_ref[...], preferred_element_type=F32).astype(zm_ref.dtype)

    @pl.when(j == n_main)
    def _():
        zt_ref[...] = jnp.dot(h_sc[...], wt_ref[...], preferred_element_type=F32).astype(zt_ref.dtype)


def _in_proj(x, gain, w_in_b, w_tail, l, tm):
    s, d = x.shape
    n_main = Z_MAIN // Z_TN
    last = n_main - 1
    return pl.pallas_call(
        functools.partial(_in_proj_kernel, n_main=n_main),
        grid=(s // tm, n_main + 1),
        in_specs=[pl.BlockSpec((tm, d), lambda i, j: (i, 0)),
                  _layer_spec(gain, l, 2),
                  pl.BlockSpec((None, d, Z_TN), lambda i, j: (l, 0, jnp.minimum(j, last))),
                  _layer_spec(w_tail, l, 2)],
        out_specs=[pl.BlockSpec((tm, Z_TN), lambda i, j: (i, jnp.minimum(j, last))),
                   pl.BlockSpec((tm, Z_TAIL), lambda i, j: (i, 0))],
        out_shape=[jax.ShapeDtypeStruct((s, Z_MAIN), BF16),
                   jax.ShapeDtypeStruct((s, Z_TAIL), BF16)],
        scratch_shapes=[pltpu.VMEM((tm, d), BF16)],
        compiler_params=_cparams("arbitrary", "arbitrary"),
        name="in_proj",
    )(x, gain, w_in_b, w_tail)


def _retention_kernel(q_ref, k_ref, v_ref, g_ref, cos_ref, sin_ref, inner_ref, qdec_ref, kdec_ref,
                      cdec_ref, gain_ref, o_ref, state_sc, *, chunk, n_chunks):
    @pl.when(pl.program_id(0) == 0)
    def _():
        state_sc[...] = jnp.zeros_like(state_sc)

    for c in range(n_chunks):
        rows = slice(c * chunk, (c + 1) * chunk)
        cos = cos_ref[rows, :]
        sin = sin_ref[rows, :]
        for h in range(N_RET_HEADS):
            cols = slice(h * HEAD_DIM, (h + 1) * HEAD_DIM)
            q = _rope(q_ref[rows, cols].astype(F32), cos, sin)
            k = _rope(k_ref[rows, cols].astype(F32), cos, sin)
            v = v_ref[rows, cols]
            state = state_sc[h]
            scores = lax.dot_general(q.astype(BF16), k.astype(BF16), (((1,), (1,)), ((), ())),
                                     preferred_element_type=F32) * inner_ref[h]
            out = jnp.dot(scores.astype(BF16), v, preferred_element_type=F32)
            out += jnp.dot((q * qdec_ref[:, cols]).astype(BF16), state.astype(BF16),
                           preferred_element_type=F32)
            kt = (k * kdec_ref[:, cols]).T.astype(BF16)
            state_sc[h] = cdec_ref[:, cols] * state + jnp.dot(kt, v, preferred_element_type=F32)
            y = _rms(out, gain_ref[:, cols])
            gate = g_ref[rows, cols].astype(F32)
            o_ref[rows, cols] = (y * (gate / (1.0 + jnp.exp(-gate)))).astype(o_ref.dtype)
        yield


def _retention_tables(chunk):
    h = N_RET_HEADS
    log_gamma = jnp.log1p(-jnp.exp2(-5.0 - jnp.arange(h, dtype=F32)))
    idx = jnp.arange(chunk, dtype=F32)
    rel = idx[:, None] - idx[None, :]
    inner = jnp.where(rel >= 0, jnp.exp(log_gamma[:, None, None] * jnp.maximum(rel, 0.0)), 0.0)
    qdec = jnp.exp(log_gamma[None, :] * (idx[:, None] + 1.0))
    kdec = jnp.exp(log_gamma[None, :] * (chunk - 1.0 - idx[:, None]))
    cdec = jnp.exp(log_gamma * chunk)
    rep = lambda a: jnp.repeat(a, HEAD_DIM, axis=-1)
    scale = HEAD_DIM ** -0.5
    return inner * scale, rep(qdec), rep(kdec) * scale, rep(cdec[None, :])


def _retention_part(z, cos_r, sin_r, tables, mix, l, t, chunk):
    s = z.shape[0]
    zspec = lambda c: pl.BlockSpec((t, RET_WIDTH), lambda i: (i, c))
    tab = pl.BlockSpec((t, LANES), lambda i: (i, 0))
    full = lambda a: pl.BlockSpec(a.shape, lambda i: (0,) * a.ndim)
    return (functools.partial(_retention_kernel, chunk=chunk, n_chunks=t // chunk),
            [z, z, z, z, cos_r, sin_r, *tables, mix],
            [zspec(0), zspec(1), zspec(2), zspec(3), tab, tab] + [full(a) for a in tables]
            + [pl.BlockSpec((None, 1, RET_WIDTH), lambda i: (l, 0, 0))],
            [pl.BlockSpec((t, RET_WIDTH), lambda i: (i, 0))],
            [jax.ShapeDtypeStruct((s, RET_WIDTH), BF16)],
            [pltpu.VMEM((N_RET_HEADS, HEAD_DIM, HEAD_DIM), F32)])


def _mla_proj_kernel(cq_ref, ckv_ref, kpe_ref, gq_ref, gkv_ref, wuq_ref, wukv_ref, cos_ref, sin_ref,
                     qt_ref, k_ref, vt_ref):
    cos = cos_ref[...]
    sin = sin_ref[...]
    scale = QK_HEAD_DIM ** -0.5 * LOG2_E
    cq = _rms(cq_ref[...].astype(F32), gq_ref[...]).astype(BF16)
    mq = jnp.dot(cq, wuq_ref[...], preferred_element_type=F32)
    yield
    for h in range(N_MLA_HEADS):
        nope = slice(h * QK_PAD, h * QK_PAD + LANES)
        pe = slice(h * QK_PAD + LANES, (h + 1) * QK_PAD)
        qt_ref[nope, :] = (mq[:, nope] * scale).T.astype(BF16)
        qt_ref[pe, :] = (_rope(mq[:, pe], cos, sin) * scale).T.astype(BF16)
    yield
    ckv = _rms(ckv_ref[...].astype(F32), gkv_ref[...]).astype(BF16)
    mkv = jnp.dot(ckv, wukv_ref[...], preferred_element_type=F32)
    kpe = _rope(kpe_ref[...].astype(F32), cos, sin).astype(BF16)
    for h in range(N_MLA_HEADS):
        k_ref[:, h * QK_PAD:h * QK_PAD + LANES] = mkv[:, h * LANES:(h + 1) * LANES].astype(BF16)
        k_ref[:, h * QK_PAD + LANES:(h + 1) * QK_PAD] = kpe
    yield
    vt = mkv[:, MLA_WIDTH:].T.astype(BF16)
    tm = vt.shape[1]
    extra = lax.broadcasted_iota(jnp.int32, (VT_ROWS - HEAD_DIM, tm), 0)
    ones_row = jnp.where(extra == 0, 1.0, 0.0).astype(BF16)
    for h in range(N_MLA_HEADS):
        vt_ref[h * VT_ROWS:h * VT_ROWS + HEAD_DIM, :] = vt[h * HEAD_DIM:(h + 1) * HEAD_DIM, :]
        vt_ref[h * VT_ROWS + HEAD_DIM:(h + 1) * VT_ROWS, :] = ones_row


def _mla_proj_part(z, zt, gq, gkv, wuq, wukv, cos_m, sin_m, l, tm):
    s = z.shape[0]
    tab = pl.BlockSpec((tm, LANES), lambda i: (i, 0))
    qk = N_MLA_HEADS * QK_PAD
    return (_mla_proj_kernel,
            [z, z, zt, gq, gkv, wuq, wukv, cos_m, sin_m],
            [pl.BlockSpec((tm, Q_LORA_RANK), lambda i: (i, Z_CQ // Q_LORA_RANK)),
             pl.BlockSpec((tm, KV_LORA_RANK), lambda i: (i, Z_CKV // KV_LORA_RANK)),
             pl.BlockSpec((tm, LANES), lambda i: (i, ZT_KPE // LANES)),
             _layer_spec(gq, l, 1), _layer_spec(gkv, l, 1),
             _layer_spec(wuq, l, 1), _layer_spec(wukv, l, 1), tab, tab],
            [pl.BlockSpec((qk, tm), lambda i: (0, i)),
             pl.BlockSpec((tm, qk), lambda i: (i, 0)),
             pl.BlockSpec((None, N_MLA_HEADS * VT_ROWS, tm), lambda i: (i, 0, 0))],
            [jax.ShapeDtypeStruct((qk, s), BF16),
             jax.ShapeDtypeStruct((s, qk), BF16),
             jax.ShapeDtypeStruct((s // tm, N_MLA_HEADS * VT_ROWS, tm), BF16)],
            [])


def _attn_kernel(qt_ref, k_ref, vt_ref, gain_ref, o_ref, m_sc, acc_sc, st0_sc, st1_sc, *, tq, hp):
    qi = pl.program_id(1)
    m_sc[...] = jnp.full_like(m_sc, -jnp.inf)
    acc_sc[...] = jnp.zeros_like(acc_sc)

    heads = range(hp)

    def scores(kb, st_sc, js=heads):
        start = pl.multiple_of(kb * tq, tq)
        for j in js:
            qk_cols = slice(j * QK_PAD, (j + 1) * QK_PAD)
            st_sc[j] = jnp.dot(k_ref[pl.ds(start, tq), qk_cols], qt_ref[qk_cols, :],
                               preferred_element_type=F32)

    def softmax_pv(kb, st_sc, masked, js=heads):
        for j in js:
            vt = vt_ref[kb, j * VT_ROWS:(j + 1) * VT_ROWS, :]
            st = st_sc[j]
            if masked:
                kv_pos = lax.broadcasted_iota(jnp.int32, st.shape, 0)
                q_pos = lax.broadcasted_iota(jnp.int32, st.shape, 1)
                st = jnp.where(kv_pos <= q_pos, st, -jnp.inf)
            m_prev = m_sc[j]
            m_new = jnp.maximum(m_prev, jnp.max(st, axis=0, keepdims=True))
            alpha = jnp.exp2(m_prev - m_new)
            p = jnp.exp2(st - m_new).astype(BF16)
            acc_sc[j] = alpha * acc_sc[j] + jnp.dot(vt, p, preferred_element_type=F32)
            m_sc[j] = m_new

    scores(0, st0_sc)

    def pair(i, carry):
        kb = 2 * i
        for j in heads:
            scores(kb + 1, st1_sc, [j])
            softmax_pv(kb, st0_sc, False, [j])
        for j in heads:
            scores(kb + 2, st0_sc, [j])
            softmax_pv(kb + 1, st1_sc, False, [j])
        return carry

    lax.fori_loop(0, qi // 2, pair, 0)

    @pl.when(qi % 2 == 0)
    def _():
        softmax_pv(qi, st0_sc, True)

    @pl.when(qi % 2 == 1)
    def _():
        for j in heads:
            scores(qi, st1_sc, [j])
            softmax_pv(qi - 1, st0_sc, False, [j])
        softmax_pv(qi, st1_sc, True)

    for j in range(hp):
        cols = slice(j * HEAD_DIM, (j + 1) * HEAD_DIM)
        denom = acc_sc[j, HEAD_DIM:HEAD_DIM + 1, :]
        o = (acc_sc[j, :HEAD_DIM, :] * (1.0 / denom)).T
        o_ref[:, cols] = _rms(o, gain_ref[:, cols]).astype(o_ref.dtype)


def _attention(qt, k, vt, mix, l, tq):
    s = k.shape[0]
    hp = ATTN_HEADS_PER_STEP
    width = hp * HEAD_DIM
    return pl.pallas_call(
        functools.partial(_attn_kernel, tq=tq, hp=hp),
        grid=(N_MLA_HEADS // hp, s // tq),
        in_specs=[pl.BlockSpec((hp * QK_PAD, tq), lambda h, i: (h, i)),
                  pl.BlockSpec((s, hp * QK_PAD), lambda h, i: (0, h)),
                  pl.BlockSpec((s // tq, hp * VT_ROWS, tq), lambda h, i: (0, h, 0)),
                  pl.BlockSpec((None, 1, width), lambda h, i: (l, 0, RET_WIDTH // width + h))],
        out_specs=pl.BlockSpec((tq, width), lambda h, i: (i, h)),
        out_shape=jax.ShapeDtypeStruct((s, MLA_WIDTH), BF16),
        scratch_shapes=[pltpu.VMEM((hp, 1, tq), F32),
                        pltpu.VMEM((hp, VT_ROWS, tq), F32),
                        pltpu.VMEM((hp, tq, tq), F32), pltpu.VMEM((hp, tq, tq), F32)],
        compiler_params=_cparams("arbitrary", "arbitrary"),
        name="mla_attention",
    )(qt, k, vt, mix)


def _gmlp_kernel(u_ref, v_ref, gv_ref, ws_ref, bs_ref, gm_ref, o_ref, *, n_chunks):
    row = lax.broadcasted_iota(jnp.int32, (CHUNK, CHUNK), 0)
    col = lax.broadcasted_iota(jnp.int32, (CHUNK, CHUNK), 1)
    for g in range(N_GMLP_GROUPS):
        cols = slice(g * HEAD_DIM, (g + 1) * HEAD_DIM)
        w = jnp.where(col <= row, ws_ref[g], 0.0).astype(BF16)
        bias = bs_ref[g]
        for c in range(n_chunks):
            rows = slice(c * CHUNK, (c + 1) * CHUNK)
            u = jax.nn.gelu(u_ref[rows, cols].astype(F32))
            v = jax.nn.gelu(v_ref[rows, cols].astype(F32))
            vn = _rms(v, gv_ref[:, cols]).astype(BF16)
            sg = jnp.dot(w, vn, preferred_element_type=F32) + bias
            o_ref[rows, cols] = _rms(u * sg, gm_ref[:, cols]).astype(o_ref.dtype)
        yield


def _gmlp_part(zt, gv, ws, bs, mix, l, t):
    s = zt.shape[0]
    return (functools.partial(_gmlp_kernel, n_chunks=t // CHUNK),
            [zt, zt, gv, ws, bs, mix],
            [pl.BlockSpec((t, GMLP_WIDTH), lambda i: (i, ZT_GU // GMLP_WIDTH)),
             pl.BlockSpec((t, GMLP_WIDTH), lambda i: (i, ZT_GV // GMLP_WIDTH)),
             _layer_spec(gv, l, 1), _layer_spec(ws, l, 1), _layer_spec(bs, l, 1),
             pl.BlockSpec((None, 1, GMLP_WIDTH),
                          lambda i: (l, 0, (RET_WIDTH + MLA_WIDTH) // GMLP_WIDTH))],
            [pl.BlockSpec((t, GMLP_WIDTH), lambda i: (i, 0))],
            [jax.ShapeDtypeStruct((s, GMLP_WIDTH), BF16)],
            [])


def _cast_kernel(w_ref, o_ref):
    o_ref[...] = w_ref[...].astype(o_ref.dtype)


def _cast_part(w, l, n_tiles):
    _, r, c = w.shape
    assert r % n_tiles == 0
    rb = r // n_tiles
    return (_cast_kernel, [w],
            [pl.BlockSpec((None, rb, c), lambda i: (l, i, 0))],
            [pl.BlockSpec((rb, c), lambda i: (i, 0))],
            [jax.ShapeDtypeStruct((r, c), BF16)],
            [])


def _mixers_kernel(*refs, bodies, n_in, n_out, n_scratch):
    ins, outs, scr = [], [], []
    pos = 0
    for group, counts in ((ins, n_in), (outs, n_out), (scr, n_scratch)):
        for c in counts:
            group.append(refs[pos:pos + c])
            pos += c
    staged = [body(*i, *o, *sc) for body, i, o, sc in zip(bodies, ins, outs, scr)]
    staged = [g for g in staged if inspect.isgenerator(g)]
    while staged:
        for g in list(staged):
            if next(g, _DONE) is _DONE:
                staged.remove(g)


def _mixers(parts, n_tiles, name="mixers"):
    bodies = [p[0] for p in parts]
    flat = lambda k: [a for p in parts for a in p[k]]
    return pl.pallas_call(
        functools.partial(_mixers_kernel, bodies=bodies, n_in=[len(p[1]) for p in parts],
                          n_out=[len(p[3]) for p in parts], n_scratch=[len(p[5]) for p in parts]),
        grid=(n_tiles,),
        in_specs=flat(2), out_specs=flat(3), out_shape=flat(4), scratch_shapes=flat(5),
        compiler_params=_cparams("arbitrary"),
        name=name,
    )(*flat(1))


def _out_proj_kernel(x_ref, yr_ref, ym_ref, yg_ref, w_ref, o_ref):
    y = jnp.concatenate([yr_ref[...], ym_ref[...], yg_ref[...]], axis=-1)
    o_ref[...] = x_ref[...] + jnp.dot(y, w_ref[...], preferred_element_type=F32)


def _out_proj(x, yr, ym, yg, w, tm):
    s, d = x.shape
    blk = lambda n: pl.BlockSpec((tm, n), lambda i: (i, 0))
    return pl.pallas_call(
        _out_proj_kernel,
        grid=(s // tm,),
        in_specs=[blk(d), blk(RET_WIDTH), blk(MLA_WIDTH), blk(GMLP_WIDTH),
                  pl.BlockSpec(w.shape, lambda i: (0, 0), pipeline_mode=pl.Buffered(1))],
        out_specs=blk(d),
        out_shape=jax.ShapeDtypeStruct((s, d), F32),
        compiler_params=_cparams("arbitrary"),
        name="out_proj",
    )(x, yr, ym, yg, w)


def _ffn_kernel(x_ref, g_ref, wg_ref, wu_ref, wd_ref, fn_ref, o_ref, h_sc, *, final_norm):
    f = pl.program_id(1)

    @pl.when(f == 0)
    def _():
        x = x_ref[...]
        h_sc[...] = _rms(x, g_ref[...]).astype(BF16)
        o_ref[...] = x

    h = h_sc[...]
    gate = jnp.dot(h, wg_ref[...], preferred_element_type=F32)
    up = jnp.dot(h, wu_ref[...], preferred_element_type=F32)
    act = (gate / (1.0 + jnp.exp(-gate)) * up).astype(BF16)
    o_ref[...] += jnp.dot(act, wd_ref[...], preferred_element_type=F32)

    if final_norm:
        @pl.when(f == pl.num_programs(1) - 1)
        def _():
            o_ref[...] = _rms(o_ref[...], fn_ref[...])


def _ffn_cast_kernel(x_ref, g_ref, wg_ref, wu_ref, wd_ref, fn_ref, ng_ref, nu_ref, nd_ref,
                     o_ref, cg_ref, cu_ref, cd_ref, h_sc):
    cg_ref[...] = ng_ref[...].astype(BF16)
    cu_ref[...] = nu_ref[...].astype(BF16)
    cd_ref[...] = nd_ref[...].astype(BF16)
    _ffn_kernel(x_ref, g_ref, wg_ref, wu_ref, wd_ref, fn_ref, o_ref, h_sc, final_norm=False)


def _ffn(x, gain, wg, wu, wd, fn_gain, l, tm, tf, final_norm, next_f32=None):
    s, d = x.shape
    dff = wg.shape[1]
    ni, nf = s // tm, dff // tf
    in_specs = [pl.BlockSpec((tm, d), lambda i, f: (i, 0)),
                _layer_spec(gain, l, 2),
                pl.BlockSpec((d, tf), lambda i, f: (0, f)),
                pl.BlockSpec((d, tf), lambda i, f: (0, f)),
                pl.BlockSpec((tf, d), lambda i, f: (f, 0)),
                pl.BlockSpec((1, d), lambda i, f: (0, 0))]
    x_spec = pl.BlockSpec((tm, d), lambda i, f: (i, 0))
    x_shape = jax.ShapeDtypeStruct((s, d), F32)
    scratch = [pltpu.VMEM((tm, d), BF16)]
    params = _cparams("arbitrary", "arbitrary")
    if next_f32 is None:
        return pl.pallas_call(
            functools.partial(_ffn_kernel, final_norm=final_norm),
            grid=(ni, nf), in_specs=in_specs, out_specs=x_spec, out_shape=x_shape,
            scratch_shapes=scratch, compiler_params=params, name="ffn",
        )(x, gain, wg, wu, wd, fn_gain)
    assert not final_norm and d % ni == 0 and dff % nf == 0
    rb, cb = d // ni, dff // nf
    in_specs += [pl.BlockSpec((None, rb, cb), lambda i, f: (l + 1, i, f)),
                 pl.BlockSpec((None, rb, cb), lambda i, f: (l + 1, i, f)),
                 pl.BlockSpec((None, cb, rb), lambda i, f: (l + 1, f, i))]
    out_specs = [x_spec,
                 pl.BlockSpec((rb, cb), lambda i, f: (i, f)),
                 pl.BlockSpec((rb, cb), lambda i, f: (i, f)),
                 pl.BlockSpec((cb, rb), lambda i, f: (f, i))]
    out_shape = [x_shape, jax.ShapeDtypeStruct((d, dff), BF16), jax.ShapeDtypeStruct((d, dff), BF16),
                 jax.ShapeDtypeStruct((dff, d), BF16)]
    return pl.pallas_call(
        _ffn_cast_kernel,
        grid=(ni, nf), in_specs=in_specs, out_specs=out_specs, out_shape=out_shape,
        scratch_shapes=scratch, name="ffn_cast",
        compiler_params=_cparams("arbitrary", "arbitrary", vmem_limit=VMEM_LIMIT_FFN_CAST),
    )(x, gain, wg, wu, wd, fn_gain, *next_f32)


def _spread_rope_cols(w):
    z = jnp.zeros(w.shape[:-1] + (32,), w.dtype)
    return jnp.concatenate([w[..., :32], z, w[..., 32:], z], axis=-1)


def _prep_w_in(w_in):
    w_in_b = w_in.astype(BF16)
    kpe = _spread_rope_cols(w_in_b[..., Z_MAIN:Z_MAIN + QK_ROPE_DIM])
    tail = jnp.concatenate([w_in_b[..., Z_MAIN + QK_ROPE_DIM:], kpe], axis=-1)
    return w_in_b, tail


def _prep_w_uq(w_uq):
    l, r, _ = w_uq.shape
    w = w_uq.reshape(l, r, N_MLA_HEADS, QK_HEAD_DIM)
    w = jnp.concatenate([w[..., :QK_NOPE_DIM], _spread_rope_cols(w[..., QK_NOPE_DIM:])], axis=-1)
    return w.reshape(l, r, N_MLA_HEADS * QK_PAD).astype(BF16)


def _prep_w_ukv(w_ukv):
    l, r, _ = w_ukv.shape
    w = w_ukv.reshape(l, r, N_MLA_HEADS, 2 * HEAD_DIM)
    k_nope = w[..., :HEAD_DIM].reshape(l, r, MLA_WIDTH)
    v = w[..., HEAD_DIM:].reshape(l, r, MLA_WIDTH)
    return jnp.concatenate([k_nope, v], axis=-1).astype(BF16)


def _tiles(s):
    pick = lambda pref: pref if s % pref == 0 else s
    attn = pick(512)
    return dict(in_tm=pick(1024), mix_tm=attn, ret_chunk=128, attn_tq=attn,
                out_tm=pick(1024), ffn_tm=pick(1024), ffn_tf=512)


def kernel(x, positions, attn_norm, w_in, mla_q_norm, w_uq, mla_kv_norm, w_ukv, gmlp_v_norm, gmlp_w_s,
           gmlp_b_s, mix_norm, w_out, ffn_norm, w_gate, w_up, w_down, final_norm):
    b, s, d = x.shape
    assert b == 1, "batch is folded away; only BATCH == 1 is supported"
    depth = w_in.shape[0]
    t = _tiles(s)

    w_in_b, w_in_tail = _prep_w_in(w_in)
    w_uq_b = _prep_w_uq(w_uq)
    w_ukv_b = _prep_w_ukv(w_ukv)
    n_prep = 8 if s % 8 == 0 else 1
    cos_r, sin_r, cos_m, sin_m, w_gate_0, w_up_0 = _mixers(
        [_rope_tables_part(positions, n_prep), _cast_part(w_gate, 0, n_prep), _cast_part(w_up, 0, n_prep)],
        n_prep, name="prep")
    ffn_w = [w_gate_0, w_up_0, None]
    rows = lambda a: a.reshape(depth, 1, -1)
    attn_g, q_g, kv_g, gv_g, mix_g, ffn_g = map(
        rows, (attn_norm, mla_q_norm, mla_kv_norm, gmlp_v_norm, mix_norm, ffn_norm))
    b_s = gmlp_b_s[..., None]
    fn_g = final_norm.reshape(1, d)

    ret_tables = _retention_tables(t["ret_chunk"])
    xs = x.reshape(s, d)
    for l in range(depth):
        z, zt = _in_proj(xs, attn_g, w_in_b, w_in_tail, l, t["in_tm"])
        tm = t["mix_tm"]
        parts = [_retention_part(z, cos_r, sin_r, ret_tables, mix_g, l, tm, t["ret_chunk"]),
                 _gmlp_part(zt, gv_g, gmlp_w_s, b_s, mix_g, l, tm),
                 _mla_proj_part(z, zt, q_g, kv_g, w_uq_b, w_ukv_b, cos_m, sin_m, l, tm),
                 _cast_part(w_out, l, s // tm)]
        if l == 0:
            parts.append(_cast_part(w_down, 0, s // tm))
        y_ret, y_gm, qt, k, vt, w_out_l, *w_down_0 = _mixers(parts, s // tm)
        if l == 0:
            ffn_w[2] = w_down_0[0]
        y_mla = _attention(qt, k, vt, mix_g, l, t["attn_tq"])
        xs = _out_proj(xs, y_ret, y_mla, y_gm, w_out_l, t["out_tm"])
        if l < depth - 1:
            xs, *ffn_w = _ffn(xs, ffn_g, *ffn_w, fn_g, l, t["ffn_tm"], t["ffn_tf"], final_norm=False,
                              next_f32=(w_gate, w_up, w_down))
        else:
            xs = _ffn(xs, ffn_g, *ffn_w, fn_g, l, t["ffn_tm"], t["ffn_tf"], final_norm=True)
    return xs.reshape(b, s, d)
```

```python
import functools
import inspect

import jax
import jax.numpy as jnp
from jax import lax
from jax.experimental import pallas as pl
from jax.experimental.pallas import tpu as pltpu

F32 = jnp.float32
BF16 = jnp.bfloat16

HEAD_DIM = 128
N_RET_HEADS = 6
N_MLA_HEADS = 6
N_GMLP_GROUPS = 4
RET_WIDTH = N_RET_HEADS * HEAD_DIM
MLA_WIDTH = N_MLA_HEADS * HEAD_DIM
GMLP_WIDTH = N_GMLP_GROUPS * HEAD_DIM
Q_LORA_RANK = 512
KV_LORA_RANK = 512
QK_NOPE_DIM = 128
QK_ROPE_DIM = 64
QK_HEAD_DIM = QK_NOPE_DIM + QK_ROPE_DIM
CHUNK = 128
ROPE_BASE = 10000.0
EPS = 1e-6

LANES = 128
QK_PAD = 2 * LANES
ATTN_HEADS_PER_STEP = 3
VT_ROWS = HEAD_DIM + 16
LOG2_E = 1.4426950408889634
_DONE = object()

Z_CQ = 4 * RET_WIDTH
Z_CKV = Z_CQ + Q_LORA_RANK
Z_MAIN = Z_CKV + KV_LORA_RANK
Z_TN = 1024
ZT_GU = 0
ZT_GV = ZT_GU + GMLP_WIDTH
ZT_KPE = ZT_GV + GMLP_WIDTH
Z_TAIL = ZT_KPE + LANES

VMEM_LIMIT = 56 * 1024 * 1024
VMEM_LIMIT_FFN_CAST = 60 * 1024 * 1024


def _cparams(*sem, vmem_limit=VMEM_LIMIT):
    return pltpu.CompilerParams(dimension_semantics=sem, vmem_limit_bytes=vmem_limit)


def _layer_spec(arr, l, ngrid):
    zeros = (0,) * (arr.ndim - 1)
    return pl.BlockSpec((None,) + arr.shape[1:], lambda *_: (l,) + zeros)


def _rms(x, gain):
    return x * lax.rsqrt(jnp.mean(x * x, axis=-1, keepdims=True) + EPS) * gain


def _rope(x, cos, sin):
    return x * cos + pltpu.roll(x, 64, 1) * sin


def _rope_table_kernel(pos_ref, inv_r_ref, sgn_r_ref, inv_m_ref, cm_ref, sm_ref,
                       cos_r_ref, sin_r_ref, cos_m_ref, sin_m_ref):
    pos = pos_ref[...].astype(F32)
    ang_r = pos * inv_r_ref[...]
    cos_r_ref[...] = jnp.cos(ang_r)
    sin_r_ref[...] = jnp.sin(ang_r) * sgn_r_ref[...]
    ang_m = pos * inv_m_ref[...]
    cos_m_ref[...] = jnp.cos(ang_m) * cm_ref[...]
    sin_m_ref[...] = jnp.sin(ang_m) * sm_ref[...]


def _rope_tables_part(positions, n_tiles):
    s = positions.shape[-1]
    ts = s // n_tiles
    pos = positions.reshape(s, 1)
    inv_r = 1.0 / (ROPE_BASE ** (jnp.arange(0, HEAD_DIM, 2, dtype=F32) / HEAD_DIM))
    inv_m = 1.0 / (ROPE_BASE ** (jnp.arange(0, QK_ROPE_DIM, 2, dtype=F32) / QK_ROPE_DIM))
    z32 = jnp.zeros((32,), F32)
    o32 = jnp.ones((32,), F32)
    inv_r_full = jnp.concatenate([inv_r, inv_r]).reshape(1, LANES)
    sgn_r = jnp.concatenate([-jnp.ones((64,), F32), jnp.ones((64,), F32)]).reshape(1, LANES)
    inv_m_full = jnp.concatenate([inv_m, z32, inv_m, z32]).reshape(1, LANES)
    cmask = jnp.concatenate([o32, z32, o32, z32]).reshape(1, LANES)
    smask = jnp.concatenate([-o32, z32, o32, z32]).reshape(1, LANES)
    row = pl.BlockSpec((1, LANES), lambda i: (0, 0))
    tab = pl.BlockSpec((ts, LANES), lambda i: (i, 0))
    out = jax.ShapeDtypeStruct((s, LANES), F32)
    return (_rope_table_kernel,
            [pos, inv_r_full, sgn_r, inv_m_full, cmask, smask],
            [pl.BlockSpec((ts, 1), lambda i: (i, 0)), row, row, row, row, row],
            [tab, tab, tab, tab],
            [out, out, out, out],
            [])


def _in_proj_kernel(x_ref, g_ref, wm_ref, wt_ref, zm_ref, zt_ref, h_sc, *, n_main):
    j = pl.program_id(1)

    @pl.when(j == 0)
    def _():
        h_sc[...] = _rms(x_ref[...], g_ref[...]).astype(BF16)

    @pl.when(j < n_main)
    def _():
        zm_ref[...] = jnp.dot(h_sc[...], wm_ref[...], preferred_element_type=F32).astype(zm_ref.dtype)

    @pl.when(j == n_main)
    def _():
        zt_ref[...] = jnp.dot(h_sc[...], wt_ref[...], preferred_element_type=F32).astype(zt_ref.dtype)


def _in_proj(x, gain, w_in_b, w_tail, l, tm):
    s, d = x.shape
    n_main = Z_MAIN // Z_TN
    last = n_main - 1
    return pl.pallas_call(
        functools.partial(_in_proj_kernel, n_main=n_main),
        grid=(s // tm, n_main + 1),
        in_specs=[pl.BlockSpec((tm, d), lambda i, j: (i, 0)),
                  _layer_spec(gain, l, 2),
                  pl.BlockSpec((None, d, Z_TN), lambda i, j: (l, 0, jnp.minimum(j, last))),
                  _layer_spec(w_tail, l, 2)],
        out_specs=[pl.BlockSpec((tm, Z_TN), lambda i, j: (i, jnp.minimum(j, last))),
                   pl.BlockSpec((tm, Z_TAIL), lambda i, j: (i, 0))],
        out_shape=[jax.ShapeDtypeStruct((s, Z_MAIN), BF16),
                   jax.ShapeDtypeStruct((s, Z_TAIL), BF16)],
        scratch_shapes=[pltpu.VMEM((tm, d), BF16)],
        compiler_params=_cparams("arbitrary", "arbitrary"),
        name="in_proj",
    )(x, gain, w_in_b, w_tail)


def _retention_kernel(q_ref, k_ref, v_ref, g_ref, cos_ref, sin_ref, inner_ref, qdec_ref, kdec_ref,
                      cdec_ref, gain_ref, o_ref, state_sc, *, chunk, n_chunks):
    @pl.when(pl.program_id(0) == 0)
    def _():
        state_sc[...] = jnp.zeros_like(state_sc)

    for c in range(n_chunks):
        rows = slice(c * chunk, (c + 1) * chunk)
        cos = cos_ref[rows, :]
        sin = sin_ref[rows, :]
        for h in range(N_RET_HEADS):
            cols = slice(h * HEAD_DIM, (h + 1) * HEAD_DIM)
            q = _rope(q_ref[rows, cols].astype(F32), cos, sin)
            k = _rope(k_ref[rows, cols].astype(F32), cos, sin)
            v = v_ref[rows, cols]
            state = state_sc[h]
            scores = lax.dot_general(q.astype(BF16), k.astype(BF16), (((1,), (1,)), ((), ())),
                                     preferred_element_type=F32) * inner_ref[h]
            out = jnp.dot(scores.astype(BF16), v, preferred_element_type=F32)
            out += jnp.dot((q * qdec_ref[:, cols]).astype(BF16), state.astype(BF16),
                           preferred_element_type=F32)
            kt = (k * kdec_ref[:, cols]).T.astype(BF16)
            state_sc[h] = cdec_ref[:, cols] * state + jnp.dot(kt, v, preferred_element_type=F32)
            y = _rms(out, gain_ref[:, cols])
            gate = g_ref[rows, cols].astype(F32)
            o_ref[rows, cols] = (y * (gate / (1.0 + jnp.exp(-gate)))).astype(o_ref.dtype)
        yield


def _retention_tables(chunk):
    h = N_RET_HEADS
    log_gamma = jnp.log1p(-jnp.exp2(-5.0 - jnp.arange(h, dtype=F32)))
    idx = jnp.arange(chunk, dtype=F32)
    rel = idx[:, None] - idx[None, :]
    inner = jnp.where(rel >= 0, jnp.exp(log_gamma[:, None, None] * jnp.maximum(rel, 0.0)), 0.0)
    qdec = jnp.exp(log_gamma[None, :] * (idx[:, None] + 1.0))
    kdec = jnp.exp(log_gamma[None, :] * (chunk - 1.0 - idx[:, None]))
    cdec = jnp.exp(log_gamma * chunk)
    rep = lambda a: jnp.repeat(a, HEAD_DIM, axis=-1)
    scale = HEAD_DIM ** -0.5
    return inner * scale, rep(qdec), rep(kdec) * scale, rep(cdec[None, :])


def _retention_part(z, cos_r, sin_r, tables, mix, l, t, chunk):
    s = z.shape[0]
    zspec = lambda c: pl.BlockSpec((t, RET_WIDTH), lambda i: (i, c))
    tab = pl.BlockSpec((t, LANES), lambda i: (i, 0))
    full = lambda a: pl.BlockSpec(a.shape, lambda i: (0,) * a.ndim)
    return (functools.partial(_retention_kernel, chunk=chunk, n_chunks=t // chunk),
            [z, z, z, z, cos_r, sin_r, *tables, mix],
            [zspec(0), zspec(1), zspec(2), zspec(3), tab, tab] + [full(a) for a in tables]
            + [pl.BlockSpec((None, 1, RET_WIDTH), lambda i: (l, 0, 0))],
            [pl.BlockSpec((t, RET_WIDTH), lambda i: (i, 0))],
            [jax.ShapeDtypeStruct((s, RET_WIDTH), BF16)],
            [pltpu.VMEM((N_RET_HEADS, HEAD_DIM, HEAD_DIM), F32)])


def _mla_proj_kernel(cq_ref, ckv_ref, kpe_ref, gq_ref, gkv_ref, wuq_ref, wukv_ref, cos_ref, sin_ref,
                     qt_ref, k_ref, vt_ref):
    cos = cos_ref[...]
    sin = sin_ref[...]
    scale = QK_HEAD_DIM ** -0.5 * LOG2_E
    cq = _rms(cq_ref[...].astype(F32), gq_ref[...]).astype(BF16)
    mq = jnp.dot(cq, wuq_ref[...], preferred_element_type=F32)
    yield
    for h in range(N_MLA_HEADS):
        nope = slice(h * QK_PAD, h * QK_PAD + LANES)
        pe = slice(h * QK_PAD + LANES, (h + 1) * QK_PAD)
        qt_ref[nope, :] = (mq[:, nope] * scale).T.astype(BF16)
        qt_ref[pe, :] = (_rope(mq[:, pe], cos, sin) * scale).T.astype(BF16)
    yield
    ckv = _rms(ckv_ref[...].astype(F32), gkv_ref[...]).astype(BF16)
    mkv = jnp.dot(ckv, wukv_ref[...], preferred_element_type=F32)
    kpe = _rope(kpe_ref[...].astype(F32), cos, sin).astype(BF16)
    for h in range(N_MLA_HEADS):
        k_ref[:, h * QK_PAD:h * QK_PAD + LANES] = mkv[:, h * LANES:(h + 1) * LANES].astype(BF16)
        k_ref[:, h * QK_PAD + LANES:(h + 1) * QK_PAD] = kpe
    yield
    vt = mkv[:, MLA_WIDTH:].T.astype(BF16)
    tm = vt.shape[1]
    extra = lax.broadcasted_iota(jnp.int32, (VT_ROWS - HEAD_DIM, tm), 0)
    ones_row = jnp.where(extra == 0, 1.0, 0.0).astype(BF16)
    for h in range(N_MLA_HEADS):
        vt_ref[h * VT_ROWS:h * VT_ROWS + HEAD_DIM, :] = vt[h * HEAD_DIM:(h + 1) * HEAD_DIM, :]
        vt_ref[h * VT_ROWS + HEAD_DIM:(h + 1) * VT_ROWS, :] = ones_row


def _mla_proj_part(z, zt, gq, gkv, wuq, wukv, cos_m, sin_m, l, tm):
    s = z.shape[0]
    tab = pl.BlockSpec((tm, LANES), lambda i: (i, 0))
    qk = N_MLA_HEADS * QK_PAD
    return (_mla_proj_kernel,
            [z, z, zt, gq, gkv, wuq, wukv, cos_m, sin_m],
            [pl.BlockSpec((tm, Q_LORA_RANK), lambda i: (i, Z_CQ // Q_LORA_RANK)),
             pl.BlockSpec((tm, KV_LORA_RANK), lambda i: (i, Z_CKV // KV_LORA_RANK)),
             pl.BlockSpec((tm, LANES), lambda i: (i, ZT_KPE // LANES)),
             _layer_spec(gq, l, 1), _layer_spec(gkv, l, 1),
             _layer_spec(wuq, l, 1), _layer_spec(wukv, l, 1), tab, tab],
            [pl.BlockSpec((qk, tm), lambda i: (0, i)),
             pl.BlockSpec((tm, qk), lambda i: (i, 0)),
             pl.BlockSpec((None, N_MLA_HEADS * VT_ROWS, tm), lambda i: (i, 0, 0))],
            [jax.ShapeDtypeStruct((qk, s), BF16),
             jax.ShapeDtypeStruct((s, qk), BF16),
             jax.ShapeDtypeStruct((s // tm, N_MLA_HEADS * VT_ROWS, tm), BF16)],
            [])


def _attn_kernel(qt_ref, k_ref, vt_ref, gain_ref, o_ref, m_sc, acc_sc, st0_sc, st1_sc, *, tq, hp):
    qi = pl.program_id(1)
    m_sc[...] = jnp.full_like(m_sc, -jnp.inf)
    acc_sc[...] = jnp.zeros_like(acc_sc)

    heads = range(hp)

    def scores(kb, st_sc, js=heads):
        start = pl.multiple_of(kb * tq, tq)
        for j in js:
            qk_cols = slice(j * QK_PAD, (j + 1) * QK_PAD)
            st_sc[j] = jnp.dot(k_ref[pl.ds(start, tq), qk_cols], qt_ref[qk_cols, :],
                               preferred_element_type=F32)

    def softmax_pv(kb, st_sc, masked, js=heads):
        for j in js:
            vt = vt_ref[kb, j * VT_ROWS:(j + 1) * VT_ROWS, :]
            st = st_sc[j]
            if masked:
                kv_pos = lax.broadcasted_iota(jnp.int32, st.shape, 0)
                q_pos = lax.broadcasted_iota(jnp.int32, st.shape, 1)
                st = jnp.where(kv_pos <= q_pos, st, -jnp.inf)
            m_prev = m_sc[j]
            m_new = jnp.maximum(m_prev, jnp.max(st, axis=0, keepdims=True))
            alpha = jnp.exp2(m_prev - m_new)
            p = jnp.exp2(st - m_new).astype(BF16)
            acc_sc[j] = alpha * acc_sc[j] + jnp.dot(vt, p, preferred_element_type=F32)
            m_sc[j] = m_new

    scores(0, st0_sc)

    def pair(i, carry):
        kb = 2 * i
        for j in heads:
            scores(kb + 1, st1_sc, [j])
            softmax_pv(kb, st0_sc, False, [j])
        for j in heads:
            scores(kb + 2, st0_sc, [j])
            softmax_pv(kb + 1, st1_sc, False, [j])
        return carry

    lax.fori_loop(0, qi // 2, pair, 0)

    @pl.when(qi % 2 == 0)
    def _():
        softmax_pv(qi, st0_sc, True)

    @pl.when(qi % 2 == 1)
    def _():
        for j in heads:
            scores(qi, st1_sc, [j])
            softmax_pv(qi - 1, st0_sc, False, [j])
        softmax_pv(qi, st1_sc, True)

    for j in range(hp):
        cols = slice(j * HEAD_DIM, (j + 1) * HEAD_DIM)
        denom = acc_sc[j, HEAD_DIM:HEAD_DIM + 1, :]
        o = (acc_sc[j, :HEAD_DIM, :] * (1.0 / denom)).T
        o_ref[:, cols] = _rms(o, gain_ref[:, cols]).astype(o_ref.dtype)


def _attention(qt, k, vt, mix, l, tq):
    s = k.shape[0]
    hp = ATTN_HEADS_PER_STEP
    width = hp * HEAD_DIM
    return pl.pallas_call(
        functools.partial(_attn_kernel, tq=tq, hp=hp),
        grid=(N_MLA_HEADS // hp, s // tq),
        in_specs=[pl.BlockSpec((hp * QK_PAD, tq), lambda h, i: (h, i)),
                  pl.BlockSpec((s, hp * QK_PAD), lambda h, i: (0, h)),
                  pl.BlockSpec((s // tq, hp * VT_ROWS, tq), lambda h, i: (0, h, 0)),
                  pl.BlockSpec((None, 1, width), lambda h, i: (l, 0, RET_WIDTH // width + h))],
        out_specs=pl.BlockSpec((tq, width), lambda h, i: (i, h)),
        out_shape=jax.ShapeDtypeStruct((s, MLA_WIDTH), BF16),
        scratch_shapes=[pltpu.VMEM((hp, 1, tq), F32),
                        pltpu.VMEM((hp, VT_ROWS, tq), F32),
                        pltpu.VMEM((hp, tq, tq), F32), pltpu.VMEM((hp, tq, tq), F32)],
        compiler_params=_cparams("arbitrary", "arbitrary"),
        name="mla_attention",
    )(qt, k, vt, mix)


def _gmlp_kernel(u_ref, v_ref, gv_ref, ws_ref, bs_ref, gm_ref, o_ref, *, n_chunks):
    row = lax.broadcasted_iota(jnp.int32, (CHUNK, CHUNK), 0)
    col = lax.broadcasted_iota(jnp.int32, (CHUNK, CHUNK), 1)
    for g in range(N_GMLP_GROUPS):
        cols = slice(g * HEAD_DIM, (g + 1) * HEAD_DIM)
        w = jnp.where(col <= row, ws_ref[g], 0.0).astype(BF16)
        bias = bs_ref[g]
        for c in range(n_chunks):
            rows = slice(c * CHUNK, (c + 1) * CHUNK)
            u = jax.nn.gelu(u_ref[rows, cols].astype(F32))
            v = jax.nn.gelu(v_ref[rows, cols].astype(F32))
            vn = _rms(v, gv_ref[:, cols]).astype(BF16)
            sg = jnp.dot(w, vn, preferred_element_type=F32) + bias
            o_ref[rows, cols] = _rms(u * sg, gm_ref[:, cols]).astype(o_ref.dtype)
        yield


def _gmlp_part(zt, gv, ws, bs, mix, l, t):
    s = zt.shape[0]
    return (functools.partial(_gmlp_kernel, n_chunks=t // CHUNK),
            [zt, zt, gv, ws, bs, mix],
            [pl.BlockSpec((t, GMLP_WIDTH), lambda i: (i, ZT_GU // GMLP_WIDTH)),
             pl.BlockSpec((t, GMLP_WIDTH), lambda i: (i, ZT_GV // GMLP_WIDTH)),
             _layer_spec(gv, l, 1), _layer_spec(ws, l, 1), _layer_spec(bs, l, 1),
             pl.BlockSpec((None, 1, GMLP_WIDTH),
                          lambda i: (l, 0, (RET_WIDTH + MLA_WIDTH) // GMLP_WIDTH))],
            [pl.BlockSpec((t, GMLP_WIDTH), lambda i: (i, 0))],
            [jax.ShapeDtypeStruct((s, GMLP_WIDTH), BF16)],
            [])


def _cast_kernel(w_ref, o_ref):
    o_ref[...] = w_ref[...].astype(o_ref.dtype)


def _cast_part(w, l, n_tiles):
    _, r, c = w.shape
    assert r % n_tiles == 0
    rb = r // n_tiles
    return (_cast_kernel, [w],
            [pl.BlockSpec((None, rb, c), lambda i: (l, i, 0))],
            [pl.BlockSpec((rb, c), lambda i: (i, 0))],
            [jax.ShapeDtypeStruct((r, c), BF16)],
            [])


def _mixers_kernel(*refs, bodies, n_in, n_out, n_scratch):
    ins, outs, scr = [], [], []
    pos = 0
    for group, counts in ((ins, n_in), (outs, n_out), (scr, n_scratch)):
        for c in counts:
            group.append(refs[pos:pos + c])
            pos += c
    staged = [body(*i, *o, *sc) for body, i, o, sc in zip(bodies, ins, outs, scr)]
    staged = [g for g in staged if inspect.isgenerator(g)]
    while staged:
        for g in list(staged):
            if next(g, _DONE) is _DONE:
                staged.remove(g)


def _mixers(parts, n_tiles, name="mixers"):
    bodies = [p[0] for p in parts]
    flat = lambda k: [a for p in parts for a in p[k]]
    return pl.pallas_call(
        functools.partial(_mixers_kernel, bodies=bodies, n_in=[len(p[1]) for p in parts],
                          n_out=[len(p[3]) for p in parts], n_scratch=[len(p[5]) for p in parts]),
        grid=(n_tiles,),
        in_specs=flat(2), out_specs=flat(3), out_shape=flat(4), scratch_shapes=flat(5),
        compiler_params=_cparams("arbitrary"),
        name=name,
    )(*flat(1))


OUT_PROJ_SLOTS = 3


def _out_proj_kernel(x_hbm, yr_hbm, ym_hbm, yg_hbm, w_ref, o_ref, xb, yrb, ymb, ygb, sem, *, tm, n):
    i = pl.program_id(0)
    streams = ((x_hbm, xb), (yr_hbm, yrb), (ym_hbm, ymb), (yg_hbm, ygb))

    def copies(tile, slot):
        rows = pl.ds(pl.multiple_of(tile * tm, tm), tm)
        return [pltpu.make_async_copy(src.at[rows, :], buf.at[slot], sem.at[k, slot])
                for k, (src, buf) in enumerate(streams)]

    @pl.when(i == 0)
    def _():
        for t in range(min(OUT_PROJ_SLOTS - 1, n)):
            for c in copies(t, t):
                c.start()

    @pl.when(i + OUT_PROJ_SLOTS - 1 < n)
    def _():
        ahead = i + OUT_PROJ_SLOTS - 1
        for c in copies(ahead, ahead % OUT_PROJ_SLOTS):
            c.start()

    slot = i % OUT_PROJ_SLOTS
    for c in copies(i, slot):
        c.wait()
    y = jnp.concatenate([yrb[slot], ymb[slot], ygb[slot]], axis=-1)
    o_ref[...] = xb[slot] + jnp.dot(y, w_ref[...], preferred_element_type=F32)


def _out_proj(x, yr, ym, yg, w, tm):
    s, d = x.shape
    n = s // tm
    hbm = pl.BlockSpec(memory_space=pl.ANY)
    ring = lambda width, dtype: pltpu.VMEM((OUT_PROJ_SLOTS, tm, width), dtype)
    return pl.pallas_call(
        functools.partial(_out_proj_kernel, tm=tm, n=n),
        grid=(n,),
        in_specs=[hbm, hbm, hbm, hbm,
                  pl.BlockSpec(w.shape, lambda i: (0, 0), pipeline_mode=pl.Buffered(1))],
        out_specs=pl.BlockSpec((tm, d), lambda i: (i, 0)),
        out_shape=jax.ShapeDtypeStruct((s, d), F32),
        scratch_shapes=[ring(d, F32), ring(RET_WIDTH, BF16), ring(MLA_WIDTH, BF16),
                        ring(GMLP_WIDTH, BF16), pltpu.SemaphoreType.DMA((4, OUT_PROJ_SLOTS))],
        compiler_params=_cparams("arbitrary"),
        name="out_proj",
    )(x, yr, ym, yg, w)


def _ffn_kernel(x_ref, g_ref, wg_ref, wu_ref, wd_ref, fn_ref, o_ref, h_sc, *, final_norm):
    f = pl.program_id(1)

    @pl.when(f == 0)
    def _():
        x = x_ref[...]
        h_sc[...] = _rms(x, g_ref[...]).astype(BF16)
        o_ref[...] = x

    h = h_sc[...]
    gate = jnp.dot(h, wg_ref[...], preferred_element_type=F32)
    up = jnp.dot(h, wu_ref[...], preferred_element_type=F32)
    act = (gate / (1.0 + jnp.exp(-gate)) * up).astype(BF16)
    o_ref[...] += jnp.dot(act, wd_ref[...], preferred_element_type=F32)

    if final_norm:
        @pl.when(f == pl.num_programs(1) - 1)
        def _():
            o_ref[...] = _rms(o_ref[...], fn_ref[...])


def _ffn_cast_kernel(x_ref, g_ref, wg_ref, wu_ref, wd_ref, fn_ref, ng_ref, nu_ref, nd_ref,
                     o_ref, cg_ref, cu_ref, cd_ref, h_sc):
    cg_ref[...] = ng_ref[...].astype(BF16)
    cu_ref[...] = nu_ref[...].astype(BF16)
    cd_ref[...] = nd_ref[...].astype(BF16)
    _ffn_kernel(x_ref, g_ref, wg_ref, wu_ref, wd_ref, fn_ref, o_ref, h_sc, final_norm=False)


def _ffn(x, gain, wg, wu, wd, fn_gain, l, tm, tf, final_norm, next_f32=None):
    s, d = x.shape
    dff = wg.shape[1]
    ni, nf = s // tm, dff // tf
    in_specs = [pl.BlockSpec((tm, d), lambda i, f: (i, 0)),
                _layer_spec(gain, l, 2),
                pl.BlockSpec((d, tf), lambda i, f: (0, f)),
                pl.BlockSpec((d, tf), lambda i, f: (0, f)),
                pl.BlockSpec((tf, d), lambda i, f: (f, 0)),
                pl.BlockSpec((1, d), lambda i, f: (0, 0))]
    x_spec = pl.BlockSpec((tm, d), lambda i, f: (i, 0))
    x_shape = jax.ShapeDtypeStruct((s, d), F32)
    scratch = [pltpu.VMEM((tm, d), BF16)]
    params = _cparams("arbitrary", "arbitrary")
    if next_f32 is None:
        return pl.pallas_call(
            functools.partial(_ffn_kernel, final_norm=final_norm),
            grid=(ni, nf), in_specs=in_specs, out_specs=x_spec, out_shape=x_shape,
            scratch_shapes=scratch, compiler_params=params, name="ffn",
        )(x, gain, wg, wu, wd, fn_gain)
    assert not final_norm and d % ni == 0 and dff % nf == 0
    rb, cb = d // ni, dff // nf
    in_specs += [pl.BlockSpec((None, rb, cb), lambda i, f: (l + 1, i, f)),
                 pl.BlockSpec((None, rb, cb), lambda i, f: (l + 1, i, f)),
                 pl.BlockSpec((None, cb, rb), lambda i, f: (l + 1, f, i))]
    out_specs = [x_spec,
                 pl.BlockSpec((rb, cb), lambda i, f: (i, f)),
                 pl.BlockSpec((rb, cb), lambda i, f: (i, f)),
                 pl.BlockSpec((cb, rb), lambda i, f: (f, i))]
    out_shape = [x_shape, jax.ShapeDtypeStruct((d, dff), BF16), jax.ShapeDtypeStruct((d, dff), BF16),
                 jax.ShapeDtypeStruct((dff, d), BF16)]
    return pl.pallas_call(
        _ffn_cast_kernel,
        grid=(ni, nf), in_specs=in_specs, out_specs=out_specs, out_shape=out_shape,
        scratch_shapes=scratch, name="ffn_cast",
        compiler_params=_cparams("arbitrary", "arbitrary", vmem_limit=VMEM_LIMIT_FFN_CAST),
    )(x, gain, wg, wu, wd, fn_gain, *next_f32)


def _spread_rope_cols(w):
    z = jnp.zeros(w.shape[:-1] + (32,), w.dtype)
    return jnp.concatenate([w[..., :32], z, w[..., 32:], z], axis=-1)


def _prep_w_in(w_in):
    w_in_b = w_in.astype(BF16)
    kpe = _spread_rope_cols(w_in_b[..., Z_MAIN:Z_MAIN + QK_ROPE_DIM])
    tail = jnp.concatenate([w_in_b[..., Z_MAIN + QK_ROPE_DIM:], kpe], axis=-1)
    return w_in_b, tail


def _prep_w_uq(w_uq):
    l, r, _ = w_uq.shape
    w = w_uq.reshape(l, r, N_MLA_HEADS, QK_HEAD_DIM)
    w = jnp.concatenate([w[..., :QK_NOPE_DIM], _spread_rope_cols(w[..., QK_NOPE_DIM:])], axis=-1)
    return w.reshape(l, r, N_MLA_HEADS * QK_PAD).astype(BF16)


def _prep_w_ukv(w_ukv):
    l, r, _ = w_ukv.shape
    w = w_ukv.reshape(l, r, N_MLA_HEADS, 2 * HEAD_DIM)
    k_nope = w[..., :HEAD_DIM].reshape(l, r, MLA_WIDTH)
    v = w[..., HEAD_DIM:].reshape(l, r, MLA_WIDTH)
    return jnp.concatenate([k_nope, v], axis=-1).astype(BF16)


def _tiles(s):
    pick = lambda pref: pref if s % pref == 0 else s
    attn = pick(512)
    return dict(in_tm=pick(1024), mix_tm=attn, ret_chunk=128, attn_tq=attn,
                out_tm=pick(512), ffn_tm=pick(1024), ffn_tf=512)


def kernel(x, positions, attn_norm, w_in, mla_q_norm, w_uq, mla_kv_norm, w_ukv, gmlp_v_norm, gmlp_w_s,
           gmlp_b_s, mix_norm, w_out, ffn_norm, w_gate, w_up, w_down, final_norm):
    b, s, d = x.shape
    assert b == 1, "batch is folded away; only BATCH == 1 is supported"
    depth = w_in.shape[0]
    t = _tiles(s)

    w_in_b, w_in_tail = _prep_w_in(w_in)
    w_uq_b = _prep_w_uq(w_uq)
    w_ukv_b = _prep_w_ukv(w_ukv)
    n_prep = 8 if s % 8 == 0 else 1
    cos_r, sin_r, cos_m, sin_m, w_gate_0, w_up_0 = _mixers(
        [_rope_tables_part(positions, n_prep), _cast_part(w_gate, 0, n_prep), _cast_part(w_up, 0, n_prep)],
        n_prep, name="prep")
    ffn_w = [w_gate_0, w_up_0, None]
    rows = lambda a: a.reshape(depth, 1, -1)
    attn_g, q_g, kv_g, gv_g, mix_g, ffn_g = map(
        rows, (attn_norm, mla_q_norm, mla_kv_norm, gmlp_v_norm, mix_norm, ffn_norm))
    b_s = gmlp_b_s[..., None]
    fn_g = final_norm.reshape(1, d)

    ret_tables = _retention_tables(t["ret_chunk"])
    xs = x.reshape(s, d)
    for l in range(depth):
        z, zt = _in_proj(xs, attn_g, w_in_b, w_in_tail, l, t["in_tm"])
        tm = t["mix_tm"]
        parts = [_retention_part(z, cos_r, sin_r, ret_tables, mix_g, l, tm, t["ret_chunk"]),
                 _gmlp_part(zt, gv_g, gmlp_w_s, b_s, mix_g, l, tm),
                 _mla_proj_part(z, zt, q_g, kv_g, w_uq_b, w_ukv_b, cos_m, sin_m, l, tm),
                 _cast_part(w_out, l, s // tm)]
        if l == 0:
            parts.append(_cast_part(w_down, 0, s // tm))
        y_ret, y_gm, qt, k, vt, w_out_l, *w_down_0 = _mixers(parts, s // tm)
        if l == 0:
            ffn_w[2] = w_down_0[0]
        y_mla = _attention(qt, k, vt, mix_g, l, t["attn_tq"])
        xs = _out_proj(xs, y_ret, y_mla, y_gm, w_out_l, t["out_tm"])
        if l < depth - 1:
            xs, *ffn_w = _ffn(xs, ffn_g, *ffn_w, fn_g, l, t["ffn_tm"], t["ffn_tf"], final_norm=False,
                              next_f32=(w_gate, w_up, w_down))
        else:
            xs = _ffn(xs, ffn_g, *ffn_w, fn_g, l, t["ffn_tm"], t["ffn_tf"], final_norm=True)
    return xs.reshape(b, s, d)
```

```python
import functools
import inspect

import jax
import jax.numpy as jnp
from jax import lax
from jax.experimental import pallas as pl
from jax.experimental.pallas import tpu as pltpu

F32 = jnp.float32
BF16 = jnp.bfloat16

HEAD_DIM = 128
N_RET_HEADS = 6
N_MLA_HEADS = 6
N_GMLP_GROUPS = 4
RET_WIDTH = N_RET_HEADS * HEAD_DIM
MLA_WIDTH = N_MLA_HEADS * HEAD_DIM
GMLP_WIDTH = N_GMLP_GROUPS * HEAD_DIM
Q_LORA_RANK = 512
KV_LORA_RANK = 512
QK_NOPE_DIM = 128
QK_ROPE_DIM = 64
QK_HEAD_DIM = QK_NOPE_DIM + QK_ROPE_DIM
CHUNK = 128
ROPE_BASE = 10000.0
EPS = 1e-6

LANES = 128
QK_PAD = 2 * LANES
ATTN_HEADS_PER_STEP = 3
VT_ROWS = HEAD_DIM + 16
LOG2_E = 1.4426950408889634
_DONE = object()

Z_CQ = 4 * RET_WIDTH
Z_CKV = Z_CQ + Q_LORA_RANK
Z_MAIN = Z_CKV + KV_LORA_RANK
Z_TN = 1024
ZT_GU = 0
ZT_GV = ZT_GU + GMLP_WIDTH
ZT_KPE = ZT_GV + GMLP_WIDTH
Z_TAIL = ZT_KPE + LANES

VMEM_LIMIT = 56 * 1024 * 1024
VMEM_LIMIT_FFN_CAST = 60 * 1024 * 1024


def _cparams(*sem, vmem_limit=VMEM_LIMIT):
    return pltpu.CompilerParams(dimension_semantics=sem, vmem_limit_bytes=vmem_limit)


def _layer_spec(arr, l, ngrid):
    zeros = (0,) * (arr.ndim - 1)
    return pl.BlockSpec((None,) + arr.shape[1:], lambda *_: (l,) + zeros)


def _rms(x, gain):
    return x * lax.rsqrt(jnp.mean(x * x, axis=-1, keepdims=True) + EPS) * gain


def _rope(x, cos, sin):
    return x * cos + pltpu.roll(x, 64, 1) * sin


def _rope_table_kernel(pos_ref, inv_r_ref, sgn_r_ref, inv_m_ref, cm_ref, sm_ref,
                       cos_r_ref, sin_r_ref, cos_m_ref, sin_m_ref):
    pos = pos_ref[...].astype(F32)
    ang_r = pos * inv_r_ref[...]
    cos_r_ref[...] = jnp.cos(ang_r)
    sin_r_ref[...] = jnp.sin(ang_r) * sgn_r_ref[...]
    ang_m = pos * inv_m_ref[...]
    cos_m_ref[...] = jnp.cos(ang_m) * cm_ref[...]
    sin_m_ref[...] = jnp.sin(ang_m) * sm_ref[...]


def _rope_tables_part(positions, n_tiles):
    s = positions.shape[-1]
    ts = s // n_tiles
    pos = positions.reshape(s, 1)
    inv_r = 1.0 / (ROPE_BASE ** (jnp.arange(0, HEAD_DIM, 2, dtype=F32) / HEAD_DIM))
    inv_m = 1.0 / (ROPE_BASE ** (jnp.arange(0, QK_ROPE_DIM, 2, dtype=F32) / QK_ROPE_DIM))
    z32 = jnp.zeros((32,), F32)
    o32 = jnp.ones((32,), F32)
    inv_r_full = jnp.concatenate([inv_r, inv_r]).reshape(1, LANES)
    sgn_r = jnp.concatenate([-jnp.ones((64,), F32), jnp.ones((64,), F32)]).reshape(1, LANES)
    inv_m_full = jnp.concatenate([inv_m, z32, inv_m, z32]).reshape(1, LANES)
    cmask = jnp.concatenate([o32, z32, o32, z32]).reshape(1, LANES)
    smask = jnp.concatenate([-o32, z32, o32, z32]).reshape(1, LANES)
    row = pl.BlockSpec((1, LANES), lambda i: (0, 0))
    tab = pl.BlockSpec((ts, LANES), lambda i: (i, 0))
    out = jax.ShapeDtypeStruct((s, LANES), F32)
    return (_rope_table_kernel,
            [pos, inv_r_full, sgn_r, inv_m_full, cmask, smask],
            [pl.BlockSpec((ts, 1), lambda i: (i, 0)), row, row, row, row, row],
            [tab, tab, tab, tab],
            [out, out, out, out],
            [])


def _in_proj_kernel(x_ref, g_ref, wm_ref, wt_ref, zm_ref, zt_ref, h_sc, *, n_main):
    j = pl.program_id(1)

    @pl.when(j == 0)
    def _():
        h = _rms(x_ref[...], g_ref[...]).astype(BF16)
        h_sc[...] = h
        zm_ref[...] = jnp.dot(h, wm_ref[...], preferred_element_type=F32).astype(zm_ref.dtype)

    @pl.when(jnp.logical_and(j > 0, j < n_main))
    def _():
        zm_ref[...] = jnp.dot(h_sc[...], wm_ref[...], preferred_element_type=F32).astype(zm_ref.dtype)

    @pl.when(j == n_main)
    def _():
        zt_ref[...] = jnp.dot(h_sc[...], wt_ref[...], preferred_element_type=F32).astype(zt_ref.dtype)


def _in_proj(x, gain, w_in_b, w_tail, l, tm):
    s, d = x.shape
    n_main = Z_MAIN // Z_TN
    last = n_main - 1
    return pl.pallas_call(
        functools.partial(_in_proj_kernel, n_main=n_main),
        grid=(s // tm, n_main + 1),
        in_specs=[pl.BlockSpec((tm, d), lambda i, j: (i, 0)),
                  _layer_spec(gain, l, 2),
                  pl.BlockSpec((None, d, Z_TN), lambda i, j: (l, 0, jnp.minimum(j, last))),
                  _layer_spec(w_tail, l, 2)],
        out_specs=[pl.BlockSpec((tm, Z_TN), lambda i, j: (i, jnp.minimum(j, last))),
                   pl.BlockSpec((tm, Z_TAIL), lambda i, j: (i, 0))],
        out_shape=[jax.ShapeDtypeStruct((s, Z_MAIN), BF16),
                   jax.ShapeDtypeStruct((s, Z_TAIL), BF16)],
        scratch_shapes=[pltpu.VMEM((tm, d), BF16)],
        compiler_params=_cparams("arbitrary", "arbitrary"),
        name="in_proj",
    )(x, gain, w_in_b, w_tail)


def _retention_kernel(q_ref, k_ref, v_ref, g_ref, cos_ref, sin_ref, inner_ref, qdec_ref, kdec_ref,
                      cdec_ref, gain_ref, o_ref, state_sc, *, chunk, n_chunks):
    @pl.when(pl.program_id(0) == 0)
    def _():
        state_sc[...] = jnp.zeros_like(state_sc)

    for c in range(n_chunks):
        rows = slice(c * chunk, (c + 1) * chunk)
        cos = cos_ref[rows, :]
        sin = sin_ref[rows, :]
        for h in range(N_RET_HEADS):
            cols = slice(h * HEAD_DIM, (h + 1) * HEAD_DIM)
            q = _rope(q_ref[rows, cols].astype(F32), cos, sin)
            k = _rope(k_ref[rows, cols].astype(F32), cos, sin)
            v = v_ref[rows, cols]
            state = state_sc[h]
            scores = lax.dot_general(q.astype(BF16), k.astype(BF16), (((1,), (1,)), ((), ())),
                                     preferred_element_type=F32) * inner_ref[h]
            out = jnp.dot(scores.astype(BF16), v, preferred_element_type=F32)
            out += jnp.dot((q * qdec_ref[:, cols]).astype(BF16), state.astype(BF16),
                           preferred_element_type=F32)
            kt = (k * kdec_ref[:, cols]).T.astype(BF16)
            state_sc[h] = cdec_ref[:, cols] * state + jnp.dot(kt, v, preferred_element_type=F32)
            y = _rms(out, gain_ref[:, cols])
            gate = g_ref[rows, cols].astype(F32)
            o_ref[rows, cols] = (y * (gate / (1.0 + jnp.exp(-gate)))).astype(o_ref.dtype)
        yield


def _retention_tables(chunk):
    h = N_RET_HEADS
    log_gamma = jnp.log1p(-jnp.exp2(-5.0 - jnp.arange(h, dtype=F32)))
    idx = jnp.arange(chunk, dtype=F32)
    rel = idx[:, None] - idx[None, :]
    inner = jnp.where(rel >= 0, jnp.exp(log_gamma[:, None, None] * jnp.maximum(rel, 0.0)), 0.0)
    qdec = jnp.exp(log_gamma[None, :] * (idx[:, None] + 1.0))
    kdec = jnp.exp(log_gamma[None, :] * (chunk - 1.0 - idx[:, None]))
    cdec = jnp.exp(log_gamma * chunk)
    rep = lambda a: jnp.repeat(a, HEAD_DIM, axis=-1)
    scale = HEAD_DIM ** -0.5
    return inner * scale, rep(qdec), rep(kdec) * scale, rep(cdec[None, :])


def _retention_part(z, cos_r, sin_r, tables, mix, l, t, chunk):
    s = z.shape[0]
    zspec = lambda c: pl.BlockSpec((t, RET_WIDTH), lambda i: (i, c))
    tab = pl.BlockSpec((t, LANES), lambda i: (i, 0))
    full = lambda a: pl.BlockSpec(a.shape, lambda i: (0,) * a.ndim)
    return (functools.partial(_retention_kernel, chunk=chunk, n_chunks=t // chunk),
            [z, z, z, z, cos_r, sin_r, *tables, mix],
            [zspec(0), zspec(1), zspec(2), zspec(3), tab, tab] + [full(a) for a in tables]
            + [pl.BlockSpec((None, 1, RET_WIDTH), lambda i: (l, 0, 0))],
            [pl.BlockSpec((t, RET_WIDTH), lambda i: (i, 0))],
            [jax.ShapeDtypeStruct((s, RET_WIDTH), BF16)],
            [pltpu.VMEM((N_RET_HEADS, HEAD_DIM, HEAD_DIM), F32)])


def _mla_proj_kernel(cq_ref, ckv_ref, kpe_ref, gq_ref, gkv_ref, wuq_ref, wukv_ref, cos_ref, sin_ref,
                     qt_ref, k_ref, vt_ref):
    cos = cos_ref[...]
    sin = sin_ref[...]
    scale = QK_HEAD_DIM ** -0.5 * LOG2_E
    cq = _rms(cq_ref[...].astype(F32), gq_ref[...]).astype(BF16)
    mq = jnp.dot(cq, wuq_ref[...], preferred_element_type=F32)
    yield
    for h in range(N_MLA_HEADS):
        nope = slice(h * QK_PAD, h * QK_PAD + LANES)
        pe = slice(h * QK_PAD + LANES, (h + 1) * QK_PAD)
        qt_ref[nope, :] = (mq[:, nope] * scale).T.astype(BF16)
        qt_ref[pe, :] = (_rope(mq[:, pe], cos, sin) * scale).T.astype(BF16)
    yield
    ckv = _rms(ckv_ref[...].astype(F32), gkv_ref[...]).astype(BF16)
    mkv = jnp.dot(ckv, wukv_ref[...], preferred_element_type=F32)
    kpe = _rope(kpe_ref[...].astype(F32), cos, sin).astype(BF16)
    for h in range(N_MLA_HEADS):
        k_ref[:, h * QK_PAD:h * QK_PAD + LANES] = mkv[:, h * LANES:(h + 1) * LANES].astype(BF16)
        k_ref[:, h * QK_PAD + LANES:(h + 1) * QK_PAD] = kpe
    yield
    vt = mkv[:, MLA_WIDTH:].T.astype(BF16)
    tm = vt.shape[1]
    extra = lax.broadcasted_iota(jnp.int32, (VT_ROWS - HEAD_DIM, tm), 0)
    ones_row = jnp.where(extra == 0, 1.0, 0.0).astype(BF16)
    for h in range(N_MLA_HEADS):
        vt_ref[h * VT_ROWS:h * VT_ROWS + HEAD_DIM, :] = vt[h * HEAD_DIM:(h + 1) * HEAD_DIM, :]
        vt_ref[h * VT_ROWS + HEAD_DIM:(h + 1) * VT_ROWS, :] = ones_row


def _mla_proj_part(z, zt, gq, gkv, wuq, wukv, cos_m, sin_m, l, tm):
    s = z.shape[0]
    tab = pl.BlockSpec((tm, LANES), lambda i: (i, 0))
    qk = N_MLA_HEADS * QK_PAD
    return (_mla_proj_kernel,
            [z, z, zt, gq, gkv, wuq, wukv, cos_m, sin_m],
            [pl.BlockSpec((tm, Q_LORA_RANK), lambda i: (i, Z_CQ // Q_LORA_RANK)),
             pl.BlockSpec((tm, KV_LORA_RANK), lambda i: (i, Z_CKV // KV_LORA_RANK)),
             pl.BlockSpec((tm, LANES), lambda i: (i, ZT_KPE // LANES)),
             _layer_spec(gq, l, 1), _layer_spec(gkv, l, 1),
             _layer_spec(wuq, l, 1), _layer_spec(wukv, l, 1), tab, tab],
            [pl.BlockSpec((qk, tm), lambda i: (0, i)),
             pl.BlockSpec((tm, qk), lambda i: (i, 0)),
             pl.BlockSpec((None, N_MLA_HEADS * VT_ROWS, tm), lambda i: (i, 0, 0))],
            [jax.ShapeDtypeStruct((qk, s), BF16),
             jax.ShapeDtypeStruct((s, qk), BF16),
             jax.ShapeDtypeStruct((s // tm, N_MLA_HEADS * VT_ROWS, tm), BF16)],
            [])


def _attn_kernel(qt_ref, k_ref, vt_ref, gain_ref, o_ref, m_sc, acc_sc, st0_sc, st1_sc, *, tq, hp):
    qi = pl.program_id(1)
    m_sc[...] = jnp.full_like(m_sc, -jnp.inf)
    acc_sc[...] = jnp.zeros_like(acc_sc)

    heads = range(hp)

    def scores(kb, st_sc, js=heads):
        start = pl.multiple_of(kb * tq, tq)
        for j in js:
            qk_cols = slice(j * QK_PAD, (j + 1) * QK_PAD)
            st_sc[j] = jnp.dot(k_ref[pl.ds(start, tq), qk_cols], qt_ref[qk_cols, :],
                               preferred_element_type=F32)

    def softmax_pv(kb, st_sc, masked, js=heads):
        for j in js:
            vt = vt_ref[kb, j * VT_ROWS:(j + 1) * VT_ROWS, :]
            st = st_sc[j]
            if masked:
                kv_pos = lax.broadcasted_iota(jnp.int32, st.shape, 0)
                q_pos = lax.broadcasted_iota(jnp.int32, st.shape, 1)
                st = jnp.where(kv_pos <= q_pos, st, -jnp.inf)
            m_prev = m_sc[j]
            m_new = jnp.maximum(m_prev, jnp.max(st, axis=0, keepdims=True))
            alpha = jnp.exp2(m_prev - m_new)
            p = jnp.exp2(st - m_new).astype(BF16)
            acc_sc[j] = alpha * acc_sc[j] + jnp.dot(vt, p, preferred_element_type=F32)
            m_sc[j] = m_new

    scores(0, st0_sc)

    def pair(i, carry):
        kb = 2 * i
        for j in heads:
            scores(kb + 1, st1_sc, [j])
            softmax_pv(kb, st0_sc, False, [j])
        for j in heads:
            scores(kb + 2, st0_sc, [j])
            softmax_pv(kb + 1, st1_sc, False, [j])
        return carry

    lax.fori_loop(0, qi // 2, pair, 0)

    @pl.when(qi % 2 == 0)
    def _():
        softmax_pv(qi, st0_sc, True)

    @pl.when(qi % 2 == 1)
    def _():
        for j in heads:
            scores(qi, st1_sc, [j])
            softmax_pv(qi - 1, st0_sc, False, [j])
        softmax_pv(qi, st1_sc, True)

    for j in range(hp):
        cols = slice(j * HEAD_DIM, (j + 1) * HEAD_DIM)
        denom = acc_sc[j, HEAD_DIM:HEAD_DIM + 1, :]
        o = (acc_sc[j, :HEAD_DIM, :] * (1.0 / denom)).T
        o_ref[:, cols] = _rms(o, gain_ref[:, cols]).astype(o_ref.dtype)


def _attention(qt, k, vt, mix, l, tq):
    s = k.shape[0]
    hp = ATTN_HEADS_PER_STEP
    width = hp * HEAD_DIM
    return pl.pallas_call(
        functools.partial(_attn_kernel, tq=tq, hp=hp),
        grid=(N_MLA_HEADS // hp, s // tq),
        in_specs=[pl.BlockSpec((hp * QK_PAD, tq), lambda h, i: (h, i)),
                  pl.BlockSpec((s, hp * QK_PAD), lambda h, i: (0, h)),
                  pl.BlockSpec((s // tq, hp * VT_ROWS, tq), lambda h, i: (0, h, 0)),
                  pl.BlockSpec((None, 1, width), lambda h, i: (l, 0, RET_WIDTH // width + h))],
        out_specs=pl.BlockSpec((tq, width), lambda h, i: (i, h)),
        out_shape=jax.ShapeDtypeStruct((s, MLA_WIDTH), BF16),
        scratch_shapes=[pltpu.VMEM((hp, 1, tq), F32),
                        pltpu.VMEM((hp, VT_ROWS, tq), F32),
                        pltpu.VMEM((hp, tq, tq), F32), pltpu.VMEM((hp, tq, tq), F32)],
        compiler_params=_cparams("arbitrary", "arbitrary"),
        name="mla_attention",
    )(qt, k, vt, mix)


def _gmlp_kernel(u_ref, v_ref, gv_ref, ws_ref, bs_ref, gm_ref, o_ref, *, n_chunks):
    row = lax.broadcasted_iota(jnp.int32, (CHUNK, CHUNK), 0)
    col = lax.broadcasted_iota(jnp.int32, (CHUNK, CHUNK), 1)
    for g in range(N_GMLP_GROUPS):
        cols = slice(g * HEAD_DIM, (g + 1) * HEAD_DIM)
        w = jnp.where(col <= row, ws_ref[g], 0.0).astype(BF16)
        bias = bs_ref[g]
        for c in range(n_chunks):
            rows = slice(c * CHUNK, (c + 1) * CHUNK)
            u = jax.nn.gelu(u_ref[rows, cols].astype(F32))
            v = jax.nn.gelu(v_ref[rows, cols].astype(F32))
            vn = _rms(v, gv_ref[:, cols]).astype(BF16)
            sg = jnp.dot(w, vn, preferred_element_type=F32) + bias
            o_ref[rows, cols] = _rms(u * sg, gm_ref[:, cols]).astype(o_ref.dtype)
        yield


def _gmlp_part(zt, gv, ws, bs, mix, l, t):
    s = zt.shape[0]
    return (functools.partial(_gmlp_kernel, n_chunks=t // CHUNK),
            [zt, zt, gv, ws, bs, mix],
            [pl.BlockSpec((t, GMLP_WIDTH), lambda i: (i, ZT_GU // GMLP_WIDTH)),
             pl.BlockSpec((t, GMLP_WIDTH), lambda i: (i, ZT_GV // GMLP_WIDTH)),
             _layer_spec(gv, l, 1), _layer_spec(ws, l, 1), _layer_spec(bs, l, 1),
             pl.BlockSpec((None, 1, GMLP_WIDTH),
                          lambda i: (l, 0, (RET_WIDTH + MLA_WIDTH) // GMLP_WIDTH))],
            [pl.BlockSpec((t, GMLP_WIDTH), lambda i: (i, 0))],
            [jax.ShapeDtypeStruct((s, GMLP_WIDTH), BF16)],
            [])


def _cast_kernel(w_ref, o_ref):
    o_ref[...] = w_ref[...].astype(o_ref.dtype)


def _cast_part(w, l, n_tiles):
    _, r, c = w.shape
    assert r % n_tiles == 0
    rb = r // n_tiles
    return (_cast_kernel, [w],
            [pl.BlockSpec((None, rb, c), lambda i: (l, i, 0))],
            [pl.BlockSpec((rb, c), lambda i: (i, 0))],
            [jax.ShapeDtypeStruct((r, c), BF16)],
            [])


def _mixers_kernel(*refs, bodies, n_in, n_out, n_scratch):
    ins, outs, scr = [], [], []
    pos = 0
    for group, counts in ((ins, n_in), (outs, n_out), (scr, n_scratch)):
        for c in counts:
            group.append(refs[pos:pos + c])
            pos += c
    staged = [body(*i, *o, *sc) for body, i, o, sc in zip(bodies, ins, outs, scr)]
    staged = [g for g in staged if inspect.isgenerator(g)]
    while staged:
        for g in list(staged):
            if next(g, _DONE) is _DONE:
                staged.remove(g)


def _mixers(parts, n_tiles, name="mixers"):
    bodies = [p[0] for p in parts]
    flat = lambda k: [a for p in parts for a in p[k]]
    return pl.pallas_call(
        functools.partial(_mixers_kernel, bodies=bodies, n_in=[len(p[1]) for p in parts],
                          n_out=[len(p[3]) for p in parts], n_scratch=[len(p[5]) for p in parts]),
        grid=(n_tiles,),
        in_specs=flat(2), out_specs=flat(3), out_shape=flat(4), scratch_shapes=flat(5),
        compiler_params=_cparams("arbitrary"),
        name=name,
    )(*flat(1))


def _out_proj_kernel(x_ref, yr_ref, ym_ref, yg_ref, w_ref, o_ref):
    y = jnp.concatenate([yr_ref[...], ym_ref[...], yg_ref[...]], axis=-1)
    o_ref[...] = x_ref[...] + jnp.dot(y, w_ref[...], preferred_element_type=F32)


def _out_proj(x, yr, ym, yg, w, tm):
    s, d = x.shape
    blk = lambda n: pl.BlockSpec((tm, n), lambda i: (i, 0))
    return pl.pallas_call(
        _out_proj_kernel,
        grid=(s // tm,),
        in_specs=[blk(d), blk(RET_WIDTH), blk(MLA_WIDTH), blk(GMLP_WIDTH),
                  pl.BlockSpec(w.shape, lambda i: (0, 0), pipeline_mode=pl.Buffered(1))],
        out_specs=blk(d),
        out_shape=jax.ShapeDtypeStruct((s, d), F32),
        compiler_params=_cparams("arbitrary"),
        name="out_proj",
    )(x, yr, ym, yg, w)


def _ffn_kernel(x_ref, g_ref, wg_ref, wu_ref, wd_ref, fn_ref, o_ref, h_sc, *, final_norm):
    f = pl.program_id(1)

    @pl.when(f == 0)
    def _():
        x = x_ref[...]
        h_sc[...] = _rms(x, g_ref[...]).astype(BF16)
        o_ref[...] = x

    h = h_sc[...]
    gate = jnp.dot(h, wg_ref[...], preferred_element_type=F32)
    up = jnp.dot(h, wu_ref[...], preferred_element_type=F32)
    act = (gate / (1.0 + jnp.exp(-gate)) * up).astype(BF16)
    o_ref[...] += jnp.dot(act, wd_ref[...], preferred_element_type=F32)

    if final_norm:
        @pl.when(f == pl.num_programs(1) - 1)
        def _():
            o_ref[...] = _rms(o_ref[...], fn_ref[...])


def _ffn_cast_kernel(x_ref, g_ref, wg_ref, wu_ref, wd_ref, fn_ref, ng_ref, nu_ref, nd_ref,
                     o_ref, cg_ref, cu_ref, cd_ref, h_sc):
    cg_ref[...] = ng_ref[...].astype(BF16)
    cu_ref[...] = nu_ref[...].astype(BF16)
    cd_ref[...] = nd_ref[...].astype(BF16)
    _ffn_kernel(x_ref, g_ref, wg_ref, wu_ref, wd_ref, fn_ref, o_ref, h_sc, final_norm=False)


def _ffn(x, gain, wg, wu, wd, fn_gain, l, tm, tf, final_norm, next_f32=None):
    s, d = x.shape
    dff = wg.shape[1]
    ni, nf = s // tm, dff // tf
    in_specs = [pl.BlockSpec((tm, d), lambda i, f: (i, 0)),
                _layer_spec(gain, l, 2),
                pl.BlockSpec((d, tf), lambda i, f: (0, f)),
                pl.BlockSpec((d, tf), lambda i, f: (0, f)),
                pl.BlockSpec((tf, d), lambda i, f: (f, 0)),
                pl.BlockSpec((1, d), lambda i, f: (0, 0))]
    x_spec = pl.BlockSpec((tm, d), lambda i, f: (i, 0))
    x_shape = jax.ShapeDtypeStruct((s, d), F32)
    scratch = [pltpu.VMEM((tm, d), BF16)]
    params = _cparams("arbitrary", "arbitrary")
    if next_f32 is None:
        return pl.pallas_call(
            functools.partial(_ffn_kernel, final_norm=final_norm),
            grid=(ni, nf), in_specs=in_specs, out_specs=x_spec, out_shape=x_shape,
            scratch_shapes=scratch, compiler_params=params, name="ffn",
        )(x, gain, wg, wu, wd, fn_gain)
    assert not final_norm and d % ni == 0 and dff % nf == 0
    rb, cb = d // ni, dff // nf
    in_specs += [pl.BlockSpec((None, rb, cb), lambda i, f: (l + 1, i, f)),
                 pl.BlockSpec((None, rb, cb), lambda i, f: (l + 1, i, f)),
                 pl.BlockSpec((None, cb, rb), lambda i, f: (l + 1, f, i))]
    out_specs = [x_spec,
                 pl.BlockSpec((rb, cb), lambda i, f: (i, f)),
                 pl.BlockSpec((rb, cb), lambda i, f: (i, f)),
                 pl.BlockSpec((cb, rb), lambda i, f: (f, i))]
    out_shape = [x_shape, jax.ShapeDtypeStruct((d, dff), BF16), jax.ShapeDtypeStruct((d, dff), BF16),
                 jax.ShapeDtypeStruct((dff, d), BF16)]
    return pl.pallas_call(
        _ffn_cast_kernel,
        grid=(ni, nf), in_specs=in_specs, out_specs=out_specs, out_shape=out_shape,
        scratch_shapes=scratch, name="ffn_cast",
        compiler_params=_cparams("arbitrary", "arbitrary", vmem_limit=VMEM_LIMIT_FFN_CAST),
    )(x, gain, wg, wu, wd, fn_gain, *next_f32)


def _spread_rope_cols(w):
    z = jnp.zeros(w.shape[:-1] + (32,), w.dtype)
    return jnp.concatenate([w[..., :32], z, w[..., 32:], z], axis=-1)


def _prep_w_in(w_in):
    w_in_b = w_in.astype(BF16)
    kpe = _spread_rope_cols(w_in_b[..., Z_MAIN:Z_MAIN + QK_ROPE_DIM])
    tail = jnp.concatenate([w_in_b[..., Z_MAIN + QK_ROPE_DIM:], kpe], axis=-1)
    return w_in_b, tail


def _prep_w_uq(w_uq):
    l, r, _ = w_uq.shape
    w = w_uq.reshape(l, r, N_MLA_HEADS, QK_HEAD_DIM)
    w = jnp.concatenate([w[..., :QK_NOPE_DIM], _spread_rope_cols(w[..., QK_NOPE_DIM:])], axis=-1)
    return w.reshape(l, r, N_MLA_HEADS * QK_PAD).astype(BF16)


def _prep_w_ukv(w_ukv):
    l, r, _ = w_ukv.shape
    w = w_ukv.reshape(l, r, N_MLA_HEADS, 2 * HEAD_DIM)
    k_nope = w[..., :HEAD_DIM].reshape(l, r, MLA_WIDTH)
    v = w[..., HEAD_DIM:].reshape(l, r, MLA_WIDTH)
    return jnp.concatenate([k_nope, v], axis=-1).astype(BF16)


def _tiles(s):
    pick = lambda pref: pref if s % pref == 0 else s
    attn = pick(512)
    return dict(in_tm=pick(1024), mix_tm=attn, ret_chunk=128, attn_tq=attn,
                out_tm=pick(1024), ffn_tm=pick(1024), ffn_tf=512)


def kernel(x, positions, attn_norm, w_in, mla_q_norm, w_uq, mla_kv_norm, w_ukv, gmlp_v_norm, gmlp_w_s,
           gmlp_b_s, mix_norm, w_out, ffn_norm, w_gate, w_up, w_down, final_norm):
    b, s, d = x.shape
    assert b == 1, "batch is folded away; only BATCH == 1 is supported"
    depth = w_in.shape[0]
    t = _tiles(s)

    w_in_b, w_in_tail = _prep_w_in(w_in)
    w_uq_b = _prep_w_uq(w_uq)
    w_ukv_b = _prep_w_ukv(w_ukv)
    n_prep = 8 if s % 8 == 0 else 1
    cos_r, sin_r, cos_m, sin_m, w_gate_0, w_up_0 = _mixers(
        [_rope_tables_part(positions, n_prep), _cast_part(w_gate, 0, n_prep), _cast_part(w_up, 0, n_prep)],
        n_prep, name="prep")
    ffn_w = [w_gate_0, w_up_0, None]
    rows = lambda a: a.reshape(depth, 1, -1)
    attn_g, q_g, kv_g, gv_g, mix_g, ffn_g = map(
        rows, (attn_norm, mla_q_norm, mla_kv_norm, gmlp_v_norm, mix_norm, ffn_norm))
    b_s = gmlp_b_s[..., None]
    fn_g = final_norm.reshape(1, d)

    ret_tables = _retention_tables(t["ret_chunk"])
    xs = x.reshape(s, d)
    for l in range(depth):
        z, zt = _in_proj(xs, attn_g, w_in_b, w_in_tail, l, t["in_tm"])
        tm = t["mix_tm"]
        parts = [_retention_part(z, cos_r, sin_r, ret_tables, mix_g, l, tm, t["ret_chunk"]),
                 _gmlp_part(zt, gv_g, gmlp_w_s, b_s, mix_g, l, tm),
                 _mla_proj_part(z, zt, q_g, kv_g, w_uq_b, w_ukv_b, cos_m, sin_m, l, tm),
                 _cast_part(w_out, l, s // tm)]
        if l == 0:
            parts.append(_cast_part(w_down, 0, s // tm))
        y_ret, y_gm, qt, k, vt, w_out_l, *w_down_0 = _mixers(parts, s // tm)
        if l == 0:
            ffn_w[2] = w_down_0[0]
        y_mla = _attention(qt, k, vt, mix_g, l, t["attn_tq"])
        xs = _out_proj(xs, y_ret, y_mla, y_gm, w_out_l, t["out_tm"])
        if l < depth - 1:
            xs, *ffn_w = _ffn(xs, ffn_g, *ffn_w, fn_g, l, t["ffn_tm"], t["ffn_tf"], final_norm=False,
                              next_f32=(w_gate, w_up, w_down))
        else:
            xs = _ffn(xs, ffn_g, *ffn_w, fn_g, l, t["ffn_tm"], t["ffn_tf"], final_norm=True)
    return xs.reshape(b, s, d)
```
